```python
import math
import jax, jax.numpy as jnp
from jax import lax
import numpy as np

D_MODEL = 1024
BATCH = 16
SEQ = 256
DEPTH = 4
DEC_BATCH = 2
DEC_SEQ = 4096
PAST_LEN = 256

GRID_W = 64
HEAD_DIM = 64
SSM_WIDTH = D_MODEL // 4
SSM_GROUP = 16
N_SSM_GROUPS = SSM_WIDTH // SSM_GROUP
SSM_STATE = 64
ATT_WIDTH = D_MODEL // 2
N_HEADS = ATT_WIDTH // HEAD_DIM
N_KV_HEADS = N_HEADS // 4
KV_WIDTH = N_KV_HEADS * HEAD_DIM
RET_WIDTH = D_MODEL // 4
RET_HEADS = RET_WIDTH // HEAD_DIM
MIX_WIDTH = SSM_WIDTH + ATT_WIDTH + RET_WIDTH
IN_SIZES = (SSM_WIDTH, ATT_WIDTH, KV_WIDTH, KV_WIDTH, RET_WIDTH, RET_WIDTH, RET_WIDTH, RET_WIDTH)
IN_WIDTH = SSM_WIDTH + ATT_WIDTH + 2 * KV_WIDTH + 4 * RET_WIDTH
Q_BLOCK = 128
RET_CHUNK = 128
N_EXPERTS = 16
EXPERT_FF = 1024
CAPACITY_FACTOR = 2
ROPE_THETA = 10000.0
EPS = 1e-6

kernel_name = 'hybrid_s5_gqa_retention_ecmoe_denoise_step'


def rms_norm(x, w):
    xf = x.astype(jnp.float32)
    y = xf * lax.rsqrt(jnp.mean(xf * xf, axis=-1, keepdims=True) + EPS)
    return (y * w.astype(jnp.float32)).astype(x.dtype)


def axial_rope(n_tokens):
    rows = n_tokens // GRID_W
    row = jnp.repeat(jnp.arange(rows, dtype=jnp.float32), GRID_W)
    col = jnp.tile(jnp.arange(GRID_W, dtype=jnp.float32), rows)
    n_freq = HEAD_DIM // 4
    inv_freq = ROPE_THETA ** (-jnp.arange(n_freq, dtype=jnp.float32) / n_freq)
    ang = jnp.concatenate([row[:, None] * inv_freq, col[:, None] * inv_freq], axis=-1)
    return jnp.cos(ang), jnp.sin(ang)


def apply_rope(x, cos, sin):
    half = HEAD_DIM // 2
    xf = x.astype(jnp.float32)
    x1, x2 = xf[..., :half], xf[..., half:]
    c, s = cos[None, :, None, :], sin[None, :, None, :]
    return jnp.concatenate([x1 * c - x2 * s, x1 * s + x2 * c], axis=-1).astype(x.dtype)


def blocked_attention(q, k, v):
    b, lq, h, dh = q.shape
    kvh = k.shape[2]
    g = h // kvh
    nb = lq // Q_BLOCK
    qb = jnp.moveaxis(q.reshape(b, nb, Q_BLOCK, kvh, g, dh), 1, 0)
    scale = dh ** -0.5

    def one_block(qi):
        s = jnp.einsum('bqkgd,bskd->bkgqs', qi, k).astype(jnp.float32) * scale
        p = jax.nn.softmax(s, axis=-1).astype(v.dtype)
        return jnp.einsum('bkgqs,bskd->bqkgd', p, v)

    o = lax.map(one_block, qb)
    return jnp.moveaxis(o, 0, 1).reshape(b, lq, h * dh)


def diag_scan(lam_bar, bu, h0):
    bu = bu.at[:, 0].add(lam_bar * h0)
    a = jnp.broadcast_to(lam_bar, bu.shape)

    def combine(l, r):
        return (l[0] * r[0], r[0] * l[1] + r[1])

    _, h = lax.associative_scan(combine, (a, bu), axis=1)
    return h


def s5_mixer(u, lam_re, lam_im, b_re, b_im, c_re, c_im, log_dt, d, w_glu, h0, want_state):
    bsz, L, _ = u.shape
    ug = u.astype(jnp.float32).reshape(bsz, L, N_SSM_GROUPS, SSM_GROUP)
    y = ug * d.astype(jnp.float32)
    finals = []
    for di in range(2):
        lam = lax.complex(lam_re[di].astype(jnp.float32), lam_im[di].astype(jnp.float32))
        dt = jnp.exp(log_dt[di].astype(jnp.float32))[:, None]
        lam_bar = jnp.exp(lam * dt)
        bmat = lax.complex(b_re[di].astype(jnp.float32), b_im[di].astype(jnp.float32))
        b_bar = ((lam_bar - 1.0) / lam)[..., None] * bmat
        cmat = lax.complex(c_re[di].astype(jnp.float32), c_im[di].astype(jnp.float32))
        seq = ug if di == 0 else jnp.flip(ug, axis=1)
        bu = jnp.einsum('blgc,gpc->blgp', seq.astype(jnp.complex64), b_bar)
        h = diag_scan(lam_bar, bu, h0[:, di])
        if want_state:
            finals.append(h[:, -1])
        yd = jnp.real(jnp.einsum('blgp,gcp->blgc', h, cmat))
        y = y + (yd if di == 0 else jnp.flip(yd, axis=1))
    y = jax.nn.gelu(y.reshape(bsz, L, SSM_WIDTH))
    y = y * jax.nn.sigmoid(y @ w_glu.astype(jnp.float32))
    state = jnp.stack(finals, axis=1) if want_state else None
    return y.astype(u.dtype), state


def retention_chunkwise(q, k, v, log_gamma, s0):
    bsz, L, h, dh = q.shape
    n = L // RET_CHUNK

    def to_chunks(t):
        return t.reshape(bsz, n, RET_CHUNK, h, dh).transpose(1, 0, 3, 2, 4)

    qc, kc, vc = to_chunks(q), to_chunks(k), to_chunks(v)
    idx = jnp.arange(RET_CHUNK, dtype=jnp.float32)
    rel = idx[:, None] - idx[None, :]
    decay_in = jnp.where(rel >= 0, jnp.exp(log_gamma[:, None, None] * jnp.maximum(rel, 0.0)), 0.0)
    q_decay = jnp.exp(log_gamma[:, None] * (idx + 1.0))[None, :, :, None]
    k_decay = jnp.exp(log_gamma[:, None] * (RET_CHUNK - 1.0 - idx))[None, :, :, None]
    chunk_decay = jnp.exp(log_gamma * RET_CHUNK)[None, :, None, None]

    def step(state, inp):
        qi, ki, vi = inp
        inner = jnp.einsum('bhqd,bhkd->bhqk', qi, ki) * decay_in
        o = (jnp.einsum('bhqk,bhkv->bhqv', inner, vi)
             + jnp.einsum('bhqd,bhdv->bhqv', qi * q_decay, state))
        state = chunk_decay * state + jnp.einsum('bhkd,bhkv->bhdv', ki * k_decay, vi)
        return state, o

    s_final, o = lax.scan(step, s0, (qc, kc, vc))
    return o.transpose(1, 0, 3, 2, 4).reshape(bsz, L, h, dh), s_final


def retention_mixer(rq, rk, rv, rg, decay_logit, norm_w, s0):
    bsz, L = rq.shape[:2]
    q = rq.astype(jnp.float32)
    k = rk.astype(jnp.float32) * (HEAD_DIM ** -0.5)
    v = rv.astype(jnp.float32).reshape(bsz, L, RET_HEADS, HEAD_DIM)
    log_gamma = jax.nn.log_sigmoid(decay_logit.astype(jnp.float32))
    o_f, s_f = retention_chunkwise(q, k, v, log_gamma[0], s0[:, 0])
    o_b, s_b = retention_chunkwise(jnp.flip(q, 1), jnp.flip(k, 1), jnp.flip(v, 1), log_gamma[1], s0[:, 1])
    o = o_f + jnp.flip(o_b, 1)
    mu = jnp.mean(o, axis=-1, keepdims=True)
    var = jnp.mean(jnp.square(o - mu), axis=-1, keepdims=True)
    o = ((o - mu) * lax.rsqrt(var + EPS)).reshape(bsz, L, RET_WIDTH) * norm_w.astype(jnp.float32)
    out = jax.nn.silu(rg.astype(jnp.float32)) * o
    return out.astype(rg.dtype), jnp.stack([s_f, s_b], axis=1)


def expert_choice_ffn(h, w_router, w_gate, w_up, w_down):
    bsz, L, d = h.shape
    cap = CAPACITY_FACTOR * L // N_EXPERTS
    affinity = jax.nn.softmax((h @ w_router).astype(jnp.float32), axis=-1)
    gates, idx = lax.top_k(jnp.swapaxes(affinity, 1, 2), cap)
    xs = jax.vmap(lambda hb, ib: hb[ib])(h, idx)
    a = jnp.einsum('becd,edf->becf', xs, w_gate)
    up = jnp.einsum('becd,edf->becf', xs, w_up)
    y = jnp.einsum('becf,efd->becd', jax.nn.silu(a) * up, w_down) * gates[..., None].astype(h.dtype)
    return jax.vmap(lambda ib, yb: jnp.zeros((L, d), yb.dtype).at[ib.reshape(-1)].add(yb.reshape(-1, d)))(idx, y)


def trunk_layer(x, cond, lp, rope, ctx):
    bsz, L, _ = x.shape
    want_state = ctx is None
    mod = jax.nn.silu(cond.astype(jnp.float32)) @ lp['w_mod'].astype(jnp.float32) + lp['b_mod'].astype(jnp.float32)
    shift1, scale1, gate1, shift2, scale2, gate2 = jnp.split(mod[:, None, :].astype(x.dtype), 6, axis=-1)

    h = rms_norm(x, lp['norm1_w']) * (1.0 + scale1) + shift1
    proj = h @ lp['w_in']
    parts, off = [], 0
    for size in IN_SIZES:
        parts.append(proj[..., off:off + size])
        off += size
    u, q, k, v, rq, rk, rv, rg = parts

    q = rms_norm(q.reshape(bsz, L, N_HEADS, HEAD_DIM), lp['qn_w'])
    k = rms_norm(k.reshape(bsz, L, N_KV_HEADS, HEAD_DIM), lp['kn_w'])
    v = v.reshape(bsz, L, N_KV_HEADS, HEAD_DIM)
    rq = rq.reshape(bsz, L, RET_HEADS, HEAD_DIM)
    rk = rk.reshape(bsz, L, RET_HEADS, HEAD_DIM)

    if ctx is None:
        keys, vals = k, v
        h_ssm0 = jnp.zeros((bsz, 2, N_SSM_GROUPS, SSM_STATE), jnp.complex64)
        s_ret0 = jnp.zeros((bsz, 2, RET_HEADS, HEAD_DIM, HEAD_DIM), jnp.float32)
    else:
        k_ctx, v_ctx, ssm_state, ret_state = ctx
        cos, sin = rope
        q, k = apply_rope(q, cos, sin), apply_rope(k, cos, sin)
        rq, rk = apply_rope(rq, cos, sin), apply_rope(rk, cos, sin)
        keys = jnp.concatenate([k, k_ctx.astype(k.dtype)], axis=1)
        vals = jnp.concatenate([v, v_ctx.astype(v.dtype)], axis=1)
        h_ssm0 = lax.complex(ssm_state[..., 0].astype(jnp.float32), ssm_state[..., 1].astype(jnp.float32))
        s_ret0 = ret_state.astype(jnp.float32)

    attn_out = blocked_attention(q, keys, vals)
    ssm_out, ssm_final = s5_mixer(u, lp['ssm_lambda_re'], lp['ssm_lambda_im'], lp['ssm_b_re'], lp['ssm_b_im'],
                                  lp['ssm_c_re'], lp['ssm_c_im'], lp['ssm_log_dt'], lp['ssm_d'], lp['ssm_w_glu'],
                                  h_ssm0, want_state)
    ret_out, ret_final = retention_mixer(rq, rk, rv, rg, lp['ret_decay_logit'], lp['ret_norm_w'], s_ret0)
    mix = jnp.concatenate([ssm_out, attn_out.astype(x.dtype), ret_out], axis=-1) @ lp['w_out']
    x = x + gate1 * mix.astype(x.dtype)

    h = rms_norm(x, lp['norm2_w']) * (1.0 + scale2) + shift2
    x = x + gate2 * expert_choice_ffn(h, lp['w_router'], lp['w_gate'], lp['w_up'], lp['w_down']).astype(x.dtype)

    if ctx is None:
        ssm_real = jnp.stack([jnp.real(ssm_final), jnp.imag(ssm_final)], axis=-1).astype(x.dtype)
        return x, (k, v, ssm_real, ret_final.astype(x.dtype))
    return x, None


def setup_inputs(seed: int = 0) -> dict:
    key = jax.random.key(seed)
    keys = iter(jax.random.split(key, 40))

    def nrm(shape, scale):
        return scale * jax.random.normal(next(keys), shape, jnp.float32)

    d = D_MODEL
    ssm_shape = (DEPTH, 2, N_SSM_GROUPS, SSM_STATE)
    return {
        'x_prompt': nrm((BATCH, SEQ, d), 1.0),
        'x_sample': nrm((DEC_BATCH, DEC_SEQ, d), 1.0),
        'cache_k': nrm((DEC_BATCH, DEPTH, PAST_LEN, N_KV_HEADS, HEAD_DIM), 1.0),
        'cache_v': nrm((DEC_BATCH, DEPTH, PAST_LEN, N_KV_HEADS, HEAD_DIM), 1.0),
        'state_ssm': nrm((DEC_BATCH, DEPTH, 2, N_SSM_GROUPS, SSM_STATE, 2), 0.1),
        'state_ret': nrm((DEC_BATCH, DEPTH, 2, RET_HEADS, HEAD_DIM, HEAD_DIM), 0.5),
        'c': nrm((DEC_BATCH, d), 1.0),
        'c_ctx': nrm((d,), 1.0),
        'w_mod': nrm((DEPTH, d, 6 * d), 0.5 * d ** -0.5),
        'b_mod': nrm((DEPTH, 6 * d), 0.01),
        'norm1_w': 1.0 + nrm((DEPTH, d), 0.02),
        'norm2_w': 1.0 + nrm((DEPTH, d), 0.02),
        'w_in': nrm((DEPTH, d, IN_WIDTH), d ** -0.5),
        'w_out': nrm((DEPTH, MIX_WIDTH, d), MIX_WIDTH ** -0.5),
        'qn_w': 1.0 + nrm((DEPTH, HEAD_DIM), 0.02),
        'kn_w': 1.0 + nrm((DEPTH, HEAD_DIM), 0.02),
        'ssm_lambda_re': -0.5 + nrm(ssm_shape, 0.01),
        'ssm_lambda_im': math.pi * jnp.arange(SSM_STATE, dtype=jnp.float32) + nrm(ssm_shape, 0.01),
        'ssm_b_re': nrm((DEPTH, 2, N_SSM_GROUPS, SSM_STATE, SSM_GROUP), (2 * SSM_GROUP) ** -0.5),
        'ssm_b_im': nrm((DEPTH, 2, N_SSM_GROUPS, SSM_STATE, SSM_GROUP), (2 * SSM_GROUP) ** -0.5),
        'ssm_c_re': nrm((DEPTH, 2, N_SSM_GROUPS, SSM_GROUP, SSM_STATE), SSM_STATE ** -0.5),
        'ssm_c_im': nrm((DEPTH, 2, N_SSM_GROUPS, SSM_GROUP, SSM_STATE), SSM_STATE ** -0.5),
        'ssm_log_dt': jax.random.uniform(next(keys), (DEPTH, 2, N_SSM_GROUPS), jnp.float32,
                                         math.log(1e-3), math.log(1e-1)),
        'ssm_d': nrm((DEPTH, N_SSM_GROUPS, SSM_GROUP), 1.0),
        'ssm_w_glu': nrm((DEPTH, SSM_WIDTH, SSM_WIDTH), SSM_WIDTH ** -0.5),
        'ret_decay_logit': jnp.log(2.0 ** (5.0 + jnp.arange(RET_HEADS, dtype=jnp.float32)) - 1.0)
                           + nrm((DEPTH, 2, RET_HEADS), 0.01),
        'ret_norm_w': 1.0 + nrm((DEPTH, RET_WIDTH), 0.02),
        'w_router': nrm((DEPTH, d, N_EXPERTS), d ** -0.5),
        'w_gate': nrm((DEPTH, N_EXPERTS, d, EXPERT_FF), d ** -0.5),
        'w_up': nrm((DEPTH, N_EXPERTS, d, EXPERT_FF), d ** -0.5),
        'w_down': nrm((DEPTH, N_EXPERTS, EXPERT_FF, d), EXPERT_FF ** -0.5),
        'final_norm_w': 1.0 + nrm((d,), 0.02),
    }


def reference(x_prompt, x_sample, cache_k, cache_v, state_ssm, state_ret, c, c_ctx,
              w_mod, b_mod, norm1_w, norm2_w, w_in, w_out, qn_w, kn_w,
              ssm_lambda_re, ssm_lambda_im, ssm_b_re, ssm_b_im, ssm_c_re, ssm_c_im,
              ssm_log_dt, ssm_d, ssm_w_glu, ret_decay_logit, ret_norm_w,
              w_router, w_gate, w_up, w_down, final_norm_w):
    per_layer = {
        'w_mod': w_mod, 'b_mod': b_mod, 'norm1_w': norm1_w, 'norm2_w': norm2_w,
        'w_in': w_in, 'w_out': w_out, 'qn_w': qn_w, 'kn_w': kn_w,
        'ssm_lambda_re': ssm_lambda_re, 'ssm_lambda_im': ssm_lambda_im,
        'ssm_b_re': ssm_b_re, 'ssm_b_im': ssm_b_im, 'ssm_c_re': ssm_c_re, 'ssm_c_im': ssm_c_im,
        'ssm_log_dt': ssm_log_dt, 'ssm_d': ssm_d, 'ssm_w_glu': ssm_w_glu,
        'ret_decay_logit': ret_decay_logit, 'ret_norm_w': ret_norm_w,
        'w_router': w_router, 'w_gate': w_gate, 'w_up': w_up, 'w_down': w_down,
    }
    cond_ctx = jnp.broadcast_to(c_ctx, (x_prompt.shape[0], c_ctx.shape[0]))
    rope = axial_rope(x_sample.shape[1])
    xp, xs = x_prompt, x_sample
    ks, vs, ss, rs = [], [], [], []
    for l in range(DEPTH):
        lp = {name: arr[l] for name, arr in per_layer.items()}
        xp, (k_l, v_l, s_l, r_l) = trunk_layer(xp, cond_ctx, lp, None, None)
        ks.append(k_l)
        vs.append(v_l)
        ss.append(s_l)
        rs.append(r_l)
        xs, _ = trunk_layer(xs, c, lp, rope,
                            (cache_k[:, l], cache_v[:, l], state_ssm[:, l], state_ret[:, l]))
    y_prompt = rms_norm(xp, final_norm_w)
    y_sample = rms_norm(xs, final_norm_w)
    return (y_prompt, y_sample, jnp.stack(ks, axis=1), jnp.stack(vs, axis=1),
            jnp.stack(ss, axis=1), jnp.stack(rs, axis=1))
```

```python
import functools

import jax
import jax.numpy as jnp
from jax import lax
from jax.experimental import pallas as pl
from jax.experimental.pallas import tpu as pltpu

F32 = jnp.float32
BF16 = jnp.bfloat16

D_MODEL = 1024
BATCH = 16
SEQ = 256
DEPTH = 4
DEC_BATCH = 2
DEC_SEQ = 4096
PAST_LEN = 256
GRID_W = 64
HEAD_DIM = 64
SSM_WIDTH = 256
SSM_GROUP = 16
N_SSM_GROUPS = 16
SSM_STATE = 64
ATT_WIDTH = 512
N_HEADS = 8
N_KV_HEADS = 2
KV_WIDTH = 128
RET_WIDTH = 256
RET_HEADS = 4
IN_WIDTH = 2048
RET_CHUNK = 128
N_EXPERTS = 16
EXPERT_FF = 1024
ROPE_THETA = 10000.0
EPS = 1e-6

N_CTX = BATCH * SEQ
N_LAT = DEC_BATCH * DEC_SEQ
N_TOK = N_CTX + N_LAT
TILE = 256
N_TILES = N_TOK // TILE
CTX_TILES = N_CTX // TILE
LAT_TILES_PER_REQ = DEC_SEQ // TILE
CAP_CTX = 2 * SEQ // N_EXPERTS
CAP_LAT = 2 * DEC_SEQ // N_EXPERTS
S5_CHUNK = 16
CTX_CHUNKS = SEQ // S5_CHUNK
LAT_CHUNKS = DEC_SEQ // S5_CHUNK
S5_ROWS_CTX = BATCH * CTX_CHUNKS
S5_ROWS = S5_ROWS_CTX + DEC_BATCH * LAT_CHUNKS
C_U, C_K, C_V, C_RQ, C_RK, C_RV, C_RG = 0, 256, 384, 512, 768, 1024, 1280
REST_WIDTH = 1536
ROUTE_ITERS = 48
VMEM_LIMIT = 56 * 1024 * 1024

TN = (((0,), (0,)), ((), ()))
NT = (((1,), (1,)), ((), ()))


def _cparams(sem):
    return pltpu.CompilerParams(dimension_semantics=sem, vmem_limit_bytes=VMEM_LIMIT)


def _tile_mod_row(i):
    return jnp.where(i < CTX_TILES, 0, 1 + (i - CTX_TILES) // LAT_TILES_PER_REQ)


def _split_dot(v, m):
    hi = v.astype(BF16)
    lo = (v - hi.astype(F32)).astype(BF16)
    return (jnp.dot(hi, m, preferred_element_type=F32)
            + jnp.dot(lo, m, preferred_element_type=F32))


def _group_avg_matrix():
    r = lax.broadcasted_iota(jnp.int32, (128, 128), 0) // HEAD_DIM
    c = lax.broadcasted_iota(jnp.int32, (128, 128), 1) // HEAD_DIM
    return jnp.where(r == c, 1.0 / HEAD_DIM, 0.0).astype(BF16)


def _mod_kernel(ct_ref, w_ref, b_ref, o_ref):
    c = ct_ref[...]
    s = c * jax.nn.sigmoid(c)
    w = w_ref[...]
    rows = [jnp.sum(w * s[:, r:r + 1], axis=0, keepdims=True) for r in range(3)]
    rows.append(jnp.zeros((5, w.shape[1]), F32))
    o_ref[...] = jnp.concatenate(rows, axis=0) + b_ref[...]


def _modulation(cond_t, w_mod, b_mod):
    tn = 512
    n = 6 * D_MODEL
    return pl.pallas_call(
        _mod_kernel,
        grid=(DEPTH, n // tn),
        in_specs=[pl.BlockSpec((D_MODEL, 8), lambda l, j: (0, 0)),
                  pl.BlockSpec((None, D_MODEL, tn), lambda l, j: (l, 0, j)),
                  pl.BlockSpec((None, 1, tn), lambda l, j: (l, 0, j))],
        out_specs=pl.BlockSpec((None, 8, tn), lambda l, j: (l, 0, j)),
        out_shape=jax.ShapeDtypeStruct((DEPTH, 8, n), F32),
        compiler_params=_cparams(("arbitrary", "arbitrary")),
        name="modulation",
    )(cond_t, w_mod, b_mod.reshape(DEPTH, 1, n))


def _inproj_kernel(x_ref, mod_ref, nw_ref, w_ref, qn_ref, kn_ref, cos_ref, sin_ref, q_ref, kv_ref, rest_ref):
    x = x_ref[...]
    shift = mod_ref[:, 0:D_MODEL]
    scale = mod_ref[:, D_MODEL:2 * D_MODEL]
    y = x * lax.rsqrt(jnp.mean(x * x, axis=-1, keepdims=True) + EPS) * nw_ref[...]
    h = y * (1.0 + scale) + shift
    proj = jnp.dot(h.astype(BF16), w_ref[...], preferred_element_type=F32)

    avg = _group_avg_matrix()
    cos = cos_ref[...]
    sin = sin_ref[...]
    first_half = (lax.broadcasted_iota(jnp.int32, (TILE, 128), 1) % HEAD_DIM) < (HEAD_DIM // 2)

    def head_norm(z, wrow):
        return z * lax.rsqrt(_split_dot(z * z, avg) + EPS) * wrow

    def rope(z):
        partner = jnp.where(first_half, pltpu.roll(z, 128 - HEAD_DIM // 2, 1), pltpu.roll(z, HEAD_DIM // 2, 1))
        return z * cos + partner * sin

    def col(off, j):
        return proj[:, off + 128 * j: off + 128 * (j + 1)]

    qn = qn_ref[...]
    for j in range(ATT_WIDTH // 128):
        z = rope(head_norm(col(SSM_WIDTH, j), qn)) * (HEAD_DIM ** -0.5)
        q_ref[:, 128 * j:128 * (j + 1)] = z.astype(q_ref.dtype)
    p_k = SSM_WIDTH + ATT_WIDTH
    rest_ref[:, C_U:C_U + SSM_WIDTH] = proj[:, 0:SSM_WIDTH]
    k = rope(head_norm(col(p_k, 0), kn_ref[...]))
    v = col(p_k + KV_WIDTH, 0)
    rest_ref[:, C_K:C_K + KV_WIDTH] = k
    rest_ref[:, C_V:C_V + KV_WIDTH] = v
    kv_ref[:, 0:KV_WIDTH] = k.astype(kv_ref.dtype)
    kv_ref[:, KV_WIDTH:2 * KV_WIDTH] = v.astype(kv_ref.dtype)
    p_r = p_k + 2 * KV_WIDTH
    for j in range(RET_WIDTH // 128):
        rest_ref[:, C_RQ + 128 * j:C_RQ + 128 * (j + 1)] = rope(col(p_r, j))
        rest_ref[:, C_RK + 128 * j:C_RK + 128 * (j + 1)] = rope(col(p_r + RET_WIDTH, j)) * (HEAD_DIM ** -0.5)
    rest_ref[:, C_RV:C_RV + 2 * RET_WIDTH] = proj[:, p_r + 2 * RET_WIDTH:p_r + 4 * RET_WIDTH]


def _in_projection(x, mod_l, nw, w_in_bf, qn, kn, cos_t, sin_t):
    def rope_blk(i):
        return (jnp.where(i < CTX_TILES, 0, 1 + (i - CTX_TILES) % LAT_TILES_PER_REQ), 0)
    return pl.pallas_call(
        _inproj_kernel,
        grid=(N_TILES,),
        in_specs=[pl.BlockSpec((TILE, D_MODEL), lambda i: (i, 0)),
                  pl.BlockSpec((None, 1, 6 * D_MODEL), lambda i: (_tile_mod_row(i), 0, 0)),
                  pl.BlockSpec((1, D_MODEL), lambda i: (0, 0)),
                  pl.BlockSpec((D_MODEL, IN_WIDTH), lambda i: (0, 0)),
                  pl.BlockSpec((1, 128), lambda i: (0, 0)),
                  pl.BlockSpec((1, 128), lambda i: (0, 0)),
                  pl.BlockSpec((TILE, 128), rope_blk),
                  pl.BlockSpec((TILE, 128), rope_blk)],
        out_specs=[pl.BlockSpec((TILE, ATT_WIDTH), lambda i: (i, 0)),
                   pl.BlockSpec((TILE, 2 * KV_WIDTH), lambda i: (i, 0)),
                   pl.BlockSpec((TILE, REST_WIDTH), lambda i: (i, 0))],
        out_shape=[jax.ShapeDtypeStruct((N_TOK, ATT_WIDTH), BF16),
                   jax.ShapeDtypeStruct((N_TOK, 2 * KV_WIDTH), BF16),
                   jax.ShapeDtypeStruct((N_TOK, REST_WIDTH), F32)],
        compiler_params=_cparams(("parallel",)),
        name="in_projection",
    )(x, mod_l, nw, w_in_bf, qn, kn, cos_t, sin_t)


def _attn_kernel(*refs):
    q_ref, k_ref, v_ref, o_ref = refs[0], refs[1], refs[2], refs[-1]
    group = N_HEADS // N_KV_HEADS
    for kv in range(N_KV_HEADS):
        k = k_ref[:, kv * HEAD_DIM:(kv + 1) * HEAD_DIM]
        v = v_ref[:, kv * HEAD_DIM:(kv + 1) * HEAD_DIM]
        for g in range(group):
            cols = slice((kv * group + g) * HEAD_DIM, (kv * group + g + 1) * HEAD_DIM)
            s = lax.dot_general(q_ref[:, cols], k, NT, preferred_element_type=F32)
            m = jnp.max(s, axis=-1, keepdims=True)
            p = jnp.exp(s - m)
            l = jnp.sum(p, axis=-1, keepdims=True)
            o = jnp.dot(p.astype(BF16), v, preferred_element_type=F32) / l
            o_ref[:, cols] = o.astype(o_ref.dtype)


def _attention(q, kv, row_block0, n_req, lq, tq, prev=None):
    lk = kv.shape[1]
    nq = lq // tq
    tok = pl.BlockSpec((tq, ATT_WIDTH), lambda r, i: (row_block0 + r * nq + i, 0))
    in_specs = [tok,
                pl.BlockSpec((None, lk, KV_WIDTH), lambda r, i: (r, 0, 0)),
                pl.BlockSpec((None, lk, KV_WIDTH), lambda r, i: (r, 0, 1))]
    args = [q, kv, kv]
    aliases = {}
    if prev is not None:
        in_specs.append(pl.BlockSpec(memory_space=pl.ANY))
        args.append(prev)
        aliases = {3: 0}
    return pl.pallas_call(
        _attn_kernel,
        grid=(n_req, nq),
        in_specs=in_specs,
        out_specs=tok,
        out_shape=jax.ShapeDtypeStruct((N_TOK, ATT_WIDTH), BF16),
        input_output_aliases=aliases,
        compiler_params=_cparams(("parallel", "parallel")),
        name="attention",
    )(*args)


def _s5_local_kernel(u_ref, w_ref, y_ref, sf_ref, sfs_ref, sb_ref, sbs_ref):
    r = jnp.dot(u_ref[...].astype(BF16), w_ref[...], preferred_element_type=F32)
    y_ref[...] = r[:, 0:256]
    sf_ref[...] = r[:, 256:384]
    sfs_ref[...] = r[:, 384:512]
    sb_ref[...] = r[:, 512:640]
    sbs_ref[...] = r[:, 640:768]


def _s5_local(u_g, w1):
    g = N_SSM_GROUPS
    st = jax.ShapeDtypeStruct((S5_ROWS, g * 128), F32)
    sspec = pl.BlockSpec((S5_ROWS, 128), lambda i: (0, i))
    return pl.pallas_call(
        _s5_local_kernel,
        grid=(g,),
        in_specs=[pl.BlockSpec((None, S5_ROWS, 256), lambda i: (i, 0, 0)),
                  pl.BlockSpec((None, 256, 768), lambda i: (i, 0, 0))],
        out_specs=[pl.BlockSpec((S5_ROWS, 256), lambda i: (0, i)), sspec, sspec, sspec, sspec],
        out_shape=[jax.ShapeDtypeStruct((S5_ROWS, g * 256), F32), st, st, st, st],
        compiler_params=_cparams(("parallel",)),
        name="s5_local",
    )(u_g, w1)


def _s5_scan_kernel(sf_ref, sfs_ref, sb_ref, sbs_ref, a_ref, h0_ref, hf_ref, hb_ref, ff_ref, fb_ref):
    a1f, a2f, a1b, a2b = a_ref[0:1, :], a_ref[1:2, :], a_ref[2:3, :], a_ref[3:4, :]
    w = sf_ref.shape[1]

    hf = jnp.zeros((BATCH, w), F32)
    hfs, hb, hbs = hf, hf, hf
    for j in range(CTX_CHUNKS):
        rows = slice(j * BATCH, (j + 1) * BATCH)
        hf_ref[rows, :] = hf
        hf, hfs = (a1f * hf + a2f * hfs + sf_ref[rows, :], a1f * hfs - a2f * hf + sfs_ref[rows, :])
        jb = CTX_CHUNKS - 1 - j
        rows = slice(jb * BATCH, (jb + 1) * BATCH)
        hb_ref[rows, :] = hb
        hb, hbs = (a1b * hb + a2b * hbs + sb_ref[rows, :], a1b * hbs - a2b * hb + sbs_ref[rows, :])
    ff_ref[...] = hf
    fb_ref[...] = hb

    def body(j, carry):
        out = []
        for r in range(DEC_BATCH):
            hf, hfs, hb, hbs = carry[4 * r:4 * r + 4]
            row = pl.ds(S5_ROWS_CTX + r * LAT_CHUNKS + j, 1)
            hf_ref[row, :] = hf
            nf = a1f * hf + a2f * hfs + sf_ref[row, :]
            nfs = a1f * hfs - a2f * hf + sfs_ref[row, :]
            row = pl.ds(S5_ROWS_CTX + r * LAT_CHUNKS + LAT_CHUNKS - 1 - j, 1)
            hb_ref[row, :] = hb
            nb = a1b * hb + a2b * hbs + sb_ref[row, :]
            nbs = a1b * hbs - a2b * hb + sbs_ref[row, :]
            out += [nf, nfs, nb, nbs]
        return tuple(out)

    init = tuple(h0_ref[i:i + 1, :] for i in range(4 * DEC_BATCH))
    lax.fori_loop(0, LAT_CHUNKS, body, init)


def _s5_scan(sf, sfs, sb, sbs, a_rows, h0_rows):
    w = 512
    full = N_SSM_GROUPS * 128
    sspec = pl.BlockSpec((S5_ROWS, w), lambda i: (0, i))
    fspec = pl.BlockSpec((BATCH, w), lambda i: (0, i))
    return pl.pallas_call(
        _s5_scan_kernel,
        grid=(full // w,),
        in_specs=[sspec, sspec, sspec, sspec,
                  pl.BlockSpec((8, w), lambda i: (0, i)),
                  pl.BlockSpec((8, w), lambda i: (0, i))],
        out_specs=[sspec, sspec, fspec, fspec],
        out_shape=[jax.ShapeDtypeStruct((S5_ROWS, full), F32)] * 2 + [jax.ShapeDtypeStruct((BATCH, full), F32)] * 2,
        compiler_params=_cparams(("parallel",)),
        name="s5_scan",
    )(sf, sfs, sb, sbs, a_rows, h0_rows)


def _s5_out_kernel(y_ref, hf_ref, hb_ref, w_ref, o_ref):
    o_ref[...] = (y_ref[...]
                  + jnp.dot(hf_ref[...].astype(BF16), w_ref[0:128, :], preferred_element_type=F32)
                  + jnp.dot(hb_ref[...].astype(BF16), w_ref[128:256, :], preferred_element_type=F32))


def _s5_out(y_loc, hf, hb, w2):
    g = N_SSM_GROUPS
    return pl.pallas_call(
        _s5_out_kernel,
        grid=(g,),
        in_specs=[pl.BlockSpec((S5_ROWS, 256), lambda i: (0, i)),
                  pl.BlockSpec((S5_ROWS, 128), lambda i: (0, i)),
                  pl.BlockSpec((S5_ROWS, 128), lambda i: (0, i)),
                  pl.BlockSpec((None, 256, 256), lambda i: (i, 0, 0))],
        out_specs=pl.BlockSpec((S5_ROWS, 256), lambda i: (0, i)),
        out_shape=jax.ShapeDtypeStruct((S5_ROWS, g * 256), F32),
        compiler_params=_cparams(("parallel",)),
        name="s5_out",
    )(y_loc, hf, hb, w2)


def _s5_matrices(lam_re, lam_im, b_re, b_im, c_re, c_im, log_dt):
    hp = lax.Precision.HIGHEST
    n = S5_CHUNK
    tau = jnp.arange(n + 1, dtype=F32)
    w1_parts, w2_parts, a_rows = {}, {}, []
    for di in range(2):
        lr, li = lam_re[di].astype(F32), lam_im[di].astype(F32)
        dt = jnp.exp(log_dt[di].astype(F32))[:, None]
        mag = jnp.exp(lr * dt * tau[:, None, None])
        ang = li * dt * tau[:, None, None]
        e_re, e_im = mag * jnp.cos(ang), mag * jnp.sin(ang)
        nr, ni = e_re[1] - 1.0, e_im[1]
        den = lr * lr + li * li
        f_re, f_im = (nr * lr + ni * li) / den, (ni * lr - nr * li) / den
        br, bi = b_re[di].astype(F32), b_im[di].astype(F32)
        bb_re = f_re[..., None] * br - f_im[..., None] * bi
        bb_im = f_re[..., None] * bi + f_im[..., None] * br
        cr, ci = c_re[di].astype(F32), c_im[di].astype(F32)
        ce_re = cr[None] * e_re[:n, :, None, :] - ci[None] * e_im[:n, :, None, :]
        ce_im = cr[None] * e_im[:n, :, None, :] + ci[None] * e_re[:n, :, None, :]
        kern = (jnp.einsum('tgcp,gpd->tgcd', ce_re, bb_re, precision=hp)
                - jnp.einsum('tgcp,gpd->tgcd', ce_im, bb_im, precision=hp))
        kp = jnp.concatenate([jnp.zeros_like(kern), kern], axis=0)
        if di == 0:
            toe = jnp.stack([kp[n - s:2 * n - s] for s in range(n)], axis=0)
        else:
            toe = jnp.stack([kp[s + 1:s + n + 1][::-1] for s in range(n)], axis=0)
        toe = toe.transpose(2, 0, 4, 1, 3).reshape(N_SSM_GROUPS, n * SSM_GROUP, n * SSM_GROUP)
        pe_re, pe_im = (e_re[:n][::-1], e_im[:n][::-1]) if di == 0 else (e_re[:n], e_im[:n])
        inj_re = pe_re[..., None] * bb_re[None] - pe_im[..., None] * bb_im[None]
        inj_im = pe_re[..., None] * bb_im[None] + pe_im[..., None] * bb_re[None]
        inj_re = inj_re.transpose(1, 0, 3, 2).reshape(N_SSM_GROUPS, n * SSM_GROUP, SSM_STATE)
        inj_im = inj_im.transpose(1, 0, 3, 2).reshape(N_SSM_GROUPS, n * SSM_GROUP, SSM_STATE)
        w1_parts[di] = (toe, jnp.concatenate([inj_re, inj_im], -1), jnp.concatenate([inj_im, inj_re], -1))
        qe_re, qe_im = (e_re[1:], e_im[1:]) if di == 0 else (e_re[1:][::-1], e_im[1:][::-1])
        ro_re = cr[None] * qe_re[:, :, None, :] - ci[None] * qe_im[:, :, None, :]
        ro_im = cr[None] * qe_im[:, :, None, :] + ci[None] * qe_re[:, :, None, :]
        ro_re = ro_re.transpose(1, 3, 0, 2).reshape(N_SSM_GROUPS, SSM_STATE, n * SSM_GROUP)
        ro_im = ro_im.transpose(1, 3, 0, 2).reshape(N_SSM_GROUPS, SSM_STATE, n * SSM_GROUP)
        w2_parts[di] = jnp.concatenate([ro_re, -ro_im], axis=1)
        a_re, a_im = e_re[n], e_im[n]
        a_rows.append(jnp.concatenate([a_re, a_re], -1).reshape(1, -1))
        a_rows.append(jnp.concatenate([-a_im, a_im], -1).reshape(1, -1))
    w1 = jnp.concatenate([w1_parts[0][0] + w1_parts[1][0], w1_parts[0][1], w1_parts[0][2],
                          w1_parts[1][1], w1_parts[1][2]], axis=-1).astype(BF16)
    w2 = jnp.concatenate([w2_parts[0], w2_parts[1]], axis=1).astype(BF16)
    a_rows = jnp.concatenate(a_rows + [jnp.zeros((4, N_SSM_GROUPS * 128), F32)], axis=0)
    return w1, w2, a_rows


def _s5_to_groups(u):
    g, c, t = N_SSM_GROUPS, SSM_GROUP, S5_CHUNK
    uc = u[:N_CTX].reshape(BATCH, CTX_CHUNKS, t, g, c).transpose(3, 1, 0, 2, 4).reshape(g, S5_ROWS_CTX, t * c)
    ul = u[N_CTX:].reshape(DEC_BATCH, LAT_CHUNKS, t, g, c).transpose(3, 0, 1, 2, 4).reshape(g, -1, t * c)
    return jnp.concatenate([uc, ul], axis=1)


def _s5_from_groups(y):
    g, c, t = N_SSM_GROUPS, SSM_GROUP, S5_CHUNK
    yc = y[:S5_ROWS_CTX].reshape(CTX_CHUNKS, BATCH, g, t, c).transpose(1, 0, 3, 2, 4).reshape(N_CTX, g * c)
    yl = y[S5_ROWS_CTX:].reshape(DEC_BATCH, LAT_CHUNKS, g, t, c).transpose(0, 1, 3, 2, 4).reshape(N_LAT, g * c)
    return jnp.concatenate([yc, yl], axis=0)


def _ret_kernel(*refs, n_chunks):
    q_ref, k_ref, v_ref, g_ref, dec_ref, mask_ref, cd_ref, s0_ref, nw_ref = refs[:9]
    o_ref, fin_ref, kvf_ref, kvb_ref = refs[-4:]
    hd = HEAD_DIM
    nh = 2

    def local_state(i, _):
        rows = pl.ds(pl.multiple_of(i * RET_CHUNK, RET_CHUNK), RET_CHUNK)
        k = k_ref[rows, :]
        v = v_ref[rows, :].astype(BF16)
        kf = (k * dec_ref[1]).astype(BF16)
        kb = (k * dec_ref[3]).astype(BF16)
        for h in range(nh):
            ls = slice(h * hd, (h + 1) * hd)
            kvf_ref[i, h] = lax.dot_general(kf[:, ls], v[:, ls], TN, preferred_element_type=F32)
            kvb_ref[i, h] = lax.dot_general(kb[:, ls], v[:, ls], TN, preferred_element_type=F32)
        return 0

    lax.fori_loop(0, n_chunks, local_state, 0)

    def scan_f(i, s):
        loc = kvf_ref[i]
        kvf_ref[i] = s
        return cd_ref[0] * s + loc

    def scan_b(i, s):
        j = n_chunks - 1 - i
        loc = kvb_ref[j]
        kvb_ref[j] = s
        return cd_ref[1] * s + loc

    fin_ref[0] = lax.fori_loop(0, n_chunks, scan_f, s0_ref[0])
    fin_ref[1] = lax.fori_loop(0, n_chunks, scan_b, s0_ref[1])

    avg = _group_avg_matrix()

    def outputs(i, _):
        rows = pl.ds(pl.multiple_of(i * RET_CHUNK, RET_CHUNK), RET_CHUNK)
        q = q_ref[rows, :]
        qb = q.astype(BF16)
        kb = k_ref[rows, :].astype(BF16)
        v = v_ref[rows, :].astype(BF16)
        qf = (q * dec_ref[0]).astype(BF16)
        qr = (q * dec_ref[2]).astype(BF16)
        outs = []
        for h in range(nh):
            ls = slice(h * hd, (h + 1) * hd)
            inner = lax.dot_general(qb[:, ls], kb[:, ls], NT, preferred_element_type=F32) * mask_ref[h]
            o = jnp.dot(inner.astype(BF16), v[:, ls], preferred_element_type=F32)
            o += jnp.dot(qf[:, ls], kvf_ref[i, h].astype(BF16), preferred_element_type=F32)
            o += jnp.dot(qr[:, ls], kvb_ref[i, h].astype(BF16), preferred_element_type=F32)
            outs.append(o)
        o = jnp.concatenate(outs, axis=1)
        d = o - _split_dot(o, avg)
        o = d * lax.rsqrt(_split_dot(d * d, avg) + EPS) * nw_ref[...]
        g = g_ref[rows, :]
        o_ref[rows, :] = (g * jax.nn.sigmoid(g) * o).astype(o_ref.dtype)
        return 0

    lax.fori_loop(0, n_chunks, outputs, 0)


def _retention(rest, dec, mask, cdec, s0, nw, row_block0, n_req, length, prev=None):
    n_chunks = length // RET_CHUNK
    hp = RET_HEADS // 2

    def tok(cb):
        return pl.BlockSpec((length, 128), lambda r, p: (row_block0 + r, cb + p))

    state = pl.BlockSpec((None, 2, 2, HEAD_DIM, HEAD_DIM), lambda r, p: (r, 0, p, 0, 0))
    in_specs = [tok(C_RQ // 128), tok(C_RK // 128), tok(C_RV // 128), tok(C_RG // 128),
                pl.BlockSpec((4, RET_CHUNK, 128), lambda r, p: (0, 0, p)),
                pl.BlockSpec((2, RET_CHUNK, RET_CHUNK), lambda r, p: (p, 0, 0)),
                pl.BlockSpec((2, 2, HEAD_DIM, HEAD_DIM), lambda r, p: (0, p, 0, 0)),
                state,
                pl.BlockSpec((1, 128), lambda r, p: (0, p))]
    args = [rest, rest, rest, rest, dec, mask, cdec, s0, nw]
    aliases = {}
    if prev is not None:
        in_specs.append(pl.BlockSpec(memory_space=pl.ANY))
        args.append(prev)
        aliases = {9: 0}
    return pl.pallas_call(
        functools.partial(_ret_kernel, n_chunks=n_chunks),
        grid=(n_req, hp),
        in_specs=in_specs,
        out_specs=[tok(0), state],
        out_shape=[jax.ShapeDtypeStruct((N_TOK, RET_WIDTH), BF16),
                   jax.ShapeDtypeStruct((n_req, 2, RET_HEADS, HEAD_DIM, HEAD_DIM), F32)],
        scratch_shapes=[pltpu.VMEM((n_chunks, 2, HEAD_DIM, HEAD_DIM), F32),
                        pltpu.VMEM((n_chunks, 2, HEAD_DIM, HEAD_DIM), F32)],
        input_output_aliases=aliases,
        compiler_params=_cparams(("parallel", "parallel")),
        name="retention",
    )(*args)


def _retention_tables(decay_logit):
    lg = jax.nn.log_sigmoid(decay_logit.astype(F32))
    idx = jnp.arange(RET_CHUNK, dtype=F32)
    rel = idx[:, None] - idx[None, :]
    d_f = jnp.where(rel >= 0, jnp.exp(lg[0][:, None, None] * jnp.maximum(rel, 0.0)), 0.0)
    d_b = jnp.where(rel <= 0, jnp.exp(lg[1][:, None, None] * jnp.maximum(-rel, 0.0)), 0.0)
    mask = d_f + d_b

    def lanes(t):
        return jnp.repeat(t.T, HEAD_DIM, axis=1)

    dec = jnp.stack([lanes(jnp.exp(lg[0][:, None] * (idx + 1.0))),
                     lanes(jnp.exp(lg[0][:, None] * (RET_CHUNK - 1.0 - idx))),
                     lanes(jnp.exp(lg[1][:, None] * (RET_CHUNK - idx))),
                     lanes(jnp.exp(lg[1][:, None] * idx))], axis=0)
    cdec = jnp.broadcast_to(jnp.exp(lg * RET_CHUNK)[:, :, None, None], (2, RET_HEADS, HEAD_DIM, HEAD_DIM))
    return dec, mask, cdec


def _outproj_kernel(x_ref, rest_ref, ys_ref, at_ref, rt_ref, mod_ref, d_ref, wglu_ref, wout_ref, nw_ref, wr_ref,
                    x1_ref, h2_ref, aff_ref):
    y = ys_ref[...] + rest_ref[...] * d_ref[...]
    y = jax.nn.gelu(y)
    y = y * jax.nn.sigmoid(jnp.dot(y.astype(BF16), wglu_ref[...], preferred_element_type=F32))
    mix = (jnp.dot(y.astype(BF16), wout_ref[0:SSM_WIDTH, :], preferred_element_type=F32)
           + jnp.dot(at_ref[...], wout_ref[SSM_WIDTH:SSM_WIDTH + ATT_WIDTH, :], preferred_element_type=F32)
           + jnp.dot(rt_ref[...], wout_ref[SSM_WIDTH + ATT_WIDTH:, :], preferred_element_type=F32))
    gate1 = mod_ref[:, 2 * D_MODEL:3 * D_MODEL]
    shift2 = mod_ref[:, 3 * D_MODEL:4 * D_MODEL]
    scale2 = mod_ref[:, 4 * D_MODEL:5 * D_MODEL]
    x1 = x_ref[...] + gate1 * mix
    x1_ref[...] = x1
    h2 = (x1 * lax.rsqrt(jnp.mean(x1 * x1, axis=-1, keepdims=True) + EPS) * nw_ref[...]) * (1.0 + scale2) + shift2
    h2_ref[...] = h2.astype(h2_ref.dtype)
    hi = h2.astype(BF16)
    lo = (h2 - hi.astype(F32)).astype(BF16)
    logits = (jnp.dot(hi, wr_ref[0], preferred_element_type=F32)
              + jnp.dot(lo, wr_ref[0], preferred_element_type=F32)
              + jnp.dot(hi, wr_ref[1], preferred_element_type=F32))
    valid = lax.broadcasted_iota(jnp.int32, logits.shape, 1) < N_EXPERTS
    logits = jnp.where(valid, logits, -1e30)
    e = jnp.exp(logits - jnp.max(logits, axis=-1, keepdims=True))
    aff = e / jnp.sum(e, axis=-1, keepdims=True)
    aff_ref[...] = aff.T


def _out_projection(x, rest, ys, attn, ret, mod_l, d_row, wglu_bf, wout_bf, nw2, wr_split):
    return pl.pallas_call(
        _outproj_kernel,
        grid=(N_TILES,),
        in_specs=[pl.BlockSpec((TILE, D_MODEL), lambda i: (i, 0)),
                  pl.BlockSpec((TILE, SSM_WIDTH), lambda i: (i, C_U // SSM_WIDTH)),
                  pl.BlockSpec((TILE, SSM_WIDTH), lambda i: (i, 0)),
                  pl.BlockSpec((TILE, ATT_WIDTH), lambda i: (i, 0)),
                  pl.BlockSpec((TILE, RET_WIDTH), lambda i: (i, 0)),
                  pl.BlockSpec((None, 1, 6 * D_MODEL), lambda i: (_tile_mod_row(i), 0, 0)),
                  pl.BlockSpec((1, SSM_WIDTH), lambda i: (0, 0)),
                  pl.BlockSpec((SSM_WIDTH, SSM_WIDTH), lambda i: (0, 0)),
                  pl.BlockSpec((D_MODEL, D_MODEL), lambda i: (0, 0)),
                  pl.BlockSpec((1, D_MODEL), lambda i: (0, 0)),
                  pl.BlockSpec((2, D_MODEL, 128), lambda i: (0, 0, 0))],
        out_specs=[pl.BlockSpec((TILE, D_MODEL), lambda i: (i, 0)),
                   pl.BlockSpec((TILE, D_MODEL), lambda i: (i, 0)),
                   pl.BlockSpec((128, TILE), lambda i: (0, i))],
        out_shape=[jax.ShapeDtypeStruct((N_TOK, D_MODEL), F32),
                   jax.ShapeDtypeStruct((N_TOK, D_MODEL), BF16),
                   jax.ShapeDtypeStruct((128, N_TOK), F32)],
        compiler_params=_cparams(("parallel",)),
        name="out_projection",
    )(x, rest, ys, attn, ret, mod_l, d_row, wglu_bf, wout_bf, nw2, wr_split)


def _lane_cumsum(x01):
    rows, n = x01.shape
    r = lax.broadcasted_iota(jnp.int32, (256, 256), 0)
    c = lax.broadcasted_iota(jnp.int32, (256, 256), 1)
    tri = jnp.where(r <= c, 1.0, 0.0).astype(BF16)
    off = jnp.zeros((rows, 1), F32)
    parts = []
    for j in range(n // 256):
        cs = jnp.dot(x01[:, 256 * j:256 * (j + 1)].astype(BF16), tri, preferred_element_type=F32) + off
        parts.append(cs)
        off = cs[:, 255:256]
    return jnp.concatenate(parts, axis=1)


def _count(m):
    return jnp.sum(jnp.where(m, 1.0, 0.0), axis=1, keepdims=True)


def _route_kernel(aff_ref, slot_ref, gate_ref, *, cap, seg):
    n_seg = aff_ref.shape[1] // seg
    segs = [slice(i * seg, (i + 1) * seg) for i in range(n_seg)]
    tiny = float(jnp.finfo(jnp.float32).tiny)

    def step(_, bounds):
        out = []
        for i in range(n_seg):
            lo, hi = bounds[2 * i], bounds[2 * i + 1]
            mid = jnp.where(lo > 0.0, jnp.sqrt(lo) * jnp.sqrt(hi), jnp.maximum(hi * (2.0 ** -16), tiny))
            mid = jnp.minimum(jnp.maximum(mid, lo), hi)
            ok = _count(aff_ref[:, segs[i]] >= mid) >= cap
            out += [jnp.where(ok, mid, lo), jnp.where(ok, hi, mid)]
        return tuple(out)

    init = (jnp.zeros((N_EXPERTS, 1), F32), jnp.full((N_EXPERTS, 1), 2.0, F32)) * n_seg
    bounds = lax.fori_loop(0, ROUTE_ITERS, step, init)

    for i in range(n_seg):
        a = aff_ref[:, segs[i]]
        lo, hi = bounds[2 * i], bounds[2 * i + 1]
        above = a >= hi
        band = (a >= lo) & (a < hi)
        sel = above | (band & (_lane_cumsum(jnp.where(band, 1.0, 0.0)) <= cap - _count(above)))
        slot = _lane_cumsum(jnp.where(sel, 1.0, 0.0)) - 1.0
        slot_ref[:, segs[i]] = jnp.where(sel, slot, -1.0).astype(jnp.int32)
        gate_ref[:, segs[i]] = jnp.where(sel, a, 0.0)


def _route(aff_t, col_block0, n_blocks, width, seg, cap):
    return pl.pallas_call(
        functools.partial(_route_kernel, cap=cap, seg=seg),
        grid=(n_blocks,),
        in_specs=[pl.BlockSpec((N_EXPERTS, width), lambda i: (0, col_block0 + i))],
        out_specs=[pl.BlockSpec((N_EXPERTS, width), lambda i: (0, i))] * 2,
        out_shape=[jax.ShapeDtypeStruct((N_EXPERTS, n_blocks * width), jnp.int32),
                   jax.ShapeDtypeStruct((N_EXPERTS, n_blocks * width), F32)],
        compiler_params=_cparams(("parallel",)),
        name="route",
    )(aff_t)


def _gather_lat_kernel(slot_ref, gate_ref, h_ref, o_ref, g_ref):
    e = pl.program_id(1)
    slot = slot_ref[pl.ds(e, 1), :]
    hit = lax.broadcasted_iota(jnp.int32, (CAP_LAT, slot.shape[1]), 0) == slot
    onehot = jnp.where(hit, 1.0, 0.0).astype(BF16)
    o_ref[...] = jnp.dot(onehot, h_ref[...], preferred_element_type=F32).astype(o_ref.dtype)
    g_ref[...] = jnp.sum(jnp.where(hit, gate_ref[pl.ds(e, 1), :], 0.0), axis=1, keepdims=True)


def _gather_lat(slot, gate, h2):
    rows = pl.BlockSpec((N_EXPERTS, DEC_SEQ), lambda r, e: (0, r))
    return pl.pallas_call(
        _gather_lat_kernel,
        grid=(DEC_BATCH, N_EXPERTS),
        in_specs=[rows, rows,
                  pl.BlockSpec((DEC_SEQ, D_MODEL), lambda r, e: (N_CTX // DEC_SEQ + r, 0))],
        out_specs=[pl.BlockSpec((None, CAP_LAT, D_MODEL), lambda r, e: (e, r, 0)),
                   pl.BlockSpec((None, CAP_LAT, 1), lambda r, e: (e, r, 0))],
        out_shape=[jax.ShapeDtypeStruct((N_EXPERTS, DEC_BATCH * CAP_LAT, D_MODEL), BF16),
                   jax.ShapeDtypeStruct((N_EXPERTS, DEC_BATCH * CAP_LAT, 1), F32)],
        compiler_params=_cparams(("parallel", "arbitrary")),
        name="gather_lat",
    )(slot, gate, h2)


def _ctx_onehot(slot):
    rows = lax.broadcasted_iota(jnp.int32, (CAP_CTX, SEQ), 0)
    hits = [rows == slot[e:e + 1, :] for e in range(N_EXPERTS)]
    onehot = jnp.concatenate([jnp.where(h, 1.0, 0.0) for h in hits], axis=0).astype(BF16)
    return onehot, hits


def _gather_ctx_kernel(slot_ref, gate_ref, h_ref, o_ref, g_ref):
    onehot, hits = _ctx_onehot(slot_ref[...])
    xs = jnp.dot(onehot, h_ref[...], preferred_element_type=F32).astype(o_ref.dtype)
    gate = gate_ref[...]
    for e in range(N_EXPERTS):
        o_ref[e] = xs[e * CAP_CTX:(e + 1) * CAP_CTX]
        g_ref[e] = jnp.sum(jnp.where(hits[e], gate[e:e + 1, :], 0.0), axis=1, keepdims=True)


def _gather_ctx(slot, gate, h2):
    rows = pl.BlockSpec((N_EXPERTS, SEQ), lambda r: (0, r))
    return pl.pallas_call(
        _gather_ctx_kernel,
        grid=(BATCH,),
        in_specs=[rows, rows, pl.BlockSpec((SEQ, D_MODEL), lambda r: (r, 0))],
        out_specs=[pl.BlockSpec((N_EXPERTS, CAP_CTX, D_MODEL), lambda r: (0, r, 0)),
                   pl.BlockSpec((N_EXPERTS, CAP_CTX, 1), lambda r: (0, r, 0))],
        out_shape=[jax.ShapeDtypeStruct((N_EXPERTS, BATCH * CAP_CTX, D_MODEL), BF16),
                   jax.ShapeDtypeStruct((N_EXPERTS, BATCH * CAP_CTX, 1), F32)],
        compiler_params=_cparams(("parallel",)),
        name="gather_ctx",
    )(slot, gate, h2)


def _ffn_kernel(xc_ref, xl_ref, gc_ref, gl_ref, wg_ref, wu_ref, wd_ref, yc_ref, yl_ref, accc_ref, accl_ref):
    f = pl.program_id(1)
    wg = wg_ref[...].astype(BF16)
    wu = wu_ref[...].astype(BF16)
    wd = wd_ref[...].astype(BF16)

    def part(x_ref, gate_ref, acc_ref, y_ref):
        x = x_ref[...]
        a = jnp.dot(x, wg, preferred_element_type=F32)
        up = jnp.dot(x, wu, preferred_element_type=F32)
        mid = (a * jax.nn.sigmoid(a) * up).astype(BF16)
        y = jnp.dot(mid, wd, preferred_element_type=F32)

        @pl.when(f == 0)
        def _():
            acc_ref[...] = y

        @pl.when(f > 0)
        def _():
            acc_ref[...] += y

        @pl.when(f == pl.num_programs(1) - 1)
        def _():
            y_ref[...] = (acc_ref[...] * gate_ref[...]).astype(y_ref.dtype)

    part(xc_ref, gc_ref, accc_ref, yc_ref)
    part(xl_ref, gl_ref, accl_ref, yl_ref)


def _expert_ffn(xs_ctx, xs_lat, gs_ctx, gs_lat, w_gate, w_up, w_down, layer):
    tf = 512
    nc, nl = xs_ctx.shape[1], xs_lat.shape[1]
    return pl.pallas_call(
        _ffn_kernel,
        grid=(N_EXPERTS, EXPERT_FF // tf),
        in_specs=[pl.BlockSpec((None, nc, D_MODEL), lambda e, f: (e, 0, 0)),
                  pl.BlockSpec((None, nl, D_MODEL), lambda e, f: (e, 0, 0)),
                  pl.BlockSpec((None, nc, 1), lambda e, f: (e, 0, 0)),
                  pl.BlockSpec((None, nl, 1), lambda e, f: (e, 0, 0)),
                  pl.BlockSpec((None, None, D_MODEL, tf), lambda e, f: (layer, e, 0, f)),
                  pl.BlockSpec((None, None, D_MODEL, tf), lambda e, f: (layer, e, 0, f)),
                  pl.BlockSpec((None, None, tf, D_MODEL), lambda e, f: (layer, e, f, 0))],
        out_specs=[pl.BlockSpec((None, nc, D_MODEL), lambda e, f: (e, 0, 0)),
                   pl.BlockSpec((None, nl, D_MODEL), lambda e, f: (e, 0, 0))],
        out_shape=[jax.ShapeDtypeStruct(xs_ctx.shape, BF16), jax.ShapeDtypeStruct(xs_lat.shape, BF16)],
        scratch_shapes=[pltpu.VMEM((nc, D_MODEL), F32), pltpu.VMEM((nl, D_MODEL), F32)],
        compiler_params=_cparams(("parallel", "arbitrary")),
        name="expert_ffn",
    )(xs_ctx, xs_lat, gs_ctx, gs_lat, w_gate, w_up, w_down)


def _scatter_lat_kernel(slot_ref, y_ref, x_ref, mod_ref, o_ref, acc_ref):
    e = pl.program_id(2)
    slot = slot_ref[pl.ds(e, 1), :]
    hit = lax.broadcasted_iota(jnp.int32, (CAP_LAT, slot.shape[1]), 0) == slot
    onehot = jnp.where(hit, 1.0, 0.0).astype(BF16)
    contrib = lax.dot_general(onehot, y_ref[...], TN, preferred_element_type=F32)

    @pl.when(e == 0)
    def _():
        acc_ref[...] = contrib

    @pl.when(e > 0)
    def _():
        acc_ref[...] += contrib

    @pl.when(e == pl.num_programs(2) - 1)
    def _():
        o_ref[...] = x_ref[...] + mod_ref[:, 5 * D_MODEL:6 * D_MODEL] * acc_ref[...]


def _scatter_lat(slot, ys, x, mod_l):
    tt = 1024
    nt = DEC_SEQ // tt
    blk0 = N_CTX // tt
    tok = pl.BlockSpec((tt, D_MODEL), lambda r, t, e: (blk0 + r * nt + t, 0))
    return pl.pallas_call(
        _scatter_lat_kernel,
        grid=(DEC_BATCH, nt, N_EXPERTS),
        in_specs=[pl.BlockSpec((N_EXPERTS, tt), lambda r, t, e: (0, r * nt + t)),
                  pl.BlockSpec((None, CAP_LAT, D_MODEL), lambda r, t, e: (e, r, 0)),
                  tok,
                  pl.BlockSpec((None, 1, 6 * D_MODEL), lambda r, t, e: (1 + r, 0, 0))],
        out_specs=tok,
        out_shape=jax.ShapeDtypeStruct(x.shape, F32),
        scratch_shapes=[pltpu.VMEM((tt, D_MODEL), F32)],
        input_output_aliases={2: 0},
        compiler_params=_cparams(("parallel", "parallel", "arbitrary")),
        name="scatter_lat",
    )(slot, ys, x, mod_l)


def _scatter_ctx_kernel(slot_ref, y_ref, x_ref, mod_ref, o_ref):
    onehot, _ = _ctx_onehot(slot_ref[...])
    y = jnp.concatenate([y_ref[e] for e in range(N_EXPERTS)], axis=0)
    ffn = lax.dot_general(onehot, y, TN, preferred_element_type=F32)
    o_ref[...] = x_ref[...] + mod_ref[:, 5 * D_MODEL:6 * D_MODEL] * ffn


def _scatter_ctx(slot, ys, x, mod_l):
    tok = pl.BlockSpec((SEQ, D_MODEL), lambda r: (r, 0))
    return pl.pallas_call(
        _scatter_ctx_kernel,
        grid=(BATCH,),
        in_specs=[pl.BlockSpec((N_EXPERTS, SEQ), lambda r: (0, r)),
                  pl.BlockSpec((N_EXPERTS, CAP_CTX, D_MODEL), lambda r: (0, r, 0)),
                  tok,
                  pl.BlockSpec((None, 1, 6 * D_MODEL), lambda r: (0, 0, 0))],
        out_specs=tok,
        out_shape=jax.ShapeDtypeStruct(x.shape, F32),
        input_output_aliases={2: 0},
        compiler_params=_cparams(("parallel",)),
        name="scatter_ctx",
    )(slot, ys, x, mod_l)


def _final_norm_kernel(x_ref, w_ref, o_ref):
    x = x_ref[...]
    o_ref[...] = x * lax.rsqrt(jnp.mean(x * x, axis=-1, keepdims=True) + EPS) * w_ref[...]


def _final_norm(x, w):
    return pl.pallas_call(
        _final_norm_kernel,
        grid=(N_TILES,),
        in_specs=[pl.BlockSpec((TILE, D_MODEL), lambda i: (i, 0)), pl.BlockSpec((1, D_MODEL), lambda i: (0, 0))],
        out_specs=pl.BlockSpec((TILE, D_MODEL), lambda i: (i, 0)),
        out_shape=jax.ShapeDtypeStruct(x.shape, F32),
        compiler_params=_cparams(("parallel",)),
        name="final_norm",
    )(x, w)


def _rope_tables():
    rows = DEC_SEQ // GRID_W
    row = jnp.repeat(jnp.arange(rows, dtype=F32), GRID_W)
    col = jnp.tile(jnp.arange(GRID_W, dtype=F32), rows)
    n_freq = HEAD_DIM // 4
    inv_freq = ROPE_THETA ** (-jnp.arange(n_freq, dtype=F32) / n_freq)
    ang = jnp.concatenate([row[:, None] * inv_freq, col[:, None] * inv_freq], axis=-1)
    cos, sin = jnp.cos(ang), jnp.sin(ang)
    cos_t = jnp.tile(jnp.concatenate([cos, cos], -1), (1, 128 // HEAD_DIM))
    sin_t = jnp.tile(jnp.concatenate([-sin, sin], -1), (1, 128 // HEAD_DIM))
    cos_t = jnp.concatenate([jnp.ones((TILE, 128), F32), cos_t], axis=0)
    sin_t = jnp.concatenate([jnp.zeros((TILE, 128), F32), sin_t], axis=0)
    return cos_t, sin_t


def _s5_initial_rows(state_ssm):
    st = state_ssm.astype(F32)
    re, im = st[..., 0], st[..., 1]
    both = jnp.stack([jnp.concatenate([re, im], -1), jnp.concatenate([im, re], -1)], axis=3)
    return both.transpose(1, 0, 2, 3, 4, 5).reshape(DEPTH, 4 * DEC_BATCH, N_SSM_GROUPS * 128)


def kernel(x_prompt, x_sample, cache_k, cache_v, state_ssm, state_ret, c, c_ctx, w_mod, b_mod, norm1_w, norm2_w, w_in, w_out, qn_w, kn_w, ssm_lambda_re, ssm_lambda_im, ssm_b_re, ssm_b_im, ssm_c_re, ssm_c_im, ssm_log_dt, ssm_d, ssm_w_glu, ret_decay_logit, ret_norm_w, w_router, w_gate, w_up, w_down, final_norm_w):
    x = jnp.concatenate([x_prompt.reshape(N_CTX, D_MODEL), x_sample.reshape(N_LAT, D_MODEL)], axis=0)
    cond_t = jnp.zeros((D_MODEL, 8), F32).at[:, 0].set(c_ctx).at[:, 1:1 + DEC_BATCH].set(c.T)
    mod = _modulation(cond_t, w_mod, b_mod).reshape(DEPTH, 8, 1, 6 * D_MODEL)
    cos_t, sin_t = _rope_tables()
    zero_ret = jnp.zeros((BATCH, 2, RET_HEADS, HEAD_DIM, HEAD_DIM), F32)
    ctx_blocks = N_CTX // DEC_SEQ

    w_in_bf, w_out_bf, w_glu_bf = w_in.astype(BF16), w_out.astype(BF16), ssm_w_glu.astype(BF16)
    s5_w1, s5_w2, s5_a = jax.vmap(_s5_matrices)(ssm_lambda_re, ssm_lambda_im, ssm_b_re, ssm_b_im,
                                                ssm_c_re, ssm_c_im, ssm_log_dt)
    s5_h0 = _s5_initial_rows(state_ssm)
    ret_dec, ret_mask, ret_cdec = jax.vmap(_retention_tables)(ret_decay_logit)
    wr = jnp.pad(w_router.astype(F32), ((0, 0), (0, 0), (0, 128 - N_EXPERTS)))
    wr_hi = wr.astype(BF16)
    wr_split = jnp.stack([wr_hi, (wr - wr_hi.astype(F32)).astype(BF16)], axis=1)
    qn_t, kn_t = jnp.tile(qn_w, (1, 2)), jnp.tile(kn_w, (1, 2))
    cache_kv = jnp.concatenate([cache_k.reshape(DEC_BATCH, DEPTH, PAST_LEN, KV_WIDTH),
                                cache_v.reshape(DEC_BATCH, DEPTH, PAST_LEN, KV_WIDTH)], axis=-1).astype(BF16)

    ks, vs, ss, rs = [], [], [], []
    for l in range(DEPTH):
        mod_l = mod[l]
        q, kv, rest = _in_projection(x, mod_l, norm1_w[l].reshape(1, -1), w_in_bf[l],
                                     qn_t[l].reshape(1, -1), kn_t[l].reshape(1, -1), cos_t, sin_t)
        ks.append(rest[:N_CTX, C_K:C_K + KV_WIDTH].reshape(BATCH, SEQ, N_KV_HEADS, HEAD_DIM))
        vs.append(rest[:N_CTX, C_V:C_V + KV_WIDTH].reshape(BATCH, SEQ, N_KV_HEADS, HEAD_DIM))

        kv_ctx = kv[:N_CTX].reshape(BATCH, SEQ, 2 * KV_WIDTH)
        kv_lat = jnp.concatenate([kv[N_CTX:].reshape(DEC_BATCH, DEC_SEQ, 2 * KV_WIDTH), cache_kv[:, l]], axis=1)
        attn = _attention(q, kv_ctx, 0, BATCH, SEQ, SEQ)
        attn = _attention(q, kv_lat, N_CTX // 256, DEC_BATCH, DEC_SEQ, 256, prev=attn)

        y_loc, sf, sfs, sb, sbs = _s5_local(_s5_to_groups(rest[:, C_U:C_U + SSM_WIDTH]), s5_w1[l])
        hf, hb, fin_f, fin_b = _s5_scan(sf, sfs, sb, sbs, s5_a[l], s5_h0[l])
        ys = _s5_from_groups(_s5_out(y_loc, hf, hb, s5_w2[l]))
        fin = jnp.stack([fin_f, fin_b], axis=1).reshape(BATCH, 2, N_SSM_GROUPS, 2, SSM_STATE)
        ss.append(fin.transpose(0, 1, 2, 4, 3))

        nw_ret = ret_norm_w[l].reshape(1, -1)
        ret, fin_ret = _retention(rest, ret_dec[l], ret_mask[l], ret_cdec[l], zero_ret, nw_ret, 0, BATCH, SEQ)
        ret, _ = _retention(rest, ret_dec[l], ret_mask[l], ret_cdec[l], state_ret[:, l].astype(F32), nw_ret,
                            ctx_blocks, DEC_BATCH, DEC_SEQ, prev=ret)
        rs.append(fin_ret)

        x1, h2, aff_t = _out_projection(x, rest, ys, attn, ret, mod_l, ssm_d[l].reshape(1, -1),
                                        w_glu_bf[l], w_out_bf[l], norm2_w[l].reshape(1, -1), wr_split[l])

        slot_ctx, gate_ctx = _route(aff_t, 0, 1, N_CTX, SEQ, CAP_CTX)
        slot_lat, gate_lat = _route(aff_t, N_CTX // DEC_SEQ, DEC_BATCH, DEC_SEQ, DEC_SEQ, CAP_LAT)
        xs_ctx, gs_ctx = _gather_ctx(slot_ctx, gate_ctx, h2)
        xs_lat, gs_lat = _gather_lat(slot_lat, gate_lat, h2)
        y_ctx, y_lat = _expert_ffn(xs_ctx, xs_lat, gs_ctx, gs_lat, w_gate, w_up, w_down, l)
        x = _scatter_ctx(slot_ctx, y_ctx, x1, mod_l)
        x = _scatter_lat(slot_lat, y_lat, x, mod_l)

    y = _final_norm(x, final_norm_w.reshape(1, -1))
    y_prompt = y[:N_CTX].reshape(BATCH, SEQ, D_MODEL)
    y_sample = y[N_CTX:].reshape(DEC_BATCH, DEC_SEQ, D_MODEL)
    return (y_prompt, y_sample, jnp.stack(ks, axis=1), jnp.stack(vs, axis=1),
            jnp.stack(ss, axis=1), jnp.stack(rs, axis=1))
```

```python
import functools

import jax
import jax.numpy as jnp
from jax import lax
from jax.experimental import pallas as pl
from jax.experimental.pallas import tpu as pltpu

F32 = jnp.float32
BF16 = jnp.bfloat16

D_MODEL = 1024
BATCH = 16
SEQ = 256
DEPTH = 4
DEC_BATCH = 2
DEC_SEQ = 4096
PAST_LEN = 256
GRID_W = 64
HEAD_DIM = 64
SSM_WIDTH = 256
SSM_GROUP = 16
N_SSM_GROUPS = 16
SSM_STATE = 64
ATT_WIDTH = 512
N_HEADS = 8
N_KV_HEADS = 2
KV_WIDTH = 128
RET_WIDTH = 256
RET_HEADS = 4
IN_WIDTH = 2048
RET_CHUNK = 128
N_EXPERTS = 16
EXPERT_FF = 1024
ROPE_THETA = 10000.0
EPS = 1e-6

N_CTX = BATCH * SEQ
N_LAT = DEC_BATCH * DEC_SEQ
N_TOK = N_CTX + N_LAT
TILE = 256
N_TILES = N_TOK // TILE
CTX_TILES = N_CTX // TILE
LAT_TILES_PER_REQ = DEC_SEQ // TILE
CAP_CTX = 2 * SEQ // N_EXPERTS
CAP_LAT = 2 * DEC_SEQ // N_EXPERTS
S5_CHUNK = 16
CTX_CHUNKS = SEQ // S5_CHUNK
LAT_CHUNKS = DEC_SEQ // S5_CHUNK
S5_ROWS_CTX = BATCH * CTX_CHUNKS
S5_ROWS = S5_ROWS_CTX + DEC_BATCH * LAT_CHUNKS
C_U, C_K, C_V, C_RQ, C_RK, C_RV, C_RG = 0, 256, 384, 512, 768, 1024, 1280
REST_WIDTH = 1536
ROUTE_ITERS = 48
VMEM_LIMIT = 56 * 1024 * 1024

TN = (((0,), (0,)), ((), ()))
NT = (((1,), (1,)), ((), ()))


def _cparams(sem):
    return pltpu.CompilerParams(dimension_semantics=sem, vmem_limit_bytes=VMEM_LIMIT)


def _tile_mod_row(i):
    return jnp.where(i < CTX_TILES, 0, 1 + (i - CTX_TILES) // LAT_TILES_PER_REQ)


def _split_dot(v, m):
    hi = v.astype(BF16)
    lo = (v - hi.astype(F32)).astype(BF16)
    return (jnp.dot(hi, m, preferred_element_type=F32)
            + jnp.dot(lo, m, preferred_element_type=F32))


def _group_avg_matrix():
    r = lax.broadcasted_iota(jnp.int32, (128, 128), 0) // HEAD_DIM
    c = lax.broadcasted_iota(jnp.int32, (128, 128), 1) // HEAD_DIM
    return jnp.where(r == c, 1.0 / HEAD_DIM, 0.0).astype(BF16)


def _mod_kernel(ct_ref, w_ref, b_ref, o_ref):
    c = ct_ref[...]
    s = c * jax.nn.sigmoid(c)
    w = w_ref[...]
    rows = [jnp.sum(w * s[:, r:r + 1], axis=0, keepdims=True) for r in range(3)]
    rows.append(jnp.zeros((5, w.shape[1]), F32))
    o_ref[...] = jnp.concatenate(rows, axis=0) + b_ref[...]


def _modulation(cond_t, w_mod, b_mod):
    tn = 512
    n = 6 * D_MODEL
    return pl.pallas_call(
        _mod_kernel,
        grid=(DEPTH, n // tn),
        in_specs=[pl.BlockSpec((D_MODEL, 8), lambda l, j: (0, 0)),
                  pl.BlockSpec((None, D_MODEL, tn), lambda l, j: (l, 0, j)),
                  pl.BlockSpec((None, 1, tn), lambda l, j: (l, 0, j))],
        out_specs=pl.BlockSpec((None, 8, tn), lambda l, j: (l, 0, j)),
        out_shape=jax.ShapeDtypeStruct((DEPTH, 8, n), F32),
        compiler_params=_cparams(("arbitrary", "arbitrary")),
        name="modulation",
    )(cond_t, w_mod, b_mod.reshape(DEPTH, 1, n))


def _inproj_kernel(x_ref, mod_ref, nw_ref, w_ref, qn_ref, kn_ref, cos_ref, sin_ref, q_ref, kv_ref, ub_ref, rest_ref):
    x = x_ref[...]
    shift = mod_ref[:, 0:D_MODEL]
    scale = mod_ref[:, D_MODEL:2 * D_MODEL]
    y = x * lax.rsqrt(jnp.mean(x * x, axis=-1, keepdims=True) + EPS) * nw_ref[...]
    h = y * (1.0 + scale) + shift
    proj = jnp.dot(h.astype(BF16), w_ref[...], preferred_element_type=F32)

    avg = _group_avg_matrix()
    cos = cos_ref[...]
    sin = sin_ref[...]
    first_half = (lax.broadcasted_iota(jnp.int32, (TILE, 128), 1) % HEAD_DIM) < (HEAD_DIM // 2)

    def head_norm(z, wrow):
        return z * lax.rsqrt(_split_dot(z * z, avg) + EPS) * wrow

    def rope(z):
        partner = jnp.where(first_half, pltpu.roll(z, 128 - HEAD_DIM // 2, 1), pltpu.roll(z, HEAD_DIM // 2, 1))
        return z * cos + partner * sin

    def col(off, j):
        return proj[:, off + 128 * j: off + 128 * (j + 1)]

    qn = qn_ref[...]
    for j in range(ATT_WIDTH // 128):
        z = rope(head_norm(col(SSM_WIDTH, j), qn)) * (HEAD_DIM ** -0.5)
        q_ref[:, 128 * j:128 * (j + 1)] = z.astype(q_ref.dtype)
    p_k = SSM_WIDTH + ATT_WIDTH
    rest_ref[:, C_U:C_U + SSM_WIDTH] = proj[:, 0:SSM_WIDTH]
    ub_ref[...] = proj[:, 0:SSM_WIDTH].astype(ub_ref.dtype)
    k = rope(head_norm(col(p_k, 0), kn_ref[...]))
    v = col(p_k + KV_WIDTH, 0)
    rest_ref[:, C_K:C_K + KV_WIDTH] = k
    rest_ref[:, C_V:C_V + KV_WIDTH] = v
    kv_ref[:, 0:KV_WIDTH] = k.astype(kv_ref.dtype)
    kv_ref[:, KV_WIDTH:2 * KV_WIDTH] = v.astype(kv_ref.dtype)
    p_r = p_k + 2 * KV_WIDTH
    for j in range(RET_WIDTH // 128):
        rest_ref[:, C_RQ + 128 * j:C_RQ + 128 * (j + 1)] = rope(col(p_r, j))
        rest_ref[:, C_RK + 128 * j:C_RK + 128 * (j + 1)] = rope(col(p_r + RET_WIDTH, j)) * (HEAD_DIM ** -0.5)
    rest_ref[:, C_RV:C_RV + 2 * RET_WIDTH] = proj[:, p_r + 2 * RET_WIDTH:p_r + 4 * RET_WIDTH]


def _in_projection(x, mod_l, nw, w_in_bf, qn, kn, cos_t, sin_t):
    def rope_blk(i):
        return (jnp.where(i < CTX_TILES, 0, 1 + (i - CTX_TILES) % LAT_TILES_PER_REQ), 0)
    return pl.pallas_call(
        _inproj_kernel,
        grid=(N_TILES,),
        in_specs=[pl.BlockSpec((TILE, D_MODEL), lambda i: (i, 0)),
                  pl.BlockSpec((None, 1, 6 * D_MODEL), lambda i: (_tile_mod_row(i), 0, 0)),
                  pl.BlockSpec((1, D_MODEL), lambda i: (0, 0)),
                  pl.BlockSpec((D_MODEL, IN_WIDTH), lambda i: (0, 0)),
                  pl.BlockSpec((1, 128), lambda i: (0, 0)),
                  pl.BlockSpec((1, 128), lambda i: (0, 0)),
                  pl.BlockSpec((TILE, 128), rope_blk),
                  pl.BlockSpec((TILE, 128), rope_blk)],
        out_specs=[pl.BlockSpec((TILE, ATT_WIDTH), lambda i: (i, 0)),
                   pl.BlockSpec((TILE, 2 * KV_WIDTH), lambda i: (i, 0)),
                   pl.BlockSpec((TILE, SSM_WIDTH), lambda i: (i, 0)),
                   pl.BlockSpec((TILE, REST_WIDTH), lambda i: (i, 0))],
        out_shape=[jax.ShapeDtypeStruct((N_TOK, ATT_WIDTH), BF16),
                   jax.ShapeDtypeStruct((N_TOK, 2 * KV_WIDTH), BF16),
                   jax.ShapeDtypeStruct((N_TOK, SSM_WIDTH), BF16),
                   jax.ShapeDtypeStruct((N_TOK, REST_WIDTH), F32)],
        compiler_params=_cparams(("parallel",)),
        name="in_projection",
    )(x, mod_l, nw, w_in_bf, qn, kn, cos_t, sin_t)


def _attn_kernel(*refs):
    q_ref, k_ref, v_ref, o_ref = refs[0], refs[1], refs[2], refs[-1]
    group = N_HEADS // N_KV_HEADS
    for kv in range(N_KV_HEADS):
        k = k_ref[:, kv * HEAD_DIM:(kv + 1) * HEAD_DIM]
        v = v_ref[:, kv * HEAD_DIM:(kv + 1) * HEAD_DIM]
        for g in range(group):
            cols = slice((kv * group + g) * HEAD_DIM, (kv * group + g + 1) * HEAD_DIM)
            s = lax.dot_general(q_ref[:, cols], k, NT, preferred_element_type=F32)
            m = jnp.max(s, axis=-1, keepdims=True)
            p = jnp.exp(s - m)
            l = jnp.sum(p, axis=-1, keepdims=True)
            o = jnp.dot(p.astype(BF16), v, preferred_element_type=F32) / l
            o_ref[:, cols] = o.astype(o_ref.dtype)


def _attention(q, kv, row_block0, n_req, lq, tq, prev=None):
    lk = kv.shape[1]
    nq = lq // tq
    tok = pl.BlockSpec((tq, ATT_WIDTH), lambda r, i: (row_block0 + r * nq + i, 0))
    in_specs = [tok,
                pl.BlockSpec((None, lk, KV_WIDTH), lambda r, i: (r, 0, 0)),
                pl.BlockSpec((None, lk, KV_WIDTH), lambda r, i: (r, 0, 1))]
    args = [q, kv, kv]
    aliases = {}
    if prev is not None:
        in_specs.append(pl.BlockSpec(memory_space=pl.ANY))
        args.append(prev)
        aliases = {3: 0}
    return pl.pallas_call(
        _attn_kernel,
        grid=(n_req, nq),
        in_specs=in_specs,
        out_specs=tok,
        out_shape=jax.ShapeDtypeStruct((N_TOK, ATT_WIDTH), BF16),
        input_output_aliases=aliases,
        compiler_params=_cparams(("parallel", "parallel")),
        name="attention",
    )(*args)


def _s5_local_kernel(u_ref, t_ref, b_ref, y_ref, s_ref):
    u = u_ref[...]
    y_ref[...] = jnp.dot(u, t_ref[...], preferred_element_type=F32)
    s_ref[...] = jnp.dot(u, b_ref[...], preferred_element_type=F32)


def _s5_local(u2, toeplitz, inject):
    tn = 256
    n = S5_CHUNK * SSM_WIDTH
    wspec = pl.BlockSpec((n, tn), lambda i: (0, i))
    ospec = pl.BlockSpec((S5_ROWS, tn), lambda i: (0, i))
    return pl.pallas_call(
        _s5_local_kernel,
        grid=(n // tn,),
        in_specs=[pl.BlockSpec((S5_ROWS, n), lambda i: (0, 0)), wspec, wspec],
        out_specs=[ospec, ospec],
        out_shape=[jax.ShapeDtypeStruct((S5_ROWS, n), F32)] * 2,
        compiler_params=_cparams(("parallel",)),
        name="s5_local",
    )(u2, toeplitz, inject)


def _s5_scan_kernel(sf_ref, sb_ref, a_ref, h0_ref, hf_ref, hb_ref, ff_ref, fb_ref, sfs_ref, sbs_ref):
    a1f, a2f, a1b, a2b = a_ref[0:1, :], a_ref[1:2, :], a_ref[2:3, :], a_ref[3:4, :]
    w = sf_ref.shape[1]
    for c in range(w // 128):
        cols = slice(128 * c, 128 * (c + 1))
        sfs_ref[:, cols] = pltpu.roll(sf_ref[:, cols], SSM_STATE, 1)
        sbs_ref[:, cols] = pltpu.roll(sb_ref[:, cols], SSM_STATE, 1)

    zero = jnp.zeros((1, w), F32)
    for r in range(BATCH):
        hf, hfs, hb, hbs = zero, zero, zero, zero
        for j in range(CTX_CHUNKS):
            row = r * CTX_CHUNKS + j
            hf_ref[row:row + 1, :] = hf
            hf, hfs = (a1f * hf + a2f * hfs + sf_ref[row:row + 1, :], a1f * hfs - a2f * hf + sfs_ref[row:row + 1, :])
            row = r * CTX_CHUNKS + CTX_CHUNKS - 1 - j
            hb_ref[row:row + 1, :] = hb
            hb, hbs = (a1b * hb + a2b * hbs + sb_ref[row:row + 1, :], a1b * hbs - a2b * hb + sbs_ref[row:row + 1, :])
        ff_ref[r:r + 1, :] = hf
        fb_ref[r:r + 1, :] = hb

    def body(j, carry):
        out = []
        for r in range(DEC_BATCH):
            hf, hfs, hb, hbs = carry[4 * r:4 * r + 4]
            row = pl.ds(S5_ROWS_CTX + r * LAT_CHUNKS + j, 1)
            hf_ref[row, :] = hf
            nf = a1f * hf + a2f * hfs + sf_ref[row, :]
            nfs = a1f * hfs - a2f * hf + sfs_ref[row, :]
            row = pl.ds(S5_ROWS_CTX + r * LAT_CHUNKS + LAT_CHUNKS - 1 - j, 1)
            hb_ref[row, :] = hb
            nb = a1b * hb + a2b * hbs + sb_ref[row, :]
            nbs = a1b * hbs - a2b * hb + sbs_ref[row, :]
            out += [nf, nfs, nb, nbs]
        return tuple(out)

    init = tuple(h0_ref[i:i + 1, :] for i in range(4 * DEC_BATCH))
    lax.fori_loop(0, LAT_CHUNKS, body, init)


def _s5_scan(states, a_rows, h0_rows):
    w = 512
    full = N_SSM_GROUPS * 128
    nb = full // w
    sspec = pl.BlockSpec((S5_ROWS, w), lambda i: (0, i))
    fspec = pl.BlockSpec((BATCH, w), lambda i: (0, i))
    return pl.pallas_call(
        _s5_scan_kernel,
        grid=(nb,),
        in_specs=[sspec,
                  pl.BlockSpec((S5_ROWS, w), lambda i: (0, nb + i)),
                  pl.BlockSpec((8, w), lambda i: (0, i)),
                  pl.BlockSpec((8, w), lambda i: (0, i))],
        out_specs=[sspec, sspec, fspec, fspec],
        out_shape=[jax.ShapeDtypeStruct((S5_ROWS, full), F32)] * 2 + [jax.ShapeDtypeStruct((BATCH, full), F32)] * 2,
        scratch_shapes=[pltpu.VMEM((S5_ROWS, w), F32), pltpu.VMEM((S5_ROWS, w), F32)],
        compiler_params=_cparams(("parallel",)),
        name="s5_scan",
    )(states, states, a_rows, h0_rows)


def _s5_out_kernel(y_ref, hf_ref, hb_ref, c_ref, o_ref, h_ref):
    full = N_SSM_GROUPS * 128

    @pl.when(pl.program_id(0) == 0)
    def _():
        h_ref[:, 0:full] = hf_ref[...].astype(BF16)
        h_ref[:, full:2 * full] = hb_ref[...].astype(BF16)

    o_ref[...] = y_ref[...] + jnp.dot(h_ref[...], c_ref[...], preferred_element_type=F32)


def _s5_out(y_loc, hf, hb, readout):
    tn = 256
    n = S5_CHUNK * SSM_WIDTH
    full = N_SSM_GROUPS * 128
    hspec = pl.BlockSpec((S5_ROWS, full), lambda i: (0, 0))
    return pl.pallas_call(
        _s5_out_kernel,
        grid=(n // tn,),
        in_specs=[pl.BlockSpec((S5_ROWS, tn), lambda i: (0, i)), hspec, hspec,
                  pl.BlockSpec((2 * full, tn), lambda i: (0, i))],
        out_specs=pl.BlockSpec((S5_ROWS, tn), lambda i: (0, i)),
        out_shape=jax.ShapeDtypeStruct((S5_ROWS, n), F32),
        scratch_shapes=[pltpu.VMEM((S5_ROWS, 2 * full), BF16)],
        compiler_params=_cparams(("arbitrary",)),
        name="s5_out",
    )(y_loc, hf, hb, readout)


def _s5_matrices(lam_re, lam_im, b_re, b_im, c_re, c_im, log_dt):
    hp = lax.Precision.HIGHEST
    n = S5_CHUNK
    tau = jnp.arange(n + 1, dtype=F32)
    toes, injs, ros, a_rows = [], [], [], []
    for di in range(2):
        lr, li = lam_re[di].astype(F32), lam_im[di].astype(F32)
        dt = jnp.exp(log_dt[di].astype(F32))[:, None]
        mag = jnp.exp(lr * dt * tau[:, None, None])
        ang = li * dt * tau[:, None, None]
        e_re, e_im = mag * jnp.cos(ang), mag * jnp.sin(ang)
        nr, ni = e_re[1] - 1.0, e_im[1]
        den = lr * lr + li * li
        f_re, f_im = (nr * lr + ni * li) / den, (ni * lr - nr * li) / den
        br, bi = b_re[di].astype(F32), b_im[di].astype(F32)
        bb_re = f_re[..., None] * br - f_im[..., None] * bi
        bb_im = f_re[..., None] * bi + f_im[..., None] * br
        cr, ci = c_re[di].astype(F32), c_im[di].astype(F32)
        ce_re = cr[None] * e_re[:n, :, None, :] - ci[None] * e_im[:n, :, None, :]
        ce_im = cr[None] * e_im[:n, :, None, :] + ci[None] * e_re[:n, :, None, :]
        kern = (jnp.einsum('tgcp,gpd->tgcd', ce_re, bb_re, precision=hp)
                - jnp.einsum('tgcp,gpd->tgcd', ce_im, bb_im, precision=hp))
        kp = jnp.concatenate([jnp.zeros_like(kern), kern], axis=0)
        if di == 0:
            toe = jnp.stack([kp[n - s:2 * n - s] for s in range(n)], axis=0)
        else:
            toe = jnp.stack([kp[s + 1:s + n + 1][::-1] for s in range(n)], axis=0)
        toes.append(toe.transpose(0, 2, 4, 1, 3))
        pe_re, pe_im = (e_re[:n][::-1], e_im[:n][::-1]) if di == 0 else (e_re[:n], e_im[:n])
        inj_re = pe_re[..., None] * bb_re[None] - pe_im[..., None] * bb_im[None]
        inj_im = pe_re[..., None] * bb_im[None] + pe_im[..., None] * bb_re[None]
        injs.append(jnp.concatenate([inj_re.transpose(0, 1, 3, 2), inj_im.transpose(0, 1, 3, 2)], -1))
        qe_re, qe_im = (e_re[1:], e_im[1:]) if di == 0 else (e_re[1:][::-1], e_im[1:][::-1])
        ro_re = cr[None] * qe_re[:, :, None, :] - ci[None] * qe_im[:, :, None, :]
        ro_im = cr[None] * qe_im[:, :, None, :] + ci[None] * qe_re[:, :, None, :]
        ros.append(jnp.concatenate([ro_re.transpose(1, 3, 0, 2), -ro_im.transpose(1, 3, 0, 2)], axis=1))
        a_re, a_im = e_re[n], e_im[n]
        a_rows.append(jnp.concatenate([a_re, a_re], -1).reshape(1, -1))
        a_rows.append(jnp.concatenate([-a_im, a_im], -1).reshape(1, -1))
    same = jnp.eye(N_SSM_GROUPS, dtype=F32)[None, :, None, None, :, None]
    size = n * SSM_WIDTH
    toeplitz = ((toes[0] + toes[1])[:, :, :, :, None, :] * same).reshape(size, size).astype(BF16)
    inj = jnp.stack(injs, axis=3)
    inject = (inj[:, :, :, :, None, :] * same).reshape(size, size).astype(BF16)
    ro = jnp.stack(ros, axis=0)
    readout = (ro[:, :, :, :, None, :] * same).reshape(size, size).astype(BF16)
    a_rows = jnp.concatenate(a_rows + [jnp.zeros((4, N_SSM_GROUPS * 128), F32)], axis=0)
    return toeplitz, inject, readout, a_rows


def _ret_kernel(*refs, n_chunks):
    q_ref, k_ref, v_ref, g_ref, dec_ref, mask_ref, cd_ref, s0_ref, nw_ref = refs[:9]
    o_ref, fin_ref, kvf_ref, kvb_ref = refs[-4:]
    hd = HEAD_DIM
    nh = 2

    def local_state(i, _):
        rows = pl.ds(pl.multiple_of(i * RET_CHUNK, RET_CHUNK), RET_CHUNK)
        k = k_ref[rows, :]
        v = v_ref[rows, :].astype(BF16)
        kf = (k * dec_ref[1]).astype(BF16)
        kb = (k * dec_ref[3]).astype(BF16)
        for h in range(nh):
            ls = slice(h * hd, (h + 1) * hd)
            kvf_ref[i, h] = lax.dot_general(kf[:, ls], v[:, ls], TN, preferred_element_type=F32)
            kvb_ref[i, h] = lax.dot_general(kb[:, ls], v[:, ls], TN, preferred_element_type=F32)
        return 0

    lax.fori_loop(0, n_chunks, local_state, 0, unroll=min(4, n_chunks))

    def scan_f(i, s):
        loc = kvf_ref[i]
        kvf_ref[i] = s
        return cd_ref[0] * s + loc

    def scan_b(i, s):
        j = n_chunks - 1 - i
        loc = kvb_ref[j]
        kvb_ref[j] = s
        return cd_ref[1] * s + loc

    fin_ref[0] = lax.fori_loop(0, n_chunks, scan_f, s0_ref[0], unroll=min(4, n_chunks))
    fin_ref[1] = lax.fori_loop(0, n_chunks, scan_b, s0_ref[1], unroll=min(4, n_chunks))

    avg = _group_avg_matrix()

    def outputs(i, _):
        rows = pl.ds(pl.multiple_of(i * RET_CHUNK, RET_CHUNK), RET_CHUNK)
        q = q_ref[rows, :]
        qb = q.astype(BF16)
        kb = k_ref[rows, :].astype(BF16)
        v = v_ref[rows, :].astype(BF16)
        qf = (q * dec_ref[0]).astype(BF16)
        qr = (q * dec_ref[2]).astype(BF16)
        outs = []
        for h in range(nh):
            ls = slice(h * hd, (h + 1) * hd)
            inner = lax.dot_general(qb[:, ls], kb[:, ls], NT, preferred_element_type=F32) * mask_ref[h]
            o = jnp.dot(inner.astype(BF16), v[:, ls], preferred_element_type=F32)
            o += jnp.dot(qf[:, ls], kvf_ref[i, h].astype(BF16), preferred_element_type=F32)
            o += jnp.dot(qr[:, ls], kvb_ref[i, h].astype(BF16), preferred_element_type=F32)
            outs.append(o)
        o = jnp.concatenate(outs, axis=1)
        d = o - _split_dot(o, avg)
        o = d * lax.rsqrt(_split_dot(d * d, avg) + EPS) * nw_ref[...]
        g = g_ref[rows, :]
        o_ref[rows, :] = (g * jax.nn.sigmoid(g) * o).astype(o_ref.dtype)
        return 0

    lax.fori_loop(0, n_chunks, outputs, 0, unroll=2)


def _retention(rest, dec, mask, cdec, s0, nw, row_block0, n_req, length, prev=None):
    n_chunks = length // RET_CHUNK
    hp = RET_HEADS // 2

    def tok(cb):
        return pl.BlockSpec((length, 128), lambda r, p: (row_block0 + r, cb + p))

    state = pl.BlockSpec((None, 2, 2, HEAD_DIM, HEAD_DIM), lambda r, p: (r, 0, p, 0, 0))
    in_specs = [tok(C_RQ // 128), tok(C_RK // 128), tok(C_RV // 128), tok(C_RG // 128),
                pl.BlockSpec((4, RET_CHUNK, 128), lambda r, p: (0, 0, p)),
                pl.BlockSpec((2, RET_CHUNK, RET_CHUNK), lambda r, p: (p, 0, 0)),
                pl.BlockSpec((2, 2, HEAD_DIM, HEAD_DIM), lambda r, p: (0, p, 0, 0)),
                state,
                pl.BlockSpec((1, 128), lambda r, p: (0, p))]
    args = [rest, rest, rest, rest, dec, mask, cdec, s0, nw]
    aliases = {}
    if prev is not None:
        in_specs.append(pl.BlockSpec(memory_space=pl.ANY))
        args.append(prev)
        aliases = {9: 0}
    return pl.pallas_call(
        functools.partial(_ret_kernel, n_chunks=n_chunks),
        grid=(n_req, hp),
        in_specs=in_specs,
        out_specs=[tok(0), state],
        out_shape=[jax.ShapeDtypeStruct((N_TOK, RET_WIDTH), BF16),
                   jax.ShapeDtypeStruct((n_req, 2, RET_HEADS, HEAD_DIM, HEAD_DIM), F32)],
        scratch_shapes=[pltpu.VMEM((n_chunks, 2, HEAD_DIM, HEAD_DIM), F32),
                        pltpu.VMEM((n_chunks, 2, HEAD_DIM, HEAD_DIM), F32)],
        input_output_aliases=aliases,
        compiler_params=_cparams(("parallel", "parallel")),
        name="retention",
    )(*args)


def _retention_tables(decay_logit):
    lg = jax.nn.log_sigmoid(decay_logit.astype(F32))
    idx = jnp.arange(RET_CHUNK, dtype=F32)
    rel = idx[:, None] - idx[None, :]
    d_f = jnp.where(rel >= 0, jnp.exp(lg[0][:, None, None] * jnp.maximum(rel, 0.0)), 0.0)
    d_b = jnp.where(rel <= 0, jnp.exp(lg[1][:, None, None] * jnp.maximum(-rel, 0.0)), 0.0)
    mask = d_f + d_b

    def lanes(t):
        return jnp.repeat(t.T, HEAD_DIM, axis=1)

    dec = jnp.stack([lanes(jnp.exp(lg[0][:, None] * (idx + 1.0))),
                     lanes(jnp.exp(lg[0][:, None] * (RET_CHUNK - 1.0 - idx))),
                     lanes(jnp.exp(lg[1][:, None] * (RET_CHUNK - idx))),
                     lanes(jnp.exp(lg[1][:, None] * idx))], axis=0)
    cdec = jnp.broadcast_to(jnp.exp(lg * RET_CHUNK)[:, :, None, None], (2, RET_HEADS, HEAD_DIM, HEAD_DIM))
    return dec, mask, cdec


def _outproj_kernel(x_ref, rest_ref, ys_ref, at_ref, rt_ref, mod_ref, d_ref, wglu_ref, wout_ref, nw_ref, wr_ref,
                    x1_ref, h2_ref, aff_ref):
    y = ys_ref[...] + rest_ref[...] * d_ref[...]
    y = jax.nn.gelu(y)
    y = y * jax.nn.sigmoid(jnp.dot(y.astype(BF16), wglu_ref[...], preferred_element_type=F32))
    mix = (jnp.dot(y.astype(BF16), wout_ref[0:SSM_WIDTH, :], preferred_element_type=F32)
           + jnp.dot(at_ref[...], wout_ref[SSM_WIDTH:SSM_WIDTH + ATT_WIDTH, :], preferred_element_type=F32)
           + jnp.dot(rt_ref[...], wout_ref[SSM_WIDTH + ATT_WIDTH:, :], preferred_element_type=F32))
    gate1 = mod_ref[:, 2 * D_MODEL:3 * D_MODEL]
    shift2 = mod_ref[:, 3 * D_MODEL:4 * D_MODEL]
    scale2 = mod_ref[:, 4 * D_MODEL:5 * D_MODEL]
    x1 = x_ref[...] + gate1 * mix
    x1_ref[...] = x1
    h2 = (x1 * lax.rsqrt(jnp.mean(x1 * x1, axis=-1, keepdims=True) + EPS) * nw_ref[...]) * (1.0 + scale2) + shift2
    h2_ref[...] = h2.astype(h2_ref.dtype)
    hi = h2.astype(BF16)
    lo = (h2 - hi.astype(F32)).astype(BF16)
    logits = (jnp.dot(hi, wr_ref[0], preferred_element_type=F32)
              + jnp.dot(lo, wr_ref[0], preferred_element_type=F32)
              + jnp.dot(hi, wr_ref[1], preferred_element_type=F32))
    valid = lax.broadcasted_iota(jnp.int32, logits.shape, 1) < N_EXPERTS
    logits = jnp.where(valid, logits, -1e30)
    e = jnp.exp(logits - jnp.max(logits, axis=-1, keepdims=True))
    aff = e / jnp.sum(e, axis=-1, keepdims=True)
    aff_ref[...] = aff.T


def _out_projection(x, rest, ys, attn, ret, mod_l, d_row, wglu_bf, wout_bf, nw2, wr_split):
    return pl.pallas_call(
        _outproj_kernel,
        grid=(N_TILES,),
        in_specs=[pl.BlockSpec((TILE, D_MODEL), lambda i: (i, 0)),
                  pl.BlockSpec((TILE, SSM_WIDTH), lambda i: (i, C_U // SSM_WIDTH)),
                  pl.BlockSpec((TILE, SSM_WIDTH), lambda i: (i, 0)),
                  pl.BlockSpec((TILE, ATT_WIDTH), lambda i: (i, 0)),
                  pl.BlockSpec((TILE, RET_WIDTH), lambda i: (i, 0)),
                  pl.BlockSpec((None, 1, 6 * D_MODEL), lambda i: (_tile_mod_row(i), 0, 0)),
                  pl.BlockSpec((1, SSM_WIDTH), lambda i: (0, 0)),
                  pl.BlockSpec((SSM_WIDTH, SSM_WIDTH), lambda i: (0, 0)),
                  pl.BlockSpec((D_MODEL, D_MODEL), lambda i: (0, 0)),
                  pl.BlockSpec((1, D_MODEL), lambda i: (0, 0)),
                  pl.BlockSpec((2, D_MODEL, 128), lambda i: (0, 0, 0))],
        out_specs=[pl.BlockSpec((TILE, D_MODEL), lambda i: (i, 0)),
                   pl.BlockSpec((TILE, D_MODEL), lambda i: (i, 0)),
                   pl.BlockSpec((128, TILE), lambda i: (0, i))],
        out_shape=[jax.ShapeDtypeStruct((N_TOK, D_MODEL), F32),
                   jax.ShapeDtypeStruct((N_TOK, D_MODEL), BF16),
                   jax.ShapeDtypeStruct((128, N_TOK), F32)],
        compiler_params=_cparams(("parallel",)),
        name="out_projection",
    )(x, rest, ys, attn, ret, mod_l, d_row, wglu_bf, wout_bf, nw2, wr_split)


def _lane_cumsum(x01):
    rows, n = x01.shape
    r = lax.broadcasted_iota(jnp.int32, (256, 256), 0)
    c = lax.broadcasted_iota(jnp.int32, (256, 256), 1)
    tri = jnp.where(r <= c, 1.0, 0.0).astype(BF16)
    off = jnp.zeros((rows, 1), F32)
    parts = []
    for j in range(n // 256):
        cs = jnp.dot(x01[:, 256 * j:256 * (j + 1)].astype(BF16), tri, preferred_element_type=F32) + off
        parts.append(cs)
        off = cs[:, 255:256]
    return jnp.concatenate(parts, axis=1)


def _count(m):
    return jnp.sum(jnp.where(m, 1.0, 0.0), axis=1, keepdims=True)


def _route_kernel(aff_ref, slot_ref, gate_ref, *, cap, seg):
    n_seg = aff_ref.shape[1] // seg
    segs = [slice(i * seg, (i + 1) * seg) for i in range(n_seg)]
    tiny = float(jnp.finfo(jnp.float32).tiny)

    def step(_, bounds):
        out = []
        for i in range(n_seg):
            lo, hi = bounds[2 * i], bounds[2 * i + 1]
            mid = jnp.where(lo > 0.0, jnp.sqrt(lo) * jnp.sqrt(hi), jnp.maximum(hi * (2.0 ** -16), tiny))
            mid = jnp.minimum(jnp.maximum(mid, lo), hi)
            ok = _count(aff_ref[:, segs[i]] >= mid) >= cap
            out += [jnp.where(ok, mid, lo), jnp.where(ok, hi, mid)]
        return tuple(out)

    init = (jnp.zeros((N_EXPERTS, 1), F32), jnp.full((N_EXPERTS, 1), 2.0, F32)) * n_seg
    bounds = lax.fori_loop(0, ROUTE_ITERS, step, init)

    for i in range(n_seg):
        a = aff_ref[:, segs[i]]
        lo, hi = bounds[2 * i], bounds[2 * i + 1]
        above = a >= hi
        band = (a >= lo) & (a < hi)
        sel = above | (band & (_lane_cumsum(jnp.where(band, 1.0, 0.0)) <= cap - _count(above)))
        slot = _lane_cumsum(jnp.where(sel, 1.0, 0.0)) - 1.0
        slot_ref[:, segs[i]] = jnp.where(sel, slot, -1.0).astype(jnp.int32)
        gate_ref[:, segs[i]] = jnp.where(sel, a, 0.0)


def _route(aff_t, col_block0, n_blocks, width, seg, cap):
    return pl.pallas_call(
        functools.partial(_route_kernel, cap=cap, seg=seg),
        grid=(n_blocks,),
        in_specs=[pl.BlockSpec((N_EXPERTS, width), lambda i: (0, col_block0 + i))],
        out_specs=[pl.BlockSpec((N_EXPERTS, width), lambda i: (0, i))] * 2,
        out_shape=[jax.ShapeDtypeStruct((N_EXPERTS, n_blocks * width), jnp.int32),
                   jax.ShapeDtypeStruct((N_EXPERTS, n_blocks * width), F32)],
        compiler_params=_cparams(("parallel",)),
        name="route",
    )(aff_t)


def _gather_lat_kernel(slot_ref, gate_ref, h_ref, o_ref, g_ref):
    e = pl.program_id(1)
    slot = slot_ref[pl.ds(e, 1), :]
    hit = lax.broadcasted_iota(jnp.int32, (CAP_LAT, slot.shape[1]), 0) == slot
    onehot = jnp.where(hit, 1.0, 0.0).astype(BF16)
    o_ref[...] = jnp.dot(onehot, h_ref[...], preferred_element_type=F32).astype(o_ref.dtype)
    g_ref[...] = jnp.sum(jnp.where(hit, gate_ref[pl.ds(e, 1), :], 0.0), axis=1, keepdims=True)


def _gather_lat(slot, gate, h2):
    rows = pl.BlockSpec((N_EXPERTS, DEC_SEQ), lambda r, e: (0, r))
    return pl.pallas_call(
        _gather_lat_kernel,
        grid=(DEC_BATCH, N_EXPERTS),
        in_specs=[rows, rows,
                  pl.BlockSpec((DEC_SEQ, D_MODEL), lambda r, e: (N_CTX // DEC_SEQ + r, 0))],
        out_specs=[pl.BlockSpec((None, CAP_LAT, D_MODEL), lambda r, e: (e, r, 0)),
                   pl.BlockSpec((None, CAP_LAT, 1), lambda r, e: (e, r, 0))],
        out_shape=[jax.ShapeDtypeStruct((N_EXPERTS, DEC_BATCH * CAP_LAT, D_MODEL), BF16),
                   jax.ShapeDtypeStruct((N_EXPERTS, DEC_BATCH * CAP_LAT, 1), F32)],
        compiler_params=_cparams(("parallel", "arbitrary")),
        name="gather_lat",
    )(slot, gate, h2)


def _ctx_onehot(slot):
    rows = lax.broadcasted_iota(jnp.int32, (CAP_CTX, SEQ), 0)
    hits = [rows == slot[e:e + 1, :] for e in range(N_EXPERTS)]
    onehot = jnp.concatenate([jnp.where(h, 1.0, 0.0) for h in hits], axis=0).astype(BF16)
    return onehot, hits


def _gather_ctx_kernel(slot_ref, gate_ref, h_ref, o_ref, g_ref):
    onehot, hits = _ctx_onehot(slot_ref[...])
    xs = jnp.dot(onehot, h_ref[...], preferred_element_type=F32).astype(o_ref.dtype)
    gate = gate_ref[...]
    for e in range(N_EXPERTS):
        o_ref[e] = xs[e * CAP_CTX:(e + 1) * CAP_CTX]
        g_ref[e] = jnp.sum(jnp.where(hits[e], gate[e:e + 1, :], 0.0), axis=1, keepdims=True)


def _gather_ctx(slot, gate, h2):
    rows = pl.BlockSpec((N_EXPERTS, SEQ), lambda r: (0, r))
    return pl.pallas_call(
        _gather_ctx_kernel,
        grid=(BATCH,),
        in_specs=[rows, rows, pl.BlockSpec((SEQ, D_MODEL), lambda r: (r, 0))],
        out_specs=[pl.BlockSpec((N_EXPERTS, CAP_CTX, D_MODEL), lambda r: (0, r, 0)),
                   pl.BlockSpec((N_EXPERTS, CAP_CTX, 1), lambda r: (0, r, 0))],
        out_shape=[jax.ShapeDtypeStruct((N_EXPERTS, BATCH * CAP_CTX, D_MODEL), BF16),
                   jax.ShapeDtypeStruct((N_EXPERTS, BATCH * CAP_CTX, 1), F32)],
        compiler_params=_cparams(("parallel",)),
        name="gather_ctx",
    )(slot, gate, h2)


def _ffn_kernel(xc_ref, xl_ref, gc_ref, gl_ref, wg_ref, wu_ref, wd_ref, yc_ref, yl_ref, accc_ref, accl_ref):
    f = pl.program_id(1)
    wg = wg_ref[...].astype(BF16)
    wu = wu_ref[...].astype(BF16)
    wd = wd_ref[...].astype(BF16)

    def part(x_ref, gate_ref, acc_ref, y_ref):
        x = x_ref[...]
        a = jnp.dot(x, wg, preferred_element_type=F32)
        up = jnp.dot(x, wu, preferred_element_type=F32)
        mid = (a * jax.nn.sigmoid(a) * up).astype(BF16)
        y = jnp.dot(mid, wd, preferred_element_type=F32)

        @pl.when(f == 0)
        def _():
            acc_ref[...] = y

        @pl.when(f > 0)
        def _():
            acc_ref[...] += y

        @pl.when(f == pl.num_programs(1) - 1)
        def _():
            y_ref[...] = (acc_ref[...] * gate_ref[...]).astype(y_ref.dtype)

    part(xc_ref, gc_ref, accc_ref, yc_ref)
    part(xl_ref, gl_ref, accl_ref, yl_ref)


def _expert_ffn(xs_ctx, xs_lat, gs_ctx, gs_lat, w_gate, w_up, w_down, layer):
    tf = 512
    nc, nl = xs_ctx.shape[1], xs_lat.shape[1]
    return pl.pallas_call(
        _ffn_kernel,
        grid=(N_EXPERTS, EXPERT_FF // tf),
        in_specs=[pl.BlockSpec((None, nc, D_MODEL), lambda e, f: (e, 0, 0)),
                  pl.BlockSpec((None, nl, D_MODEL), lambda e, f: (e, 0, 0)),
                  pl.BlockSpec((None, nc, 1), lambda e, f: (e, 0, 0)),
                  pl.BlockSpec((None, nl, 1), lambda e, f: (e, 0, 0)),
                  pl.BlockSpec((None, None, D_MODEL, tf), lambda e, f: (layer, e, 0, f)),
                  pl.BlockSpec((None, None, D_MODEL, tf), lambda e, f: (layer, e, 0, f)),
                  pl.BlockSpec((None, None, tf, D_MODEL), lambda e, f: (layer, e, f, 0))],
        out_specs=[pl.BlockSpec((None, nc, D_MODEL), lambda e, f: (e, 0, 0)),
                   pl.BlockSpec((None, nl, D_MODEL), lambda e, f: (e, 0, 0))],
        out_shape=[jax.ShapeDtypeStruct(xs_ctx.shape, BF16), jax.ShapeDtypeStruct(xs_lat.shape, BF16)],
        scratch_shapes=[pltpu.VMEM((nc, D_MODEL), F32), pltpu.VMEM((nl, D_MODEL), F32)],
        compiler_params=_cparams(("parallel", "arbitrary")),
        name="expert_ffn",
    )(xs_ctx, xs_lat, gs_ctx, gs_lat, w_gate, w_up, w_down)


def _scatter_lat_kernel(slot_ref, y_ref, x_ref, mod_ref, o_ref):
    slot_t = slot_ref[...].astype(F32).T
    lane = lax.broadcasted_iota(jnp.int32, (slot_t.shape[0], CAP_LAT), 1).astype(F32)
    onehot = jnp.concatenate([jnp.where(lane == slot_t[:, e:e + 1], 1.0, 0.0).astype(BF16)
                              for e in range(N_EXPERTS)], axis=1)
    y = y_ref[...].reshape(N_EXPERTS * CAP_LAT, D_MODEL)
    ffn = jnp.dot(onehot, y, preferred_element_type=F32)
    o_ref[...] = x_ref[...] + mod_ref[:, 5 * D_MODEL:6 * D_MODEL] * ffn


def _scatter_lat(slot, ys, x, mod_l):
    tt = 256
    nt = DEC_SEQ // tt
    blk0 = N_CTX // tt
    tok = pl.BlockSpec((tt, D_MODEL), lambda r, t: (blk0 + r * nt + t, 0))
    return pl.pallas_call(
        _scatter_lat_kernel,
        grid=(DEC_BATCH, nt),
        in_specs=[pl.BlockSpec((N_EXPERTS, tt), lambda r, t: (0, r * nt + t)),
                  pl.BlockSpec((N_EXPERTS, CAP_LAT, D_MODEL), lambda r, t: (0, r, 0)),
                  tok,
                  pl.BlockSpec((None, 1, 6 * D_MODEL), lambda r, t: (1 + r, 0, 0))],
        out_specs=tok,
        out_shape=jax.ShapeDtypeStruct(x.shape, F32),
        input_output_aliases={2: 0},
        compiler_params=_cparams(("parallel", "parallel")),
        name="scatter_lat",
    )(slot, ys, x, mod_l)


def _scatter_ctx_kernel(slot_ref, y_ref, x_ref, mod_ref, o_ref):
    onehot, _ = _ctx_onehot(slot_ref[...])
    y = jnp.concatenate([y_ref[e] for e in range(N_EXPERTS)], axis=0)
    ffn = lax.dot_general(onehot, y, TN, preferred_element_type=F32)
    o_ref[...] = x_ref[...] + mod_ref[:, 5 * D_MODEL:6 * D_MODEL] * ffn


def _scatter_ctx(slot, ys, x, mod_l):
    tok = pl.BlockSpec((SEQ, D_MODEL), lambda r: (r, 0))
    return pl.pallas_call(
        _scatter_ctx_kernel,
        grid=(BATCH,),
        in_specs=[pl.BlockSpec((N_EXPERTS, SEQ), lambda r: (0, r)),
                  pl.BlockSpec((N_EXPERTS, CAP_CTX, D_MODEL), lambda r: (0, r, 0)),
                  tok,
                  pl.BlockSpec((None, 1, 6 * D_MODEL), lambda r: (0, 0, 0))],
        out_specs=tok,
        out_shape=jax.ShapeDtypeStruct(x.shape, F32),
        input_output_aliases={2: 0},
        compiler_params=_cparams(("parallel",)),
        name="scatter_ctx",
    )(slot, ys, x, mod_l)


def _final_norm_kernel(x_ref, w_ref, o_ref):
    x = x_ref[...]
    o_ref[...] = x * lax.rsqrt(jnp.mean(x * x, axis=-1, keepdims=True) + EPS) * w_ref[...]


def _final_norm(x, w):
    return pl.pallas_call(
        _final_norm_kernel,
        grid=(N_TILES,),
        in_specs=[pl.BlockSpec((TILE, D_MODEL), lambda i: (i, 0)), pl.BlockSpec((1, D_MODEL), lambda i: (0, 0))],
        out_specs=pl.BlockSpec((TILE, D_MODEL), lambda i: (i, 0)),
        out_shape=jax.ShapeDtypeStruct(x.shape, F32),
        compiler_params=_cparams(("parallel",)),
        name="final_norm",
    )(x, w)


def _rope_tables():
    rows = DEC_SEQ // GRID_W
    row = jnp.repeat(jnp.arange(rows, dtype=F32), GRID_W)
    col = jnp.tile(jnp.arange(GRID_W, dtype=F32), rows)
    n_freq = HEAD_DIM // 4
    inv_freq = ROPE_THETA ** (-jnp.arange(n_freq, dtype=F32) / n_freq)
    ang = jnp.concatenate([row[:, None] * inv_freq, col[:, None] * inv_freq], axis=-1)
    cos, sin = jnp.cos(ang), jnp.sin(ang)
    cos_t = jnp.tile(jnp.concatenate([cos, cos], -1), (1, 128 // HEAD_DIM))
    sin_t = jnp.tile(jnp.concatenate([-sin, sin], -1), (1, 128 // HEAD_DIM))
    cos_t = jnp.concatenate([jnp.ones((TILE, 128), F32), cos_t], axis=0)
    sin_t = jnp.concatenate([jnp.zeros((TILE, 128), F32), sin_t], axis=0)
    return cos_t, sin_t


def _s5_initial_rows(state_ssm):
    st = state_ssm.astype(F32)
    re, im = st[..., 0], st[..., 1]
    both = jnp.stack([jnp.concatenate([re, im], -1), jnp.concatenate([im, re], -1)], axis=3)
    return both.transpose(1, 0, 2, 3, 4, 5).reshape(DEPTH, 4 * DEC_BATCH, N_SSM_GROUPS * 128)


def kernel(x_prompt, x_sample, cache_k, cache_v, state_ssm, state_ret, c, c_ctx, w_mod, b_mod, norm1_w, norm2_w, w_in, w_out, qn_w, kn_w, ssm_lambda_re, ssm_lambda_im, ssm_b_re, ssm_b_im, ssm_c_re, ssm_c_im, ssm_log_dt, ssm_d, ssm_w_glu, ret_decay_logit, ret_norm_w, w_router, w_gate, w_up, w_down, final_norm_w):
    x = jnp.concatenate([x_prompt.reshape(N_CTX, D_MODEL), x_sample.reshape(N_LAT, D_MODEL)], axis=0)
    cond_t = jnp.zeros((D_MODEL, 8), F32).at[:, 0].set(c_ctx).at[:, 1:1 + DEC_BATCH].set(c.T)
    mod = _modulation(cond_t, w_mod, b_mod).reshape(DEPTH, 8, 1, 6 * D_MODEL)
    cos_t, sin_t = _rope_tables()
    zero_ret = jnp.zeros((BATCH, 2, RET_HEADS, HEAD_DIM, HEAD_DIM), F32)
    ctx_blocks = N_CTX // DEC_SEQ

    w_in_bf, w_out_bf, w_glu_bf = w_in.astype(BF16), w_out.astype(BF16), ssm_w_glu.astype(BF16)
    s5_toe, s5_inj, s5_ro, s5_a = jax.vmap(_s5_matrices)(ssm_lambda_re, ssm_lambda_im, ssm_b_re, ssm_b_im,
                                                         ssm_c_re, ssm_c_im, ssm_log_dt)
    s5_h0 = _s5_initial_rows(state_ssm)
    ret_dec, ret_mask, ret_cdec = jax.vmap(_retention_tables)(ret_decay_logit)
    wr = jnp.pad(w_router.astype(F32), ((0, 0), (0, 0), (0, 128 - N_EXPERTS)))
    wr_hi = wr.astype(BF16)
    wr_split = jnp.stack([wr_hi, (wr - wr_hi.astype(F32)).astype(BF16)], axis=1)
    qn_t, kn_t = jnp.tile(qn_w, (1, 2)), jnp.tile(kn_w, (1, 2))
    cache_kv = jnp.concatenate([cache_k.reshape(DEC_BATCH, DEPTH, PAST_LEN, KV_WIDTH),
                                cache_v.reshape(DEC_BATCH, DEPTH, PAST_LEN, KV_WIDTH)], axis=-1).astype(BF16)

    ks, vs, ss, rs = [], [], [], []
    for l in range(DEPTH):
        mod_l = mod[l]
        q, kv, ub, rest = _in_projection(x, mod_l, norm1_w[l].reshape(1, -1), w_in_bf[l],
                                     qn_t[l].reshape(1, -1), kn_t[l].reshape(1, -1), cos_t, sin_t)
        ks.append(rest[:N_CTX, C_K:C_K + KV_WIDTH].reshape(BATCH, SEQ, N_KV_HEADS, HEAD_DIM))
        vs.append(rest[:N_CTX, C_V:C_V + KV_WIDTH].reshape(BATCH, SEQ, N_KV_HEADS, HEAD_DIM))

        kv_ctx = kv[:N_CTX].reshape(BATCH, SEQ, 2 * KV_WIDTH)
        kv_lat = jnp.concatenate([kv[N_CTX:].reshape(DEC_BATCH, DEC_SEQ, 2 * KV_WIDTH), cache_kv[:, l]], axis=1)
        attn = _attention(q, kv_ctx, 0, BATCH, SEQ, SEQ)
        attn = _attention(q, kv_lat, N_CTX // 256, DEC_BATCH, DEC_SEQ, 256, prev=attn)

        y_loc, states = _s5_local(ub.reshape(S5_ROWS, S5_CHUNK * SSM_WIDTH), s5_toe[l], s5_inj[l])
        hf, hb, fin_f, fin_b = _s5_scan(states, s5_a[l], s5_h0[l])
        ys = _s5_out(y_loc, hf, hb, s5_ro[l]).reshape(N_TOK, SSM_WIDTH)
        fin = jnp.stack([fin_f, fin_b], axis=1).reshape(BATCH, 2, N_SSM_GROUPS, 2, SSM_STATE)
        ss.append(fin.transpose(0, 1, 2, 4, 3))

        nw_ret = ret_norm_w[l].reshape(1, -1)
        ret, fin_ret = _retention(rest, ret_dec[l], ret_mask[l], ret_cdec[l], zero_ret, nw_ret, 0, BATCH, SEQ)
        ret, _ = _retention(rest, ret_dec[l], ret_mask[l], ret_cdec[l], state_ret[:, l].astype(F32), nw_ret,
                            ctx_blocks, DEC_BATCH, DEC_SEQ, prev=ret)
        rs.append(fin_ret)

        x1, h2, aff_t = _out_projection(x, rest, ys, attn, ret, mod_l, ssm_d[l].reshape(1, -1),
                                        w_glu_bf[l], w_out_bf[l], norm2_w[l].reshape(1, -1), wr_split[l])

        slot_ctx, gate_ctx = _route(aff_t, 0, 1, N_CTX, SEQ, CAP_CTX)
        slot_lat, gate_lat = _route(aff_t, N_CTX // DEC_SEQ, DEC_BATCH, DEC_SEQ, DEC_SEQ, CAP_LAT)
        xs_ctx, gs_ctx = _gather_ctx(slot_ctx, gate_ctx, h2)
        xs_lat, gs_lat = _gather_lat(slot_lat, gate_lat, h2)
        y_ctx, y_lat = _expert_ffn(xs_ctx, xs_lat, gs_ctx, gs_lat, w_gate, w_up, w_down, l)
        x = _scatter_ctx(slot_ctx, y_ctx, x1, mod_l)
        x = _scatter_lat(slot_lat, y_lat, x, mod_l)

    y = _final_norm(x, final_norm_w.reshape(1, -1))
    y_prompt = y[:N_CTX].reshape(BATCH, SEQ, D_MODEL)
    y_sample = y[N_CTX:].reshape(DEC_BATCH, DEC_SEQ, D_MODEL)
    return (y_prompt, y_sample, jnp.stack(ks, axis=1), jnp.stack(vs, axis=1),
            jnp.stack(ss, axis=1), jnp.stack(rs, axis=1))
```

```python
import functools

import jax
import jax.numpy as jnp
from jax import lax
from jax.experimental import pallas as pl
from jax.experimental.pallas import tpu as pltpu

F32 = jnp.float32
BF16 = jnp.bfloat16

D_MODEL = 1024
BATCH = 16
SEQ = 256
DEPTH = 4
DEC_BATCH = 2
DEC_SEQ = 4096
PAST_LEN = 256
GRID_W = 64
HEAD_DIM = 64
SSM_WIDTH = 256
SSM_GROUP = 16
N_SSM_GROUPS = 16
SSM_STATE = 64
ATT_WIDTH = 512
N_HEADS = 8
N_KV_HEADS = 2
KV_WIDTH = 128
RET_WIDTH = 256
RET_HEADS = 4
IN_WIDTH = 2048
RET_CHUNK = 128
N_EXPERTS = 16
EXPERT_FF = 1024
ROPE_THETA = 10000.0
EPS = 1e-6

N_CTX = BATCH * SEQ
N_LAT = DEC_BATCH * DEC_SEQ
N_TOK = N_CTX + N_LAT
TILE = 256
N_TILES = N_TOK // TILE
CTX_TILES = N_CTX // TILE
LAT_TILES_PER_REQ = DEC_SEQ // TILE
CAP_CTX = 2 * SEQ // N_EXPERTS
CAP_LAT = 2 * DEC_SEQ // N_EXPERTS
S5_CHUNK = 16
CTX_CHUNKS = SEQ // S5_CHUNK
LAT_CHUNKS = DEC_SEQ // S5_CHUNK
S5_ROWS_CTX = BATCH * CTX_CHUNKS
S5_ROWS = S5_ROWS_CTX + DEC_BATCH * LAT_CHUNKS
C_U, C_K, C_V, C_RQ, C_RK, C_RV, C_RG = 0, 256, 384, 512, 768, 1024, 1280
REST_WIDTH = 1536
ROUTE_ITERS = 48
VMEM_LIMIT = 56 * 1024 * 1024

TN = (((0,), (0,)), ((), ()))
NT = (((1,), (1,)), ((), ()))


def _cparams(sem):
    return pltpu.CompilerParams(dimension_semantics=sem, vmem_limit_bytes=VMEM_LIMIT)


def _tile_mod_row(i):
    return jnp.where(i < CTX_TILES, 0, 1 + (i - CTX_TILES) // LAT_TILES_PER_REQ)


def _split_dot(v, m):
    hi = v.astype(BF16)
    lo = (v - hi.astype(F32)).astype(BF16)
    return (jnp.dot(hi, m, preferred_element_type=F32)
            + jnp.dot(lo, m, preferred_element_type=F32))


def _group_avg_matrix():
    r = lax.broadcasted_iota(jnp.int32, (128, 128), 0) // HEAD_DIM
    c = lax.broadcasted_iota(jnp.int32, (128, 128), 1) // HEAD_DIM
    return jnp.where(r == c, 1.0 / HEAD_DIM, 0.0).astype(BF16)


def _mod_kernel(ct_ref, w_ref, b_ref, o_ref):
    c = ct_ref[...]
    s = c * jax.nn.sigmoid(c)
    w = w_ref[...]
    rows = [jnp.sum(w * s[:, r:r + 1], axis=0, keepdims=True) for r in range(3)]
    rows.append(jnp.zeros((5, w.shape[1]), F32))
    o_ref[...] = jnp.concatenate(rows, axis=0) + b_ref[...]


def _modulation(cond_t, w_mod, b_mod):
    tn = 512
    n = 6 * D_MODEL
    return pl.pallas_call(
        _mod_kernel,
        grid=(DEPTH, n // tn),
        in_specs=[pl.BlockSpec((D_MODEL, 8), lambda l, j: (0, 0)),
                  pl.BlockSpec((None, D_MODEL, tn), lambda l, j: (l, 0, j)),
                  pl.BlockSpec((None, 1, tn), lambda l, j: (l, 0, j))],
        out_specs=pl.BlockSpec((None, 8, tn), lambda l, j: (l, 0, j)),
        out_shape=jax.ShapeDtypeStruct((DEPTH, 8, n), F32),
        compiler_params=_cparams(("arbitrary", "arbitrary")),
        name="modulation",
    )(cond_t, w_mod, b_mod.reshape(DEPTH, 1, n))


def _inproj_kernel(x_ref, mod_ref, nw_ref, w_ref, qn_ref, kn_ref, cos_ref, sin_ref, q_ref, kv_ref, ub_ref, rest_ref):
    x = x_ref[...]
    shift = mod_ref[:, 0:D_MODEL]
    scale = mod_ref[:, D_MODEL:2 * D_MODEL]
    y = x * lax.rsqrt(jnp.mean(x * x, axis=-1, keepdims=True) + EPS) * nw_ref[...]
    h = y * (1.0 + scale) + shift
    proj = jnp.dot(h.astype(BF16), w_ref[...], preferred_element_type=F32)

    avg = _group_avg_matrix()
    cos = cos_ref[...]
    sin = sin_ref[...]
    first_half = (lax.broadcasted_iota(jnp.int32, (TILE, 128), 1) % HEAD_DIM) < (HEAD_DIM // 2)

    def head_norm(z, wrow):
        return z * lax.rsqrt(_split_dot(z * z, avg) + EPS) * wrow

    def rope(z):
        partner = jnp.where(first_half, pltpu.roll(z, 128 - HEAD_DIM // 2, 1), pltpu.roll(z, HEAD_DIM // 2, 1))
        return z * cos + partner * sin

    def col(off, j):
        return proj[:, off + 128 * j: off + 128 * (j + 1)]

    qn = qn_ref[...]
    for j in range(ATT_WIDTH // 128):
        z = rope(head_norm(col(SSM_WIDTH, j), qn)) * (HEAD_DIM ** -0.5)
        q_ref[:, 128 * j:128 * (j + 1)] = z.astype(q_ref.dtype)
    p_k = SSM_WIDTH + ATT_WIDTH
    rest_ref[:, C_U:C_U + SSM_WIDTH] = proj[:, 0:SSM_WIDTH]
    ub_ref[...] = proj[:, 0:SSM_WIDTH].astype(ub_ref.dtype)
    k = rope(head_norm(col(p_k, 0), kn_ref[...]))
    v = col(p_k + KV_WIDTH, 0)
    rest_ref[:, C_K:C_K + KV_WIDTH] = k
    rest_ref[:, C_V:C_V + KV_WIDTH] = v
    kv_ref[:, 0:KV_WIDTH] = k.astype(kv_ref.dtype)
    kv_ref[:, KV_WIDTH:2 * KV_WIDTH] = v.astype(kv_ref.dtype)
    p_r = p_k + 2 * KV_WIDTH
    for j in range(RET_WIDTH // 128):
        rest_ref[:, C_RQ + 128 * j:C_RQ + 128 * (j + 1)] = rope(col(p_r, j))
        rest_ref[:, C_RK + 128 * j:C_RK + 128 * (j + 1)] = rope(col(p_r + RET_WIDTH, j)) * (HEAD_DIM ** -0.5)
    rest_ref[:, C_RV:C_RV + 2 * RET_WIDTH] = proj[:, p_r + 2 * RET_WIDTH:p_r + 4 * RET_WIDTH]


def _in_projection(x, mod_l, nw, w_in_bf, qn, kn, cos_t, sin_t):
    def rope_blk(i):
        return (jnp.where(i < CTX_TILES, 0, 1 + (i - CTX_TILES) % LAT_TILES_PER_REQ), 0)
    return pl.pallas_call(
        _inproj_kernel,
        grid=(N_TILES,),
        in_specs=[pl.BlockSpec((TILE, D_MODEL), lambda i: (i, 0)),
                  pl.BlockSpec((None, 1, 6 * D_MODEL), lambda i: (_tile_mod_row(i), 0, 0)),
                  pl.BlockSpec((1, D_MODEL), lambda i: (0, 0)),
                  pl.BlockSpec((D_MODEL, IN_WIDTH), lambda i: (0, 0)),
                  pl.BlockSpec((1, 128), lambda i: (0, 0)),
                  pl.BlockSpec((1, 128), lambda i: (0, 0)),
                  pl.BlockSpec((TILE, 128), rope_blk),
                  pl.BlockSpec((TILE, 128), rope_blk)],
        out_specs=[pl.BlockSpec((TILE, ATT_WIDTH), lambda i: (i, 0)),
                   pl.BlockSpec((TILE, 2 * KV_WIDTH), lambda i: (i, 0)),
                   pl.BlockSpec((TILE, SSM_WIDTH), lambda i: (i, 0)),
                   pl.BlockSpec((TILE, REST_WIDTH), lambda i: (i, 0))],
        out_shape=[jax.ShapeDtypeStruct((N_TOK, ATT_WIDTH), BF16),
                   jax.ShapeDtypeStruct((N_TOK, 2 * KV_WIDTH), BF16),
                   jax.ShapeDtypeStruct((N_TOK, SSM_WIDTH), BF16),
                   jax.ShapeDtypeStruct((N_TOK, REST_WIDTH), F32)],
        compiler_params=_cparams(("parallel",)),
        name="in_projection",
    )(x, mod_l, nw, w_in_bf, qn, kn, cos_t, sin_t)


def _attn_kernel(q_ref, k_ref, v_ref, o_ref):
    group = N_HEADS // N_KV_HEADS
    for kv in range(N_KV_HEADS):
        k = k_ref[:, kv * HEAD_DIM:(kv + 1) * HEAD_DIM]
        v = v_ref[:, kv * HEAD_DIM:(kv + 1) * HEAD_DIM]
        for g in range(group):
            cols = slice((kv * group + g) * HEAD_DIM, (kv * group + g + 1) * HEAD_DIM)
            s = lax.dot_general(q_ref[:, cols], k, NT, preferred_element_type=F32)
            m = jnp.max(s, axis=-1, keepdims=True)
            p = jnp.exp(s - m)
            l = jnp.sum(p, axis=-1, keepdims=True)
            o = jnp.dot(p.astype(BF16), v, preferred_element_type=F32) / l
            o_ref[:, cols] = o.astype(o_ref.dtype)


def _attention(q, kv, row_block0, n_req, lq, tq):
    lk = kv.shape[1]
    nq = lq // tq
    return pl.pallas_call(
        _attn_kernel,
        grid=(n_req, nq),
        in_specs=[pl.BlockSpec((tq, ATT_WIDTH), lambda r, i: (row_block0 + r * nq + i, 0)),
                  pl.BlockSpec((None, lk, KV_WIDTH), lambda r, i: (r, 0, 0)),
                  pl.BlockSpec((None, lk, KV_WIDTH), lambda r, i: (r, 0, 1))],
        out_specs=pl.BlockSpec((tq, ATT_WIDTH), lambda r, i: (r * nq + i, 0)),
        out_shape=jax.ShapeDtypeStruct((n_req * lq, ATT_WIDTH), BF16),
        compiler_params=_cparams(("parallel", "parallel")),
        name="attention",
    )(q, kv, kv)


def _s5_local_kernel(u_ref, t_ref, b_ref, y_ref, s_ref):
    u = u_ref[...]
    y_ref[...] = jnp.dot(u, t_ref[...], preferred_element_type=F32)
    s_ref[...] = jnp.dot(u, b_ref[...], preferred_element_type=F32)


def _s5_local(u2, toeplitz, inject):
    tn = 256
    n = S5_CHUNK * SSM_WIDTH
    wspec = pl.BlockSpec((n, tn), lambda i: (0, i))
    ospec = pl.BlockSpec((S5_ROWS, tn), lambda i: (0, i))
    return pl.pallas_call(
        _s5_local_kernel,
        grid=(n // tn,),
        in_specs=[pl.BlockSpec((S5_ROWS, n), lambda i: (0, 0)), wspec, wspec],
        out_specs=[ospec, ospec],
        out_shape=[jax.ShapeDtypeStruct((S5_ROWS, n), F32)] * 2,
        compiler_params=_cparams(("parallel",)),
        name="s5_local",
    )(u2, toeplitz, inject)


def _s5_scan_kernel(sf_ref, sb_ref, a_ref, h0_ref, hf_ref, hb_ref, ff_ref, fb_ref, sfs_ref, sbs_ref):
    a1f, a2f, a1b, a2b = a_ref[0:1, :], a_ref[1:2, :], a_ref[2:3, :], a_ref[3:4, :]
    w = sf_ref.shape[1]
    for c in range(w // 128):
        cols = slice(128 * c, 128 * (c + 1))
        sfs_ref[:, cols] = pltpu.roll(sf_ref[:, cols], SSM_STATE, 1)
        sbs_ref[:, cols] = pltpu.roll(sb_ref[:, cols], SSM_STATE, 1)

    zero = jnp.zeros((1, w), F32)
    for r in range(BATCH):
        hf, hfs, hb, hbs = zero, zero, zero, zero
        for j in range(CTX_CHUNKS):
            row = r * CTX_CHUNKS + j
            hf_ref[row:row + 1, :] = hf
            hf, hfs = (a1f * hf + a2f * hfs + sf_ref[row:row + 1, :], a1f * hfs - a2f * hf + sfs_ref[row:row + 1, :])
            row = r * CTX_CHUNKS + CTX_CHUNKS - 1 - j
            hb_ref[row:row + 1, :] = hb
            hb, hbs = (a1b * hb + a2b * hbs + sb_ref[row:row + 1, :], a1b * hbs - a2b * hb + sbs_ref[row:row + 1, :])
        ff_ref[r:r + 1, :] = hf
        fb_ref[r:r + 1, :] = hb

    def body(j, carry):
        out = []
        for r in range(DEC_BATCH):
            hf, hfs, hb, hbs = carry[4 * r:4 * r + 4]
            row = pl.ds(S5_ROWS_CTX + r * LAT_CHUNKS + j, 1)
            hf_ref[row, :] = hf
            nf = a1f * hf + a2f * hfs + sf_ref[row, :]
            nfs = a1f * hfs - a2f * hf + sfs_ref[row, :]
            row = pl.ds(S5_ROWS_CTX + r * LAT_CHUNKS + LAT_CHUNKS - 1 - j, 1)
            hb_ref[row, :] = hb
            nb = a1b * hb + a2b * hbs + sb_ref[row, :]
            nbs = a1b * hbs - a2b * hb + sbs_ref[row, :]
            out += [nf, nfs, nb, nbs]
        return tuple(out)

    init = tuple(h0_ref[i:i + 1, :] for i in range(4 * DEC_BATCH))
    lax.fori_loop(0, LAT_CHUNKS, body, init)


def _s5_scan(states, a_rows, h0_rows):
    w = 512
    full = N_SSM_GROUPS * 128
    nb = full // w
    sspec = pl.BlockSpec((S5_ROWS, w), lambda i: (0, i))
    fspec = pl.BlockSpec((BATCH, w), lambda i: (0, i))
    return pl.pallas_call(
        _s5_scan_kernel,
        grid=(nb,),
        in_specs=[sspec,
                  pl.BlockSpec((S5_ROWS, w), lambda i: (0, nb + i)),
                  pl.BlockSpec((8, w), lambda i: (0, i)),
                  pl.BlockSpec((8, w), lambda i: (0, i))],
        out_specs=[sspec, sspec, fspec, fspec],
        out_shape=[jax.ShapeDtypeStruct((S5_ROWS, full), F32)] * 2 + [jax.ShapeDtypeStruct((BATCH, full), F32)] * 2,
        scratch_shapes=[pltpu.VMEM((S5_ROWS, w), F32), pltpu.VMEM((S5_ROWS, w), F32)],
        compiler_params=_cparams(("parallel",)),
        name="s5_scan",
    )(states, states, a_rows, h0_rows)


def _s5_out_kernel(y_ref, hf_ref, hb_ref, c_ref, o_ref, h_ref):
    full = N_SSM_GROUPS * 128

    @pl.when(pl.program_id(0) == 0)
    def _():
        h_ref[:, 0:full] = hf_ref[...].astype(BF16)
        h_ref[:, full:2 * full] = hb_ref[...].astype(BF16)

    o_ref[...] = y_ref[...] + jnp.dot(h_ref[...], c_ref[...], preferred_element_type=F32)


def _s5_out(y_loc, hf, hb, readout):
    tn = 256
    n = S5_CHUNK * SSM_WIDTH
    full = N_SSM_GROUPS * 128
    hspec = pl.BlockSpec((S5_ROWS, full), lambda i: (0, 0))
    return pl.pallas_call(
        _s5_out_kernel,
        grid=(n // tn,),
        in_specs=[pl.BlockSpec((S5_ROWS, tn), lambda i: (0, i)), hspec, hspec,
                  pl.BlockSpec((2 * full, tn), lambda i: (0, i))],
        out_specs=pl.BlockSpec((S5_ROWS, tn), lambda i: (0, i)),
        out_shape=jax.ShapeDtypeStruct((S5_ROWS, n), F32),
        scratch_shapes=[pltpu.VMEM((S5_ROWS, 2 * full), BF16)],
        compiler_params=_cparams(("arbitrary",)),
        name="s5_out",
    )(y_loc, hf, hb, readout)


def _s5_matrices(lam_re, lam_im, b_re, b_im, c_re, c_im, log_dt):
    hp = lax.Precision.HIGHEST
    n = S5_CHUNK
    tau = jnp.arange(n + 1, dtype=F32)
    toes, injs, ros, a_rows = [], [], [], []
    for di in range(2):
        lr, li = lam_re[di].astype(F32), lam_im[di].astype(F32)
        dt = jnp.exp(log_dt[di].astype(F32))[:, None]
        mag = jnp.exp(lr * dt * tau[:, None, None])
        ang = li * dt * tau[:, None, None]
        e_re, e_im = mag * jnp.cos(ang), mag * jnp.sin(ang)
        nr, ni = e_re[1] - 1.0, e_im[1]
        den = lr * lr + li * li
        f_re, f_im = (nr * lr + ni * li) / den, (ni * lr - nr * li) / den
        br, bi = b_re[di].astype(F32), b_im[di].astype(F32)
        bb_re = f_re[..., None] * br - f_im[..., None] * bi
        bb_im = f_re[..., None] * bi + f_im[..., None] * br
        cr, ci = c_re[di].astype(F32), c_im[di].astype(F32)
        ce_re = cr[None] * e_re[:n, :, None, :] - ci[None] * e_im[:n, :, None, :]
        ce_im = cr[None] * e_im[:n, :, None, :] + ci[None] * e_re[:n, :, None, :]
        kern = (jnp.einsum('tgcp,gpd->tgcd', ce_re, bb_re, precision=hp)
                - jnp.einsum('tgcp,gpd->tgcd', ce_im, bb_im, precision=hp))
        kp = jnp.concatenate([jnp.zeros_like(kern), kern], axis=0)
        if di == 0:
            toe = jnp.stack([kp[n - s:2 * n - s] for s in range(n)], axis=0)
        else:
            toe = jnp.stack([kp[s + 1:s + n + 1][::-1] for s in range(n)], axis=0)
        toes.append(toe.transpose(0, 2, 4, 1, 3))
        pe_re, pe_im = (e_re[:n][::-1], e_im[:n][::-1]) if di == 0 else (e_re[:n], e_im[:n])
        inj_re = pe_re[..., None] * bb_re[None] - pe_im[..., None] * bb_im[None]
        inj_im = pe_re[..., None] * bb_im[None] + pe_im[..., None] * bb_re[None]
        injs.append(jnp.concatenate([inj_re.transpose(0, 1, 3, 2), inj_im.transpose(0, 1, 3, 2)], -1))
        qe_re, qe_im = (e_re[1:], e_im[1:]) if di == 0 else (e_re[1:][::-1], e_im[1:][::-1])
        ro_re = cr[None] * qe_re[:, :, None, :] - ci[None] * qe_im[:, :, None, :]
        ro_im = cr[None] * qe_im[:, :, None, :] + ci[None] * qe_re[:, :, None, :]
        ros.append(jnp.concatenate([ro_re.transpose(1, 3, 0, 2), -ro_im.transpose(1, 3, 0, 2)], axis=1))
        a_re, a_im = e_re[n], e_im[n]
        a_rows.append(jnp.concatenate([a_re, a_re], -1).reshape(1, -1))
        a_rows.append(jnp.concatenate([-a_im, a_im], -1).reshape(1, -1))
    size = n * SSM_WIDTH
    g = N_SSM_GROUPS
    row = jnp.arange(size)[:, None]
    col = jnp.arange(size)[None, :]
    src = jnp.arange(n * SSM_GROUP)[:, None]

    def spread(table, copy, keep):
        wide = jnp.dot(table.astype(BF16), copy.astype(BF16), preferred_element_type=F32)
        return jnp.where(keep, wide, 0.0).astype(BF16)

    copy_tc = (src // SSM_GROUP == col // SSM_WIDTH) & (src % SSM_GROUP == col % SSM_GROUP)
    copy_dp = (src // 128 == col // (g * 128)) & (src % 128 == col % 128)
    toe = (toes[0] + toes[1]).reshape(size, n * SSM_GROUP)
    toeplitz = spread(toe, copy_tc, (row // SSM_GROUP) % g == (col // SSM_GROUP) % g)
    inj = jnp.stack(injs, axis=3).reshape(size, 2 * 128)
    inject = spread(inj, copy_dp, (row // SSM_GROUP) % g == (col // 128) % g)
    ro = jnp.stack(ros, axis=0).reshape(size, n * SSM_GROUP)
    readout = spread(ro, copy_tc, (row // 128) % g == (col // SSM_GROUP) % g)
    a_rows = jnp.concatenate(a_rows + [jnp.zeros((4, N_SSM_GROUPS * 128), F32)], axis=0)
    return toeplitz, inject, readout, a_rows


def _ret_kernel(q_ref, k_ref, v_ref, g_ref, dec_ref, mask_ref, cd_ref, s0_ref, nw_ref, o_ref, fin_ref,
                kvf_ref, kvb_ref, *, n_chunks):
    hd = HEAD_DIM
    nh = 2

    def local_state(i, _):
        rows = pl.ds(pl.multiple_of(i * RET_CHUNK, RET_CHUNK), RET_CHUNK)
        k = k_ref[rows, :]
        v = v_ref[rows, :].astype(BF16)
        kf = (k * dec_ref[1]).astype(BF16)
        kb = (k * dec_ref[3]).astype(BF16)
        for h in range(nh):
            ls = slice(h * hd, (h + 1) * hd)
            kvf_ref[i, h] = lax.dot_general(kf[:, ls], v[:, ls], TN, preferred_element_type=F32)
            kvb_ref[i, h] = lax.dot_general(kb[:, ls], v[:, ls], TN, preferred_element_type=F32)
        return 0

    lax.fori_loop(0, n_chunks, local_state, 0, unroll=min(4, n_chunks))

    def scan_f(i, s):
        loc = kvf_ref[i]
        kvf_ref[i] = s
        return cd_ref[0] * s + loc

    def scan_b(i, s):
        j = n_chunks - 1 - i
        loc = kvb_ref[j]
        kvb_ref[j] = s
        return cd_ref[1] * s + loc

    fin_ref[0] = lax.fori_loop(0, n_chunks, scan_f, s0_ref[0], unroll=min(4, n_chunks))
    fin_ref[1] = lax.fori_loop(0, n_chunks, scan_b, s0_ref[1], unroll=min(4, n_chunks))

    avg = _group_avg_matrix()

    def outputs(i, _):
        rows = pl.ds(pl.multiple_of(i * RET_CHUNK, RET_CHUNK), RET_CHUNK)
        q = q_ref[rows, :]
        qb = q.astype(BF16)
        kb = k_ref[rows, :].astype(BF16)
        v = v_ref[rows, :].astype(BF16)
        qf = (q * dec_ref[0]).astype(BF16)
        qr = (q * dec_ref[2]).astype(BF16)
        outs = []
        for h in range(nh):
            ls = slice(h * hd, (h + 1) * hd)
            inner = lax.dot_general(qb[:, ls], kb[:, ls], NT, preferred_element_type=F32) * mask_ref[h]
            o = jnp.dot(inner.astype(BF16), v[:, ls], preferred_element_type=F32)
            o += jnp.dot(qf[:, ls], kvf_ref[i, h].astype(BF16), preferred_element_type=F32)
            o += jnp.dot(qr[:, ls], kvb_ref[i, h].astype(BF16), preferred_element_type=F32)
            outs.append(o)
        o = jnp.concatenate(outs, axis=1)
        d = o - _split_dot(o, avg)
        o = d * lax.rsqrt(_split_dot(d * d, avg) + EPS) * nw_ref[...]
        g = g_ref[rows, :]
        o_ref[rows, :] = (g * jax.nn.sigmoid(g) * o).astype(o_ref.dtype)
        return 0

    lax.fori_loop(0, n_chunks, outputs, 0, unroll=2)


def _retention(rest, dec, mask, cdec, s0, nw, row_block0, n_req, length):
    n_chunks = length // RET_CHUNK
    hp = RET_HEADS // 2

    def tok(cb):
        return pl.BlockSpec((length, 128), lambda r, p: (row_block0 + r, cb + p))

    state = pl.BlockSpec((None, 2, 2, HEAD_DIM, HEAD_DIM), lambda r, p: (r, 0, p, 0, 0))
    return pl.pallas_call(
        functools.partial(_ret_kernel, n_chunks=n_chunks),
        grid=(n_req, hp),
        in_specs=[tok(C_RQ // 128), tok(C_RK // 128), tok(C_RV // 128), tok(C_RG // 128),
                  pl.BlockSpec((4, RET_CHUNK, 128), lambda r, p: (0, 0, p)),
                  pl.BlockSpec((2, RET_CHUNK, RET_CHUNK), lambda r, p: (p, 0, 0)),
                  pl.BlockSpec((2, 2, HEAD_DIM, HEAD_DIM), lambda r, p: (0, p, 0, 0)),
                  state,
                  pl.BlockSpec((1, 128), lambda r, p: (0, p))],
        out_specs=[pl.BlockSpec((length, 128), lambda r, p: (r, p)), state],
        out_shape=[jax.ShapeDtypeStruct((n_req * length, RET_WIDTH), BF16),
                   jax.ShapeDtypeStruct((n_req, 2, RET_HEADS, HEAD_DIM, HEAD_DIM), F32)],
        scratch_shapes=[pltpu.VMEM((n_chunks, 2, HEAD_DIM, HEAD_DIM), F32),
                        pltpu.VMEM((n_chunks, 2, HEAD_DIM, HEAD_DIM), F32)],
        compiler_params=_cparams(("parallel", "parallel")),
        name="retention",
    )(rest, rest, rest, rest, dec, mask, cdec, s0, nw)


def _retention_tables(decay_logit):
    lg = jax.nn.log_sigmoid(decay_logit.astype(F32))
    idx = jnp.arange(RET_CHUNK, dtype=F32)
    rel = idx[:, None] - idx[None, :]
    d_f = jnp.where(rel >= 0, jnp.exp(lg[0][:, None, None] * jnp.maximum(rel, 0.0)), 0.0)
    d_b = jnp.where(rel <= 0, jnp.exp(lg[1][:, None, None] * jnp.maximum(-rel, 0.0)), 0.0)
    mask = d_f + d_b

    def lanes(t):
        return jnp.repeat(t.T, HEAD_DIM, axis=1)

    dec = jnp.stack([lanes(jnp.exp(lg[0][:, None] * (idx + 1.0))),
                     lanes(jnp.exp(lg[0][:, None] * (RET_CHUNK - 1.0 - idx))),
                     lanes(jnp.exp(lg[1][:, None] * (RET_CHUNK - idx))),
                     lanes(jnp.exp(lg[1][:, None] * idx))], axis=0)
    cdec = jnp.broadcast_to(jnp.exp(lg * RET_CHUNK)[:, :, None, None], (2, RET_HEADS, HEAD_DIM, HEAD_DIM))
    return dec, mask, cdec


def _outproj_kernel(x_ref, rest_ref, ys_ref, atc_ref, atl_ref, rtc_ref, rtl_ref, mod_ref, d_ref, wglu_ref, wout_ref,
                    nw_ref, wr_ref, x1_ref, h2_ref, aff_ref):
    y = ys_ref[...] + rest_ref[...] * d_ref[...]
    y = jax.nn.gelu(y)
    y = y * jax.nn.sigmoid(jnp.dot(y.astype(BF16), wglu_ref[...], preferred_element_type=F32))
    is_ctx = pl.program_id(0) < CTX_TILES
    attn = jnp.where(is_ctx, atc_ref[...], atl_ref[...])
    ret = jnp.where(is_ctx, rtc_ref[...], rtl_ref[...])
    mix = (jnp.dot(y.astype(BF16), wout_ref[0:SSM_WIDTH, :], preferred_element_type=F32)
           + jnp.dot(attn, wout_ref[SSM_WIDTH:SSM_WIDTH + ATT_WIDTH, :], preferred_element_type=F32)
           + jnp.dot(ret, wout_ref[SSM_WIDTH + ATT_WIDTH:, :], preferred_element_type=F32))
    gate1 = mod_ref[:, 2 * D_MODEL:3 * D_MODEL]
    shift2 = mod_ref[:, 3 * D_MODEL:4 * D_MODEL]
    scale2 = mod_ref[:, 4 * D_MODEL:5 * D_MODEL]
    x1 = x_ref[...] + gate1 * mix
    x1_ref[...] = x1
    h2 = (x1 * lax.rsqrt(jnp.mean(x1 * x1, axis=-1, keepdims=True) + EPS) * nw_ref[...]) * (1.0 + scale2) + shift2
    h2_ref[...] = h2.astype(h2_ref.dtype)
    hi = h2.astype(BF16)
    lo = (h2 - hi.astype(F32)).astype(BF16)
    logits = (jnp.dot(hi, wr_ref[0], preferred_element_type=F32)
              + jnp.dot(lo, wr_ref[0], preferred_element_type=F32)
              + jnp.dot(hi, wr_ref[1], preferred_element_type=F32))
    valid = lax.broadcasted_iota(jnp.int32, logits.shape, 1) < N_EXPERTS
    logits = jnp.where(valid, logits, -1e30)
    e = jnp.exp(logits - jnp.max(logits, axis=-1, keepdims=True))
    aff = e / jnp.sum(e, axis=-1, keepdims=True)
    aff_ref[...] = aff.T


def _out_projection(x, rest, ys, attn_ctx, attn_lat, ret_ctx, ret_lat, mod_l, d_row, wglu_bf, wout_bf, nw2, wr_split):
    ctx_blk = lambda i: (jnp.minimum(i, CTX_TILES - 1), 0)
    lat_blk = lambda i: (jnp.maximum(i - CTX_TILES, 0), 0)
    return pl.pallas_call(
        _outproj_kernel,
        grid=(N_TILES,),
        in_specs=[pl.BlockSpec((TILE, D_MODEL), lambda i: (i, 0)),
                  pl.BlockSpec((TILE, SSM_WIDTH), lambda i: (i, C_U // SSM_WIDTH)),
                  pl.BlockSpec((TILE, SSM_WIDTH), lambda i: (i, 0)),
                  pl.BlockSpec((TILE, ATT_WIDTH), ctx_blk),
                  pl.BlockSpec((TILE, ATT_WIDTH), lat_blk),
                  pl.BlockSpec((TILE, RET_WIDTH), ctx_blk),
                  pl.BlockSpec((TILE, RET_WIDTH), lat_blk),
                  pl.BlockSpec((None, 1, 6 * D_MODEL), lambda i: (_tile_mod_row(i), 0, 0)),
                  pl.BlockSpec((1, SSM_WIDTH), lambda i: (0, 0)),
                  pl.BlockSpec((SSM_WIDTH, SSM_WIDTH), lambda i: (0, 0)),
                  pl.BlockSpec((D_MODEL, D_MODEL), lambda i: (0, 0)),
                  pl.BlockSpec((1, D_MODEL), lambda i: (0, 0)),
                  pl.BlockSpec((2, D_MODEL, 128), lambda i: (0, 0, 0))],
        out_specs=[pl.BlockSpec((TILE, D_MODEL), lambda i: (i, 0)),
                   pl.BlockSpec((TILE, D_MODEL), lambda i: (i, 0)),
                   pl.BlockSpec((128, TILE), lambda i: (0, i))],
        out_shape=[jax.ShapeDtypeStruct((N_TOK, D_MODEL), F32),
                   jax.ShapeDtypeStruct((N_TOK, D_MODEL), BF16),
                   jax.ShapeDtypeStruct((128, N_TOK), F32)],
        compiler_params=_cparams(("parallel",)),
        name="out_projection",
    )(x, rest, ys, attn_ctx, attn_lat, ret_ctx, ret_lat, mod_l, d_row, wglu_bf, wout_bf, nw2, wr_split)


def _lane_cumsum(x01):
    rows, n = x01.shape
    r = lax.broadcasted_iota(jnp.int32, (256, 256), 0)
    c = lax.broadcasted_iota(jnp.int32, (256, 256), 1)
    tri = jnp.where(r <= c, 1.0, 0.0).astype(BF16)
    off = jnp.zeros((rows, 1), F32)
    parts = []
    for j in range(n // 256):
        cs = jnp.dot(x01[:, 256 * j:256 * (j + 1)].astype(BF16), tri, preferred_element_type=F32) + off
        parts.append(cs)
        off = cs[:, 255:256]
    return jnp.concatenate(parts, axis=1)


def _count(m):
    return jnp.sum(jnp.where(m, 1.0, 0.0), axis=1, keepdims=True)


def _route_kernel(aff_ref, slot_ref, gate_ref, *, cap, seg):
    n_seg = aff_ref.shape[1] // seg
    segs = [slice(i * seg, (i + 1) * seg) for i in range(n_seg)]
    tiny = float(jnp.finfo(jnp.float32).tiny)

    def step(_, bounds):
        out = []
        for i in range(n_seg):
            lo, hi = bounds[2 * i], bounds[2 * i + 1]
            mid = jnp.where(lo > 0.0, jnp.sqrt(lo) * jnp.sqrt(hi), jnp.maximum(hi * (2.0 ** -16), tiny))
            mid = jnp.minimum(jnp.maximum(mid, lo), hi)
            ok = _count(aff_ref[:, segs[i]] >= mid) >= cap
            out += [jnp.where(ok, mid, lo), jnp.where(ok, hi, mid)]
        return tuple(out)

    init = (jnp.zeros((N_EXPERTS, 1), F32), jnp.full((N_EXPERTS, 1), 2.0, F32)) * n_seg
    bounds = lax.fori_loop(0, ROUTE_ITERS, step, init)

    for i in range(n_seg):
        a = aff_ref[:, segs[i]]
        lo, hi = bounds[2 * i], bounds[2 * i + 1]
        above = a >= hi
        band = (a >= lo) & (a < hi)
        sel = above | (band & (_lane_cumsum(jnp.where(band, 1.0, 0.0)) <= cap - _count(above)))
        slot = _lane_cumsum(jnp.where(sel, 1.0, 0.0)) - 1.0
        slot_ref[:, segs[i]] = jnp.where(sel, slot, -1.0).astype(jnp.int32)
        gate_ref[:, segs[i]] = jnp.where(sel, a, 0.0)


def _route(aff_t, col_block0, n_blocks, width, seg, cap):
    return pl.pallas_call(
        functools.partial(_route_kernel, cap=cap, seg=seg),
        grid=(n_blocks,),
        in_specs=[pl.BlockSpec((N_EXPERTS, width), lambda i: (0, col_block0 + i))],
        out_specs=[pl.BlockSpec((N_EXPERTS, width), lambda i: (0, i))] * 2,
        out_shape=[jax.ShapeDtypeStruct((N_EXPERTS, n_blocks * width), jnp.int32),
                   jax.ShapeDtypeStruct((N_EXPERTS, n_blocks * width), F32)],
        compiler_params=_cparams(("parallel",)),
        name="route",
    )(aff_t)


def _gather_lat_kernel(slot_ref, gate_ref, h_ref, o_ref, g_ref):
    e = pl.program_id(1)
    slot = slot_ref[pl.ds(e, 1), :]
    hit = lax.broadcasted_iota(jnp.int32, (CAP_LAT, slot.shape[1]), 0) == slot
    onehot = jnp.where(hit, 1.0, 0.0).astype(BF16)
    o_ref[...] = jnp.dot(onehot, h_ref[...], preferred_element_type=F32).astype(o_ref.dtype)
    g_ref[...] = jnp.sum(jnp.where(hit, gate_ref[pl.ds(e, 1), :], 0.0), axis=1, keepdims=True)


def _gather_lat(slot, gate, h2):
    rows = pl.BlockSpec((N_EXPERTS, DEC_SEQ), lambda r, e: (0, r))
    return pl.pallas_call(
        _gather_lat_kernel,
        grid=(DEC_BATCH, N_EXPERTS),
        in_specs=[rows, rows,
                  pl.BlockSpec((DEC_SEQ, D_MODEL), lambda r, e: (N_CTX // DEC_SEQ + r, 0))],
        out_specs=[pl.BlockSpec((None, CAP_LAT, D_MODEL), lambda r, e: (e, r, 0)),
                   pl.BlockSpec((None, CAP_LAT, 1), lambda r, e: (e, r, 0))],
        out_shape=[jax.ShapeDtypeStruct((N_EXPERTS, DEC_BATCH * CAP_LAT, D_MODEL), BF16),
                   jax.ShapeDtypeStruct((N_EXPERTS, DEC_BATCH * CAP_LAT, 1), F32)],
        compiler_params=_cparams(("parallel", "arbitrary")),
        name="gather_lat",
    )(slot, gate, h2)


def _ctx_onehot(slot):
    rows = lax.broadcasted_iota(jnp.int32, (CAP_CTX, SEQ), 0)
    hits = [rows == slot[e:e + 1, :] for e in range(N_EXPERTS)]
    onehot = jnp.concatenate([jnp.where(h, 1.0, 0.0) for h in hits], axis=0).astype(BF16)
    return onehot, hits


def _gather_ctx_kernel(slot_ref, gate_ref, h_ref, o_ref, g_ref):
    onehot, hits = _ctx_onehot(slot_ref[...])
    xs = jnp.dot(onehot, h_ref[...], preferred_element_type=F32).astype(o_ref.dtype)
    gate = gate_ref[...]
    for e in range(N_EXPERTS):
        o_ref[e] = xs[e * CAP_CTX:(e + 1) * CAP_CTX]
        g_ref[e] = jnp.sum(jnp.where(hits[e], gate[e:e + 1, :], 0.0), axis=1, keepdims=True)


def _gather_ctx(slot, gate, h2):
    rows = pl.BlockSpec((N_EXPERTS, SEQ), lambda r: (0, r))
    return pl.pallas_call(
        _gather_ctx_kernel,
        grid=(BATCH,),
        in_specs=[rows, rows, pl.BlockSpec((SEQ, D_MODEL), lambda r: (r, 0))],
        out_specs=[pl.BlockSpec((N_EXPERTS, CAP_CTX, D_MODEL), lambda r: (0, r, 0)),
                   pl.BlockSpec((N_EXPERTS, CAP_CTX, 1), lambda r: (0, r, 0))],
        out_shape=[jax.ShapeDtypeStruct((N_EXPERTS, BATCH * CAP_CTX, D_MODEL), BF16),
                   jax.ShapeDtypeStruct((N_EXPERTS, BATCH * CAP_CTX, 1), F32)],
        compiler_params=_cparams(("parallel",)),
        name="gather_ctx",
    )(slot, gate, h2)


def _ffn_kernel(xc_ref, xl_ref, gc_ref, gl_ref, wg_ref, wu_ref, wd_ref, yc_ref, yl_ref, accc_ref, accl_ref):
    f = pl.program_id(1)
    wg = wg_ref[...].astype(BF16)
    wu = wu_ref[...].astype(BF16)
    wd = wd_ref[...].astype(BF16)

    def part(x_ref, gate_ref, acc_ref, y_ref):
        x = x_ref[...]
        a = jnp.dot(x, wg, preferred_element_type=F32)
        up = jnp.dot(x, wu, preferred_element_type=F32)
        mid = (a * jax.nn.sigmoid(a) * up).astype(BF16)
        y = jnp.dot(mid, wd, preferred_element_type=F32)

        @pl.when(f == 0)
        def _():
            acc_ref[...] = y

        @pl.when(f > 0)
        def _():
            acc_ref[...] += y

        @pl.when(f == pl.num_programs(1) - 1)
        def _():
            y_ref[...] = (acc_ref[...] * gate_ref[...]).astype(y_ref.dtype)

    part(xc_ref, gc_ref, accc_ref, yc_ref)
    part(xl_ref, gl_ref, accl_ref, yl_ref)


def _expert_ffn(xs_ctx, xs_lat, gs_ctx, gs_lat, w_gate, w_up, w_down, layer):
    tf = 512
    nc, nl = xs_ctx.shape[1], xs_lat.shape[1]
    return pl.pallas_call(
        _ffn_kernel,
        grid=(N_EXPERTS, EXPERT_FF // tf),
        in_specs=[pl.BlockSpec((None, nc, D_MODEL), lambda e, f: (e, 0, 0)),
                  pl.BlockSpec((None, nl, D_MODEL), lambda e, f: (e, 0, 0)),
                  pl.BlockSpec((None, nc, 1), lambda e, f: (e, 0, 0)),
                  pl.BlockSpec((None, nl, 1), lambda e, f: (e, 0, 0)),
                  pl.BlockSpec((None, None, D_MODEL, tf), lambda e, f: (layer, e, 0, f)),
                  pl.BlockSpec((None, None, D_MODEL, tf), lambda e, f: (layer, e, 0, f)),
                  pl.BlockSpec((None, None, tf, D_MODEL), lambda e, f: (layer, e, f, 0))],
        out_specs=[pl.BlockSpec((None, nc, D_MODEL), lambda e, f: (e, 0, 0)),
                   pl.BlockSpec((None, nl, D_MODEL), lambda e, f: (e, 0, 0))],
        out_shape=[jax.ShapeDtypeStruct(xs_ctx.shape, BF16), jax.ShapeDtypeStruct(xs_lat.shape, BF16)],
        scratch_shapes=[pltpu.VMEM((nc, D_MODEL), F32), pltpu.VMEM((nl, D_MODEL), F32)],
        compiler_params=_cparams(("parallel", "arbitrary")),
        name="expert_ffn",
    )(xs_ctx, xs_lat, gs_ctx, gs_lat, w_gate, w_up, w_down)


def _scatter_lat_kernel(slot_ref, y_ref, x_ref, mod_ref, o_ref):
    slot_t = slot_ref[...].astype(F32).T
    lane = lax.broadcasted_iota(jnp.int32, (slot_t.shape[0], CAP_LAT), 1).astype(F32)
    onehot = jnp.concatenate([jnp.where(lane == slot_t[:, e:e + 1], 1.0, 0.0).astype(BF16)
                              for e in range(N_EXPERTS)], axis=1)
    y = y_ref[...].reshape(N_EXPERTS * CAP_LAT, D_MODEL)
    ffn = jnp.dot(onehot, y, preferred_element_type=F32)
    o_ref[...] = x_ref[...] + mod_ref[:, 5 * D_MODEL:6 * D_MODEL] * ffn


def _scatter_lat(slot, ys, x, mod_l):
    tt = 256
    nt = DEC_SEQ // tt
    blk0 = N_CTX // tt
    tok = pl.BlockSpec((tt, D_MODEL), lambda r, t: (blk0 + r * nt + t, 0))
    return pl.pallas_call(
        _scatter_lat_kernel,
        grid=(DEC_BATCH, nt),
        in_specs=[pl.BlockSpec((N_EXPERTS, tt), lambda r, t: (0, r * nt + t)),
                  pl.BlockSpec((N_EXPERTS, CAP_LAT, D_MODEL), lambda r, t: (0, r, 0)),
                  tok,
                  pl.BlockSpec((None, 1, 6 * D_MODEL), lambda r, t: (1 + r, 0, 0))],
        out_specs=tok,
        out_shape=jax.ShapeDtypeStruct(x.shape, F32),
        input_output_aliases={2: 0},
        compiler_params=_cparams(("parallel", "parallel")),
        name="scatter_lat",
    )(slot, ys, x, mod_l)


def _scatter_ctx_kernel(slot_ref, y_ref, x_ref, mod_ref, o_ref):
    onehot, _ = _ctx_onehot(slot_ref[...])
    y = jnp.concatenate([y_ref[e] for e in range(N_EXPERTS)], axis=0)
    ffn = lax.dot_general(onehot, y, TN, preferred_element_type=F32)
    o_ref[...] = x_ref[...] + mod_ref[:, 5 * D_MODEL:6 * D_MODEL] * ffn


def _scatter_ctx(slot, ys, x, mod_l):
    tok = pl.BlockSpec((SEQ, D_MODEL), lambda r: (r, 0))
    return pl.pallas_call(
        _scatter_ctx_kernel,
        grid=(BATCH,),
        in_specs=[pl.BlockSpec((N_EXPERTS, SEQ), lambda r: (0, r)),
                  pl.BlockSpec((N_EXPERTS, CAP_CTX, D_MODEL), lambda r: (0, r, 0)),
                  tok,
                  pl.BlockSpec((None, 1, 6 * D_MODEL), lambda r: (0, 0, 0))],
        out_specs=tok,
        out_shape=jax.ShapeDtypeStruct(x.shape, F32),
        input_output_aliases={2: 0},
        compiler_params=_cparams(("parallel",)),
        name="scatter_ctx",
    )(slot, ys, x, mod_l)


def _final_norm_kernel(x_ref, w_ref, o_ref):
    x = x_ref[...]
    o_ref[...] = x * lax.rsqrt(jnp.mean(x * x, axis=-1, keepdims=True) + EPS) * w_ref[...]


def _final_norm(x, w):
    return pl.pallas_call(
        _final_norm_kernel,
        grid=(N_TILES,),
        in_specs=[pl.BlockSpec((TILE, D_MODEL), lambda i: (i, 0)), pl.BlockSpec((1, D_MODEL), lambda i: (0, 0))],
        out_specs=pl.BlockSpec((TILE, D_MODEL), lambda i: (i, 0)),
        out_shape=jax.ShapeDtypeStruct(x.shape, F32),
        compiler_params=_cparams(("parallel",)),
        name="final_norm",
    )(x, w)


def _rope_tables():
    rows = DEC_SEQ // GRID_W
    row = jnp.repeat(jnp.arange(rows, dtype=F32), GRID_W)
    col = jnp.tile(jnp.arange(GRID_W, dtype=F32), rows)
    n_freq = HEAD_DIM // 4
    inv_freq = ROPE_THETA ** (-jnp.arange(n_freq, dtype=F32) / n_freq)
    ang = jnp.concatenate([row[:, None] * inv_freq, col[:, None] * inv_freq], axis=-1)
    cos, sin = jnp.cos(ang), jnp.sin(ang)
    cos_t = jnp.tile(jnp.concatenate([cos, cos], -1), (1, 128 // HEAD_DIM))
    sin_t = jnp.tile(jnp.concatenate([-sin, sin], -1), (1, 128 // HEAD_DIM))
    cos_t = jnp.concatenate([jnp.ones((TILE, 128), F32), cos_t], axis=0)
    sin_t = jnp.concatenate([jnp.zeros((TILE, 128), F32), sin_t], axis=0)
    return cos_t, sin_t


def _s5_initial_rows(state_ssm):
    st = state_ssm.astype(F32)
    re, im = st[..., 0], st[..., 1]
    both = jnp.stack([jnp.concatenate([re, im], -1), jnp.concatenate([im, re], -1)], axis=3)
    return both.transpose(1, 0, 2, 3, 4, 5).reshape(DEPTH, 4 * DEC_BATCH, N_SSM_GROUPS * 128)


def kernel(x_prompt, x_sample, cache_k, cache_v, state_ssm, state_ret, c, c_ctx, w_mod, b_mod, norm1_w, norm2_w, w_in, w_out, qn_w, kn_w, ssm_lambda_re, ssm_lambda_im, ssm_b_re, ssm_b_im, ssm_c_re, ssm_c_im, ssm_log_dt, ssm_d, ssm_w_glu, ret_decay_logit, ret_norm_w, w_router, w_gate, w_up, w_down, final_norm_w):
    x = jnp.concatenate([x_prompt.reshape(N_CTX, D_MODEL), x_sample.reshape(N_LAT, D_MODEL)], axis=0)
    cond_t = jnp.zeros((D_MODEL, 8), F32).at[:, 0].set(c_ctx).at[:, 1:1 + DEC_BATCH].set(c.T)
    mod = _modulation(cond_t, w_mod, b_mod).reshape(DEPTH, 8, 1, 6 * D_MODEL)
    cos_t, sin_t = _rope_tables()
    zero_ret = jnp.zeros((BATCH, 2, RET_HEADS, HEAD_DIM, HEAD_DIM), F32)
    ctx_blocks = N_CTX // DEC_SEQ

    w_in_bf, w_out_bf, w_glu_bf = w_in.astype(BF16), w_out.astype(BF16), ssm_w_glu.astype(BF16)
    s5_toe, s5_inj, s5_ro, s5_a = jax.vmap(_s5_matrices)(ssm_lambda_re, ssm_lambda_im, ssm_b_re, ssm_b_im,
                                                         ssm_c_re, ssm_c_im, ssm_log_dt)
    s5_h0 = _s5_initial_rows(state_ssm)
    ret_dec, ret_mask, ret_cdec = jax.vmap(_retention_tables)(ret_decay_logit)
    wr = jnp.pad(w_router.astype(F32), ((0, 0), (0, 0), (0, 128 - N_EXPERTS)))
    wr_hi = wr.astype(BF16)
    wr_split = jnp.stack([wr_hi, (wr - wr_hi.astype(F32)).astype(BF16)], axis=1)
    qn_t, kn_t = jnp.tile(qn_w, (1, 2)), jnp.tile(kn_w, (1, 2))
    cache_kv = jnp.concatenate([cache_k.reshape(DEC_BATCH, DEPTH, PAST_LEN, KV_WIDTH),
                                cache_v.reshape(DEC_BATCH, DEPTH, PAST_LEN, KV_WIDTH)], axis=-1).astype(BF16)

    ks, vs, ss, rs = [], [], [], []
    for l in range(DEPTH):
        mod_l = mod[l]
        q, kv, ub, rest = _in_projection(x, mod_l, norm1_w[l].reshape(1, -1), w_in_bf[l],
                                     qn_t[l].reshape(1, -1), kn_t[l].reshape(1, -1), cos_t, sin_t)
        ks.append(rest[:N_CTX, C_K:C_K + KV_WIDTH].reshape(BATCH, SEQ, N_KV_HEADS, HEAD_DIM))
        vs.append(rest[:N_CTX, C_V:C_V + KV_WIDTH].reshape(BATCH, SEQ, N_KV_HEADS, HEAD_DIM))

        kv_ctx = kv[:N_CTX].reshape(BATCH, SEQ, 2 * KV_WIDTH)
        kv_lat = jnp.concatenate([kv[N_CTX:].reshape(DEC_BATCH, DEC_SEQ, 2 * KV_WIDTH), cache_kv[:, l]], axis=1)
        attn_ctx = _attention(q, kv_ctx, 0, BATCH, SEQ, SEQ)
        attn_lat = _attention(q, kv_lat, N_CTX // 256, DEC_BATCH, DEC_SEQ, 256)

        y_loc, states = _s5_local(ub.reshape(S5_ROWS, S5_CHUNK * SSM_WIDTH), s5_toe[l], s5_inj[l])
        hf, hb, fin_f, fin_b = _s5_scan(states, s5_a[l], s5_h0[l])
        ys = _s5_out(y_loc, hf, hb, s5_ro[l]).reshape(N_TOK, SSM_WIDTH)
        fin = jnp.stack([fin_f, fin_b], axis=1).reshape(BATCH, 2, N_SSM_GROUPS, 2, SSM_STATE)
        ss.append(fin.transpose(0, 1, 2, 4, 3))

        nw_ret = ret_norm_w[l].reshape(1, -1)
        ret_ctx, fin_ret = _retention(rest, ret_dec[l], ret_mask[l], ret_cdec[l], zero_ret, nw_ret, 0, BATCH, SEQ)
        ret_lat, _ = _retention(rest, ret_dec[l], ret_mask[l], ret_cdec[l], state_ret[:, l].astype(F32), nw_ret,
                                ctx_blocks, DEC_BATCH, DEC_SEQ)
        rs.append(fin_ret)

        x1, h2, aff_t = _out_projection(x, rest, ys, attn_ctx, attn_lat, ret_ctx, ret_lat, mod_l,
                                        ssm_d[l].reshape(1, -1), w_glu_bf[l], w_out_bf[l],
                                        norm2_w[l].reshape(1, -1), wr_split[l])

        slot_ctx, gate_ctx = _route(aff_t, 0, 1, N_CTX, SEQ, CAP_CTX)
        slot_lat, gate_lat = _route(aff_t, N_CTX // DEC_SEQ, DEC_BATCH, DEC_SEQ, DEC_SEQ, CAP_LAT)
        xs_ctx, gs_ctx = _gather_ctx(slot_ctx, gate_ctx, h2)
        xs_lat, gs_lat = _gather_lat(slot_lat, gate_lat, h2)
        y_ctx, y_lat = _expert_ffn(xs_ctx, xs_lat, gs_ctx, gs_lat, w_gate, w_up, w_down, l)
        x = _scatter_ctx(slot_ctx, y_ctx, x1, mod_l)
        x = _scatter_lat(slot_lat, y_lat, x, mod_l)

    y = _final_norm(x, final_norm_w.reshape(1, -1))
    y_prompt = y[:N_CTX].reshape(BATCH, SEQ, D_MODEL)
    y_sample = y[N_CTX:].reshape(DEC_BATCH, DEC_SEQ, D_MODEL)
    return (y_prompt, y_sample, jnp.stack(ks, axis=1), jnp.stack(vs, axis=1),
            jnp.stack(ss, axis=1), jnp.stack(rs, axis=1))
```

```python
import functools

import jax
import jax.numpy as jnp
from jax import lax
from jax.experimental import pallas as pl
from jax.experimental.pallas import tpu as pltpu

F32 = jnp.float32
BF16 = jnp.bfloat16

D_MODEL = 1024
BATCH = 16
SEQ = 256
DEPTH = 4
DEC_BATCH = 2
DEC_SEQ = 4096
PAST_LEN = 256
GRID_W = 64
HEAD_DIM = 64
SSM_WIDTH = 256
SSM_GROUP = 16
N_SSM_GROUPS = 16
SSM_STATE = 64
ATT_WIDTH = 512
N_HEADS = 8
N_KV_HEADS = 2
KV_WIDTH = 128
RET_WIDTH = 256
RET_HEADS = 4
IN_WIDTH = 2048
RET_CHUNK = 128
N_EXPERTS = 16
EXPERT_FF = 1024
ROPE_THETA = 10000.0
EPS = 1e-6

N_CTX = BATCH * SEQ
N_LAT = DEC_BATCH * DEC_SEQ
N_TOK = N_CTX + N_LAT
TILE = 256
N_TILES = N_TOK // TILE
CTX_TILES = N_CTX // TILE
LAT_TILES_PER_REQ = DEC_SEQ // TILE
CAP_CTX = 2 * SEQ // N_EXPERTS
CAP_LAT = 2 * DEC_SEQ // N_EXPERTS
S5_CHUNK = 16
CTX_CHUNKS = SEQ // S5_CHUNK
LAT_CHUNKS = DEC_SEQ // S5_CHUNK
S5_ROWS_CTX = BATCH * CTX_CHUNKS
S5_ROWS = S5_ROWS_CTX + DEC_BATCH * LAT_CHUNKS
C_U, C_K, C_V, C_RQ, C_RK, C_RV, C_RG = 0, 256, 384, 512, 768, 1024, 1280
REST_WIDTH = 1536
ROUTE_ITERS = 48
OFFS_STEP = 128
GATHER_BLOCK = 256
SCATTER_WINDOW = 256
VMEM_LIMIT = 56 * 1024 * 1024

TN = (((0,), (0,)), ((), ()))
NT = (((1,), (1,)), ((), ()))


def _cparams(sem):
    return pltpu.CompilerParams(dimension_semantics=sem, vmem_limit_bytes=VMEM_LIMIT)


def _tile_mod_row(i):
    return jnp.where(i < CTX_TILES, 0, 1 + (i - CTX_TILES) // LAT_TILES_PER_REQ)


def _split_dot(v, m):
    hi = v.astype(BF16)
    lo = (v - hi.astype(F32)).astype(BF16)
    return (jnp.dot(hi, m, preferred_element_type=F32)
            + jnp.dot(lo, m, preferred_element_type=F32))


def _group_avg_matrix():
    r = lax.broadcasted_iota(jnp.int32, (128, 128), 0) // HEAD_DIM
    c = lax.broadcasted_iota(jnp.int32, (128, 128), 1) // HEAD_DIM
    return jnp.where(r == c, 1.0 / HEAD_DIM, 0.0).astype(BF16)


def _mod_kernel(ct_ref, w_ref, b_ref, o_ref):
    c = ct_ref[...]
    s = c * jax.nn.sigmoid(c)
    w = w_ref[...]
    rows = [jnp.sum(w * s[:, r:r + 1], axis=0, keepdims=True) for r in range(3)]
    rows.append(jnp.zeros((5, w.shape[1]), F32))
    o_ref[...] = jnp.concatenate(rows, axis=0) + b_ref[...]


def _modulation(cond_t, w_mod, b_mod):
    tn = 512
    n = 6 * D_MODEL
    return pl.pallas_call(
        _mod_kernel,
        grid=(DEPTH, n // tn),
        in_specs=[pl.BlockSpec((D_MODEL, 8), lambda l, j: (0, 0)),
                  pl.BlockSpec((None, D_MODEL, tn), lambda l, j: (l, 0, j)),
                  pl.BlockSpec((None, 1, tn), lambda l, j: (l, 0, j))],
        out_specs=pl.BlockSpec((None, 8, tn), lambda l, j: (l, 0, j)),
        out_shape=jax.ShapeDtypeStruct((DEPTH, 8, n), F32),
        compiler_params=_cparams(("arbitrary", "arbitrary")),
        name="modulation",
    )(cond_t, w_mod, b_mod.reshape(DEPTH, 1, n))


def _inproj_kernel(x_ref, mod_ref, nw_ref, w_ref, qn_ref, kn_ref, cos_ref, sin_ref, q_ref, kv_ref, ub_ref, rest_ref):
    x = x_ref[...]
    shift = mod_ref[:, 0:D_MODEL]
    scale = mod_ref[:, D_MODEL:2 * D_MODEL]
    y = x * lax.rsqrt(jnp.mean(x * x, axis=-1, keepdims=True) + EPS) * nw_ref[...]
    h = y * (1.0 + scale) + shift
    proj = jnp.dot(h.astype(BF16), w_ref[...], preferred_element_type=F32)

    avg = _group_avg_matrix()
    cos = cos_ref[...]
    sin = sin_ref[...]
    first_half = (lax.broadcasted_iota(jnp.int32, (TILE, 128), 1) % HEAD_DIM) < (HEAD_DIM // 2)

    def head_norm(z, wrow):
        return z * lax.rsqrt(_split_dot(z * z, avg) + EPS) * wrow

    def rope(z):
        partner = jnp.where(first_half, pltpu.roll(z, 128 - HEAD_DIM // 2, 1), pltpu.roll(z, HEAD_DIM // 2, 1))
        return z * cos + partner * sin

    def col(off, j):
        return proj[:, off + 128 * j: off + 128 * (j + 1)]

    qn = qn_ref[...]
    for j in range(ATT_WIDTH // 128):
        z = rope(head_norm(col(SSM_WIDTH, j), qn)) * (HEAD_DIM ** -0.5)
        q_ref[:, 128 * j:128 * (j + 1)] = z.astype(q_ref.dtype)
    p_k = SSM_WIDTH + ATT_WIDTH
    rest_ref[:, C_U:C_U + SSM_WIDTH] = proj[:, 0:SSM_WIDTH]
    ub_ref[...] = proj[:, 0:SSM_WIDTH].astype(ub_ref.dtype)
    k = rope(head_norm(col(p_k, 0), kn_ref[...]))
    v = col(p_k + KV_WIDTH, 0)
    rest_ref[:, C_K:C_K + KV_WIDTH] = k
    rest_ref[:, C_V:C_V + KV_WIDTH] = v
    kv_ref[:, 0:KV_WIDTH] = k.astype(kv_ref.dtype)
    kv_ref[:, KV_WIDTH:2 * KV_WIDTH] = v.astype(kv_ref.dtype)
    p_r = p_k + 2 * KV_WIDTH
    for j in range(RET_WIDTH // 128):
        rest_ref[:, C_RQ + 128 * j:C_RQ + 128 * (j + 1)] = rope(col(p_r, j))
        rest_ref[:, C_RK + 128 * j:C_RK + 128 * (j + 1)] = rope(col(p_r + RET_WIDTH, j)) * (HEAD_DIM ** -0.5)
    rest_ref[:, C_RV:C_RV + 2 * RET_WIDTH] = proj[:, p_r + 2 * RET_WIDTH:p_r + 4 * RET_WIDTH]


def _in_projection(x, mod_l, nw, w_in_bf, qn, kn, cos_t, sin_t):
    def rope_blk(i):
        return (jnp.where(i < CTX_TILES, 0, 1 + (i - CTX_TILES) % LAT_TILES_PER_REQ), 0)
    return pl.pallas_call(
        _inproj_kernel,
        grid=(N_TILES,),
        in_specs=[pl.BlockSpec((TILE, D_MODEL), lambda i: (i, 0)),
                  pl.BlockSpec((None, 1, 6 * D_MODEL), lambda i: (_tile_mod_row(i), 0, 0)),
                  pl.BlockSpec((1, D_MODEL), lambda i: (0, 0)),
                  pl.BlockSpec((D_MODEL, IN_WIDTH), lambda i: (0, 0)),
                  pl.BlockSpec((1, 128), lambda i: (0, 0)),
                  pl.BlockSpec((1, 128), lambda i: (0, 0)),
                  pl.BlockSpec((TILE, 128), rope_blk),
                  pl.BlockSpec((TILE, 128), rope_blk)],
        out_specs=[pl.BlockSpec((TILE, ATT_WIDTH), lambda i: (i, 0)),
                   pl.BlockSpec((TILE, 2 * KV_WIDTH), lambda i: (i, 0)),
                   pl.BlockSpec((TILE, SSM_WIDTH), lambda i: (i, 0)),
                   pl.BlockSpec((TILE, REST_WIDTH), lambda i: (i, 0))],
        out_shape=[jax.ShapeDtypeStruct((N_TOK, ATT_WIDTH), BF16),
                   jax.ShapeDtypeStruct((N_TOK, 2 * KV_WIDTH), BF16),
                   jax.ShapeDtypeStruct((N_TOK, SSM_WIDTH), BF16),
                   jax.ShapeDtypeStruct((N_TOK, REST_WIDTH), F32)],
        compiler_params=_cparams(("parallel",)),
        name="in_projection",
    )(x, mod_l, nw, w_in_bf, qn, kn, cos_t, sin_t)


def _attn_kernel(q_ref, k_ref, v_ref, o_ref):
    group = N_HEADS // N_KV_HEADS
    for kv in range(N_KV_HEADS):
        k = k_ref[:, kv * HEAD_DIM:(kv + 1) * HEAD_DIM]
        v = v_ref[:, kv * HEAD_DIM:(kv + 1) * HEAD_DIM]
        for g in range(group):
            cols = slice((kv * group + g) * HEAD_DIM, (kv * group + g + 1) * HEAD_DIM)
            s = lax.dot_general(q_ref[:, cols], k, NT, preferred_element_type=F32)
            m = jnp.max(s, axis=-1, keepdims=True)
            p = jnp.exp(s - m)
            l = jnp.sum(p, axis=-1, keepdims=True)
            o = jnp.dot(p.astype(BF16), v, preferred_element_type=F32) / l
            o_ref[:, cols] = o.astype(o_ref.dtype)


def _attention(q, kv, row_block0, n_req, lq, tq):
    lk = kv.shape[1]
    nq = lq // tq
    return pl.pallas_call(
        _attn_kernel,
        grid=(n_req, nq),
        in_specs=[pl.BlockSpec((tq, ATT_WIDTH), lambda r, i: (row_block0 + r * nq + i, 0)),
                  pl.BlockSpec((None, lk, KV_WIDTH), lambda r, i: (r, 0, 0)),
                  pl.BlockSpec((None, lk, KV_WIDTH), lambda r, i: (r, 0, 1))],
        out_specs=pl.BlockSpec((tq, ATT_WIDTH), lambda r, i: (r * nq + i, 0)),
        out_shape=jax.ShapeDtypeStruct((n_req * lq, ATT_WIDTH), BF16),
        compiler_params=_cparams(("parallel", "parallel")),
        name="attention",
    )(q, kv, kv)


def _s5_local_kernel(u_ref, t_ref, b_ref, y_ref, s_ref):
    u = u_ref[...]
    y_ref[...] = jnp.dot(u, t_ref[...], preferred_element_type=F32)
    s_ref[...] = jnp.dot(u, b_ref[...], preferred_element_type=F32)


def _s5_local(u2, toeplitz, inject):
    tn = 256
    n = S5_CHUNK * SSM_WIDTH
    wspec = pl.BlockSpec((n, tn), lambda i: (0, i))
    ospec = pl.BlockSpec((S5_ROWS, tn), lambda i: (0, i))
    return pl.pallas_call(
        _s5_local_kernel,
        grid=(n // tn,),
        in_specs=[pl.BlockSpec((S5_ROWS, n), lambda i: (0, 0)), wspec, wspec],
        out_specs=[ospec, ospec],
        out_shape=[jax.ShapeDtypeStruct((S5_ROWS, n), F32)] * 2,
        compiler_params=_cparams(("parallel",)),
        name="s5_local",
    )(u2, toeplitz, inject)


def _s5_scan_kernel(sf_ref, sb_ref, a_ref, h0_ref, hf_ref, hb_ref, ff_ref, fb_ref, sfs_ref, sbs_ref):
    a1f, a2f, a1b, a2b = a_ref[0:1, :], a_ref[1:2, :], a_ref[2:3, :], a_ref[3:4, :]
    w = sf_ref.shape[1]
    for c in range(w // 128):
        cols = slice(128 * c, 128 * (c + 1))
        sfs_ref[:, cols] = pltpu.roll(sf_ref[:, cols], SSM_STATE, 1)
        sbs_ref[:, cols] = pltpu.roll(sb_ref[:, cols], SSM_STATE, 1)

    zero = jnp.zeros((1, w), F32)
    for r in range(BATCH):
        hf, hfs, hb, hbs = zero, zero, zero, zero
        for j in range(CTX_CHUNKS):
            row = r * CTX_CHUNKS + j
            hf_ref[row:row + 1, :] = hf
            hf, hfs = (a1f * hf + a2f * hfs + sf_ref[row:row + 1, :], a1f * hfs - a2f * hf + sfs_ref[row:row + 1, :])
            row = r * CTX_CHUNKS + CTX_CHUNKS - 1 - j
            hb_ref[row:row + 1, :] = hb
            hb, hbs = (a1b * hb + a2b * hbs + sb_ref[row:row + 1, :], a1b * hbs - a2b * hb + sbs_ref[row:row + 1, :])
        ff_ref[r:r + 1, :] = hf
        fb_ref[r:r + 1, :] = hb

    def body(j, carry):
        out = []
        for r in range(DEC_BATCH):
            hf, hfs, hb, hbs = carry[4 * r:4 * r + 4]
            row = pl.ds(S5_ROWS_CTX + r * LAT_CHUNKS + j, 1)
            hf_ref[row, :] = hf
            nf = a1f * hf + a2f * hfs + sf_ref[row, :]
            nfs = a1f * hfs - a2f * hf + sfs_ref[row, :]
            row = pl.ds(S5_ROWS_CTX + r * LAT_CHUNKS + LAT_CHUNKS - 1 - j, 1)
            hb_ref[row, :] = hb
            nb = a1b * hb + a2b * hbs + sb_ref[row, :]
            nbs = a1b * hbs - a2b * hb + sbs_ref[row, :]
            out += [nf, nfs, nb, nbs]
        return tuple(out)

    init = tuple(h0_ref[i:i + 1, :] for i in range(4 * DEC_BATCH))
    lax.fori_loop(0, LAT_CHUNKS, body, init)


def _s5_scan(states, a_rows, h0_rows):
    w = 512
    full = N_SSM_GROUPS * 128
    nb = full // w
    sspec = pl.BlockSpec((S5_ROWS, w), lambda i: (0, i))
    fspec = pl.BlockSpec((BATCH, w), lambda i: (0, i))
    return pl.pallas_call(
        _s5_scan_kernel,
        grid=(nb,),
        in_specs=[sspec,
                  pl.BlockSpec((S5_ROWS, w), lambda i: (0, nb + i)),
                  pl.BlockSpec((8, w), lambda i: (0, i)),
                  pl.BlockSpec((8, w), lambda i: (0, i))],
        out_specs=[sspec, sspec, fspec, fspec],
        out_shape=[jax.ShapeDtypeStruct((S5_ROWS, full), F32)] * 2 + [jax.ShapeDtypeStruct((BATCH, full), F32)] * 2,
        scratch_shapes=[pltpu.VMEM((S5_ROWS, w), F32), pltpu.VMEM((S5_ROWS, w), F32)],
        compiler_params=_cparams(("parallel",)),
        name="s5_scan",
    )(states, states, a_rows, h0_rows)


def _s5_out_kernel(y_ref, hf_ref, hb_ref, c_ref, o_ref, h_ref):
    full = N_SSM_GROUPS * 128

    @pl.when(pl.program_id(0) == 0)
    def _():
        h_ref[:, 0:full] = hf_ref[...].astype(BF16)
        h_ref[:, full:2 * full] = hb_ref[...].astype(BF16)

    o_ref[...] = y_ref[...] + jnp.dot(h_ref[...], c_ref[...], preferred_element_type=F32)


def _s5_out(y_loc, hf, hb, readout):
    tn = 256
    n = S5_CHUNK * SSM_WIDTH
    full = N_SSM_GROUPS * 128
    hspec = pl.BlockSpec((S5_ROWS, full), lambda i: (0, 0))
    return pl.pallas_call(
        _s5_out_kernel,
        grid=(n // tn,),
        in_specs=[pl.BlockSpec((S5_ROWS, tn), lambda i: (0, i)), hspec, hspec,
                  pl.BlockSpec((2 * full, tn), lambda i: (0, i))],
        out_specs=pl.BlockSpec((S5_ROWS, tn), lambda i: (0, i)),
        out_shape=jax.ShapeDtypeStruct((S5_ROWS, n), F32),
        scratch_shapes=[pltpu.VMEM((S5_ROWS, 2 * full), BF16)],
        compiler_params=_cparams(("arbitrary",)),
        name="s5_out",
    )(y_loc, hf, hb, readout)


def _s5_matrices(lam_re, lam_im, b_re, b_im, c_re, c_im, log_dt):
    hp = lax.Precision.HIGHEST
    n = S5_CHUNK
    tau = jnp.arange(n + 1, dtype=F32)
    toes, injs, ros, a_rows = [], [], [], []
    for di in range(2):
        lr, li = lam_re[di].astype(F32), lam_im[di].astype(F32)
        dt = jnp.exp(log_dt[di].astype(F32))[:, None]
        mag = jnp.exp(lr * dt * tau[:, None, None])
        ang = li * dt * tau[:, None, None]
        e_re, e_im = mag * jnp.cos(ang), mag * jnp.sin(ang)
        nr, ni = e_re[1] - 1.0, e_im[1]
        den = lr * lr + li * li
        f_re, f_im = (nr * lr + ni * li) / den, (ni * lr - nr * li) / den
        br, bi = b_re[di].astype(F32), b_im[di].astype(F32)
        bb_re = f_re[..., None] * br - f_im[..., None] * bi
        bb_im = f_re[..., None] * bi + f_im[..., None] * br
        cr, ci = c_re[di].astype(F32), c_im[di].astype(F32)
        ce_re = cr[None] * e_re[:n, :, None, :] - ci[None] * e_im[:n, :, None, :]
        ce_im = cr[None] * e_im[:n, :, None, :] + ci[None] * e_re[:n, :, None, :]
        kern = (jnp.einsum('tgcp,gpd->tgcd', ce_re, bb_re, precision=hp)
                - jnp.einsum('tgcp,gpd->tgcd', ce_im, bb_im, precision=hp))
        kp = jnp.concatenate([jnp.zeros_like(kern), kern], axis=0)
        if di == 0:
            toe = jnp.stack([kp[n - s:2 * n - s] for s in range(n)], axis=0)
        else:
            toe = jnp.stack([kp[s + 1:s + n + 1][::-1] for s in range(n)], axis=0)
        toes.append(toe.transpose(0, 2, 4, 1, 3))
        pe_re, pe_im = (e_re[:n][::-1], e_im[:n][::-1]) if di == 0 else (e_re[:n], e_im[:n])
        inj_re = pe_re[..., None] * bb_re[None] - pe_im[..., None] * bb_im[None]
        inj_im = pe_re[..., None] * bb_im[None] + pe_im[..., None] * bb_re[None]
        injs.append(jnp.concatenate([inj_re.transpose(0, 1, 3, 2), inj_im.transpose(0, 1, 3, 2)], -1))
        qe_re, qe_im = (e_re[1:], e_im[1:]) if di == 0 else (e_re[1:][::-1], e_im[1:][::-1])
        ro_re = cr[None] * qe_re[:, :, None, :] - ci[None] * qe_im[:, :, None, :]
        ro_im = cr[None] * qe_im[:, :, None, :] + ci[None] * qe_re[:, :, None, :]
        ros.append(jnp.concatenate([ro_re.transpose(1, 3, 0, 2), -ro_im.transpose(1, 3, 0, 2)], axis=1))
        a_re, a_im = e_re[n], e_im[n]
        a_rows.append(jnp.concatenate([a_re, a_re], -1).reshape(1, -1))
        a_rows.append(jnp.concatenate([-a_im, a_im], -1).reshape(1, -1))
    size = n * SSM_WIDTH
    g = N_SSM_GROUPS
    row = jnp.arange(size)[:, None]
    col = jnp.arange(size)[None, :]
    src = jnp.arange(n * SSM_GROUP)[:, None]

    def spread(table, copy, keep):
        wide = jnp.dot(table.astype(BF16), copy.astype(BF16), preferred_element_type=F32)
        return jnp.where(keep, wide, 0.0).astype(BF16)

    copy_tc = (src // SSM_GROUP == col // SSM_WIDTH) & (src % SSM_GROUP == col % SSM_GROUP)
    copy_dp = (src // 128 == col // (g * 128)) & (src % 128 == col % 128)
    toe = (toes[0] + toes[1]).reshape(size, n * SSM_GROUP)
    toeplitz = spread(toe, copy_tc, (row // SSM_GROUP) % g == (col // SSM_GROUP) % g)
    inj = jnp.stack(injs, axis=3).reshape(size, 2 * 128)
    inject = spread(inj, copy_dp, (row // SSM_GROUP) % g == (col // 128) % g)
    ro = jnp.stack(ros, axis=0).reshape(size, n * SSM_GROUP)
    readout = spread(ro, copy_tc, (row // 128) % g == (col // SSM_GROUP) % g)
    a_rows = jnp.concatenate(a_rows + [jnp.zeros((4, N_SSM_GROUPS * 128), F32)], axis=0)
    return toeplitz, inject, readout, a_rows


def _ret_kernel(q_ref, k_ref, v_ref, g_ref, dec_ref, mask_ref, cd_ref, s0_ref, nw_ref, o_ref, fin_ref,
                kvf_ref, kvb_ref, *, n_chunks):
    hd = HEAD_DIM
    nh = 2

    def local_state(i, _):
        rows = pl.ds(pl.multiple_of(i * RET_CHUNK, RET_CHUNK), RET_CHUNK)
        k = k_ref[rows, :]
        v = v_ref[rows, :].astype(BF16)
        kf = (k * dec_ref[1]).astype(BF16)
        kb = (k * dec_ref[3]).astype(BF16)
        for h in range(nh):
            ls = slice(h * hd, (h + 1) * hd)
            kvf_ref[i, h] = lax.dot_general(kf[:, ls], v[:, ls], TN, preferred_element_type=F32)
            kvb_ref[i, h] = lax.dot_general(kb[:, ls], v[:, ls], TN, preferred_element_type=F32)
        return 0

    lax.fori_loop(0, n_chunks, local_state, 0, unroll=min(4, n_chunks))

    def scan_f(i, s):
        loc = kvf_ref[i]
        kvf_ref[i] = s
        return cd_ref[0] * s + loc

    def scan_b(i, s):
        j = n_chunks - 1 - i
        loc = kvb_ref[j]
        kvb_ref[j] = s
        return cd_ref[1] * s + loc

    fin_ref[0] = lax.fori_loop(0, n_chunks, scan_f, s0_ref[0], unroll=min(4, n_chunks))
    fin_ref[1] = lax.fori_loop(0, n_chunks, scan_b, s0_ref[1], unroll=min(4, n_chunks))

    avg = _group_avg_matrix()

    def outputs(i, _):
        rows = pl.ds(pl.multiple_of(i * RET_CHUNK, RET_CHUNK), RET_CHUNK)
        q = q_ref[rows, :]
        qb = q.astype(BF16)
        kb = k_ref[rows, :].astype(BF16)
        v = v_ref[rows, :].astype(BF16)
        qf = (q * dec_ref[0]).astype(BF16)
        qr = (q * dec_ref[2]).astype(BF16)
        outs = []
        for h in range(nh):
            ls = slice(h * hd, (h + 1) * hd)
            inner = lax.dot_general(qb[:, ls], kb[:, ls], NT, preferred_element_type=F32) * mask_ref[h]
            o = jnp.dot(inner.astype(BF16), v[:, ls], preferred_element_type=F32)
            o += jnp.dot(qf[:, ls], kvf_ref[i, h].astype(BF16), preferred_element_type=F32)
            o += jnp.dot(qr[:, ls], kvb_ref[i, h].astype(BF16), preferred_element_type=F32)
            outs.append(o)
        o = jnp.concatenate(outs, axis=1)
        d = o - _split_dot(o, avg)
        o = d * lax.rsqrt(_split_dot(d * d, avg) + EPS) * nw_ref[...]
        g = g_ref[rows, :]
        o_ref[rows, :] = (g * jax.nn.sigmoid(g) * o).astype(o_ref.dtype)
        return 0

    lax.fori_loop(0, n_chunks, outputs, 0, unroll=2)


def _retention(rest, dec, mask, cdec, s0, nw, row_block0, n_req, length):
    n_chunks = length // RET_CHUNK
    hp = RET_HEADS // 2

    def tok(cb):
        return pl.BlockSpec((length, 128), lambda r, p: (row_block0 + r, cb + p))

    state = pl.BlockSpec((None, 2, 2, HEAD_DIM, HEAD_DIM), lambda r, p: (r, 0, p, 0, 0))
    return pl.pallas_call(
        functools.partial(_ret_kernel, n_chunks=n_chunks),
        grid=(n_req, hp),
        in_specs=[tok(C_RQ // 128), tok(C_RK // 128), tok(C_RV // 128), tok(C_RG // 128),
                  pl.BlockSpec((4, RET_CHUNK, 128), lambda r, p: (0, 0, p)),
                  pl.BlockSpec((2, RET_CHUNK, RET_CHUNK), lambda r, p: (p, 0, 0)),
                  pl.BlockSpec((2, 2, HEAD_DIM, HEAD_DIM), lambda r, p: (0, p, 0, 0)),
                  state,
                  pl.BlockSpec((1, 128), lambda r, p: (0, p))],
        out_specs=[pl.BlockSpec((length, 128), lambda r, p: (r, p)), state],
        out_shape=[jax.ShapeDtypeStruct((n_req * length, RET_WIDTH), BF16),
                   jax.ShapeDtypeStruct((n_req, 2, RET_HEADS, HEAD_DIM, HEAD_DIM), F32)],
        scratch_shapes=[pltpu.VMEM((n_chunks, 2, HEAD_DIM, HEAD_DIM), F32),
                        pltpu.VMEM((n_chunks, 2, HEAD_DIM, HEAD_DIM), F32)],
        compiler_params=_cparams(("parallel", "parallel")),
        name="retention",
    )(rest, rest, rest, rest, dec, mask, cdec, s0, nw)


def _retention_tables(decay_logit):
    lg = jax.nn.log_sigmoid(decay_logit.astype(F32))
    idx = jnp.arange(RET_CHUNK, dtype=F32)
    rel = idx[:, None] - idx[None, :]
    d_f = jnp.where(rel >= 0, jnp.exp(lg[0][:, None, None] * jnp.maximum(rel, 0.0)), 0.0)
    d_b = jnp.where(rel <= 0, jnp.exp(lg[1][:, None, None] * jnp.maximum(-rel, 0.0)), 0.0)
    mask = d_f + d_b

    def lanes(t):
        return jnp.repeat(t.T, HEAD_DIM, axis=1)

    dec = jnp.stack([lanes(jnp.exp(lg[0][:, None] * (idx + 1.0))),
                     lanes(jnp.exp(lg[0][:, None] * (RET_CHUNK - 1.0 - idx))),
                     lanes(jnp.exp(lg[1][:, None] * (RET_CHUNK - idx))),
                     lanes(jnp.exp(lg[1][:, None] * idx))], axis=0)
    cdec = jnp.broadcast_to(jnp.exp(lg * RET_CHUNK)[:, :, None, None], (2, RET_HEADS, HEAD_DIM, HEAD_DIM))
    return dec, mask, cdec


def _outproj_kernel(x_ref, rest_ref, ys_ref, atc_ref, atl_ref, rtc_ref, rtl_ref, mod_ref, d_ref, wglu_ref, wout_ref,
                    nw_ref, wr_ref, x1_ref, h2_ref, aff_ref):
    y = ys_ref[...] + rest_ref[...] * d_ref[...]
    y = jax.nn.gelu(y)
    y = y * jax.nn.sigmoid(jnp.dot(y.astype(BF16), wglu_ref[...], preferred_element_type=F32))
    is_ctx = pl.program_id(0) < CTX_TILES
    attn = jnp.where(is_ctx, atc_ref[...], atl_ref[...])
    ret = jnp.where(is_ctx, rtc_ref[...], rtl_ref[...])
    mix = (jnp.dot(y.astype(BF16), wout_ref[0:SSM_WIDTH, :], preferred_element_type=F32)
           + jnp.dot(attn, wout_ref[SSM_WIDTH:SSM_WIDTH + ATT_WIDTH, :], preferred_element_type=F32)
           + jnp.dot(ret, wout_ref[SSM_WIDTH + ATT_WIDTH:, :], preferred_element_type=F32))
    gate1 = mod_ref[:, 2 * D_MODEL:3 * D_MODEL]
    shift2 = mod_ref[:, 3 * D_MODEL:4 * D_MODEL]
    scale2 = mod_ref[:, 4 * D_MODEL:5 * D_MODEL]
    x1 = x_ref[...] + gate1 * mix
    x1_ref[...] = x1
    h2 = (x1 * lax.rsqrt(jnp.mean(x1 * x1, axis=-1, keepdims=True) + EPS) * nw_ref[...]) * (1.0 + scale2) + shift2
    h2_ref[...] = h2.astype(h2_ref.dtype)
    hi = h2.astype(BF16)
    lo = (h2 - hi.astype(F32)).astype(BF16)
    logits = (jnp.dot(hi, wr_ref[0], preferred_element_type=F32)
              + jnp.dot(lo, wr_ref[0], preferred_element_type=F32)
              + jnp.dot(hi, wr_ref[1], preferred_element_type=F32))
    valid = lax.broadcasted_iota(jnp.int32, logits.shape, 1) < N_EXPERTS
    logits = jnp.where(valid, logits, -1e30)
    e = jnp.exp(logits - jnp.max(logits, axis=-1, keepdims=True))
    aff = e / jnp.sum(e, axis=-1, keepdims=True)
    aff_ref[...] = aff.T


def _out_projection(x, rest, ys, attn_ctx, attn_lat, ret_ctx, ret_lat, mod_l, d_row, wglu_bf, wout_bf, nw2, wr_split):
    ctx_blk = lambda i: (jnp.minimum(i, CTX_TILES - 1), 0)
    lat_blk = lambda i: (jnp.maximum(i - CTX_TILES, 0), 0)
    return pl.pallas_call(
        _outproj_kernel,
        grid=(N_TILES,),
        in_specs=[pl.BlockSpec((TILE, D_MODEL), lambda i: (i, 0)),
                  pl.BlockSpec((TILE, SSM_WIDTH), lambda i: (i, C_U // SSM_WIDTH)),
                  pl.BlockSpec((TILE, SSM_WIDTH), lambda i: (i, 0)),
                  pl.BlockSpec((TILE, ATT_WIDTH), ctx_blk),
                  pl.BlockSpec((TILE, ATT_WIDTH), lat_blk),
                  pl.BlockSpec((TILE, RET_WIDTH), ctx_blk),
                  pl.BlockSpec((TILE, RET_WIDTH), lat_blk),
                  pl.BlockSpec((None, 1, 6 * D_MODEL), lambda i: (_tile_mod_row(i), 0, 0)),
                  pl.BlockSpec((1, SSM_WIDTH), lambda i: (0, 0)),
                  pl.BlockSpec((SSM_WIDTH, SSM_WIDTH), lambda i: (0, 0)),
                  pl.BlockSpec((D_MODEL, D_MODEL), lambda i: (0, 0)),
                  pl.BlockSpec((1, D_MODEL), lambda i: (0, 0)),
                  pl.BlockSpec((2, D_MODEL, 128), lambda i: (0, 0, 0))],
        out_specs=[pl.BlockSpec((TILE, D_MODEL), lambda i: (i, 0)),
                   pl.BlockSpec((TILE, D_MODEL), lambda i: (i, 0)),
                   pl.BlockSpec((128, TILE), lambda i: (0, i))],
        out_shape=[jax.ShapeDtypeStruct((N_TOK, D_MODEL), F32),
                   jax.ShapeDtypeStruct((N_TOK, D_MODEL), BF16),
                   jax.ShapeDtypeStruct((128, N_TOK), F32)],
        compiler_params=_cparams(("parallel",)),
        name="out_projection",
    )(x, rest, ys, attn_ctx, attn_lat, ret_ctx, ret_lat, mod_l, d_row, wglu_bf, wout_bf, nw2, wr_split)


def _lane_cumsum(x01):
    rows, n = x01.shape
    r = lax.broadcasted_iota(jnp.int32, (256, 256), 0)
    c = lax.broadcasted_iota(jnp.int32, (256, 256), 1)
    tri = jnp.where(r <= c, 1.0, 0.0).astype(BF16)
    off = jnp.zeros((rows, 1), F32)
    parts = []
    for j in range(n // 256):
        cs = jnp.dot(x01[:, 256 * j:256 * (j + 1)].astype(BF16), tri, preferred_element_type=F32) + off
        parts.append(cs)
        off = cs[:, 255:256]
    return jnp.concatenate(parts, axis=1)


def _count(m):
    return jnp.sum(jnp.where(m, 1.0, 0.0), axis=1, keepdims=True)


def _route_kernel(aff_ref, slot_ref, gate_ref, offs_ref, *, cap, seg):
    n_seg = aff_ref.shape[1] // seg
    segs = [slice(i * seg, (i + 1) * seg) for i in range(n_seg)]
    tiny = float(jnp.finfo(jnp.float32).tiny)

    def step(_, bounds):
        out = []
        for i in range(n_seg):
            lo, hi = bounds[2 * i], bounds[2 * i + 1]
            mid = jnp.where(lo > 0.0, jnp.sqrt(lo) * jnp.sqrt(hi), jnp.maximum(hi * (2.0 ** -16), tiny))
            mid = jnp.minimum(jnp.maximum(mid, lo), hi)
            ok = _count(aff_ref[:, segs[i]] >= mid) >= cap
            out += [jnp.where(ok, mid, lo), jnp.where(ok, hi, mid)]
        return tuple(out)

    init = (jnp.zeros((N_EXPERTS, 1), F32), jnp.full((N_EXPERTS, 1), 2.0, F32)) * n_seg
    bounds = lax.fori_loop(0, ROUTE_ITERS, step, init)

    for i in range(n_seg):
        a = aff_ref[:, segs[i]]
        lo, hi = bounds[2 * i], bounds[2 * i + 1]
        above = a >= hi
        band = (a >= lo) & (a < hi)
        sel = above | (band & (_lane_cumsum(jnp.where(band, 1.0, 0.0)) <= cap - _count(above)))
        taken = _lane_cumsum(jnp.where(sel, 1.0, 0.0))
        slot_ref[:, segs[i]] = jnp.where(sel, taken - 1.0, -1.0).astype(jnp.int32)
        gate_ref[:, segs[i]] = jnp.where(sel, a, 0.0)
    if n_seg == 1:
        n_off = seg // OFFS_STEP
        before = [jnp.zeros((N_EXPERTS, 1), F32)] + [taken[:, OFFS_STEP * k - 1:OFFS_STEP * k] for k in range(1, n_off)]
        before.append(jnp.zeros((N_EXPERTS, 128 - n_off), F32))
        offs_ref[...] = jnp.concatenate(before, axis=1).astype(jnp.int32)
    else:
        offs_ref[...] = jnp.zeros(offs_ref.shape, jnp.int32)


def _route(aff_t, col_block0, n_blocks, width, seg, cap):
    return pl.pallas_call(
        functools.partial(_route_kernel, cap=cap, seg=seg),
        grid=(n_blocks,),
        in_specs=[pl.BlockSpec((N_EXPERTS, width), lambda i: (0, col_block0 + i))],
        out_specs=[pl.BlockSpec((N_EXPERTS, width), lambda i: (0, i))] * 2
                  + [pl.BlockSpec((N_EXPERTS, 128), lambda i: (0, i))],
        out_shape=[jax.ShapeDtypeStruct((N_EXPERTS, n_blocks * width), jnp.int32),
                   jax.ShapeDtypeStruct((N_EXPERTS, n_blocks * width), F32),
                   jax.ShapeDtypeStruct((N_EXPERTS, n_blocks * 128), jnp.int32)],
        compiler_params=_cparams(("parallel",)),
        name="route",
    )(aff_t)


def _gather_lat_kernel(offs_ref, slot_ref, gate_ref, h_ref, o_ref, g_ref, acc_ref, gacc_ref):
    r = pl.program_id(0)
    e = pl.program_id(1)
    win = GATHER_BLOCK + 16
    acc_ref[...] = jnp.zeros(acc_ref.shape, F32)
    gacc_ref[...] = jnp.zeros(gacc_ref.shape, F32)
    rows = lax.broadcasted_iota(jnp.int32, (win, GATHER_BLOCK), 0)
    for b in range(DEC_SEQ // GATHER_BLOCK):
        off = offs_ref[e, r * 128 + b * (GATHER_BLOCK // OFFS_STEP)]
        base = pl.multiple_of((off // 8) * 8, 8)
        toks = slice(b * GATHER_BLOCK, (b + 1) * GATHER_BLOCK)
        hit = (rows + base) == slot_ref[pl.ds(e, 1), toks]
        onehot = jnp.where(hit, 1.0, 0.0).astype(BF16)
        acc_ref[pl.ds(base, win), :] += jnp.dot(onehot, h_ref[toks, :], preferred_element_type=F32)
        gacc_ref[pl.ds(base, win), :] += jnp.sum(jnp.where(hit, gate_ref[pl.ds(e, 1), toks], 0.0), axis=1, keepdims=True)
    o_ref[...] = acc_ref[0:CAP_LAT, :].astype(o_ref.dtype)
    g_ref[...] = gacc_ref[0:CAP_LAT, :]


def _gather_lat(offs, slot, gate, h2):
    rows = pl.BlockSpec((N_EXPERTS, DEC_SEQ), lambda r, e, offs: (0, r))
    acc_rows = CAP_LAT + GATHER_BLOCK + 16
    return pl.pallas_call(
        _gather_lat_kernel,
        grid_spec=pltpu.PrefetchScalarGridSpec(
            num_scalar_prefetch=1,
            grid=(DEC_BATCH, N_EXPERTS),
            in_specs=[rows, rows,
                      pl.BlockSpec((DEC_SEQ, D_MODEL), lambda r, e, offs: (N_CTX // DEC_SEQ + r, 0))],
            out_specs=[pl.BlockSpec((None, CAP_LAT, D_MODEL), lambda r, e, offs: (e, r, 0)),
                       pl.BlockSpec((None, CAP_LAT, 1), lambda r, e, offs: (e, r, 0))],
            scratch_shapes=[pltpu.VMEM((acc_rows, D_MODEL), F32), pltpu.VMEM((acc_rows, 1), F32)]),
        out_shape=[jax.ShapeDtypeStruct((N_EXPERTS, DEC_BATCH * CAP_LAT, D_MODEL), BF16),
                   jax.ShapeDtypeStruct((N_EXPERTS, DEC_BATCH * CAP_LAT, 1), F32)],
        compiler_params=_cparams(("parallel", "arbitrary")),
        name="gather_lat",
    )(offs, slot, gate, h2)


def _ctx_onehot(slot):
    rows = lax.broadcasted_iota(jnp.int32, (CAP_CTX, SEQ), 0)
    hits = [rows == slot[e:e + 1, :] for e in range(N_EXPERTS)]
    onehot = jnp.concatenate([jnp.where(h, 1.0, 0.0) for h in hits], axis=0).astype(BF16)
    return onehot, hits


def _gather_ctx_kernel(slot_ref, gate_ref, h_ref, o_ref, g_ref):
    onehot, hits = _ctx_onehot(slot_ref[...])
    xs = jnp.dot(onehot, h_ref[...], preferred_element_type=F32).astype(o_ref.dtype)
    gate = gate_ref[...]
    for e in range(N_EXPERTS):
        o_ref[e] = xs[e * CAP_CTX:(e + 1) * CAP_CTX]
        g_ref[e] = jnp.sum(jnp.where(hits[e], gate[e:e + 1, :], 0.0), axis=1, keepdims=True)


def _gather_ctx(slot, gate, h2):
    rows = pl.BlockSpec((N_EXPERTS, SEQ), lambda r: (0, r))
    return pl.pallas_call(
        _gather_ctx_kernel,
        grid=(BATCH,),
        in_specs=[rows, rows, pl.BlockSpec((SEQ, D_MODEL), lambda r: (r, 0))],
        out_specs=[pl.BlockSpec((N_EXPERTS, CAP_CTX, D_MODEL), lambda r: (0, r, 0)),
                   pl.BlockSpec((N_EXPERTS, CAP_CTX, 1), lambda r: (0, r, 0))],
        out_shape=[jax.ShapeDtypeStruct((N_EXPERTS, BATCH * CAP_CTX, D_MODEL), BF16),
                   jax.ShapeDtypeStruct((N_EXPERTS, BATCH * CAP_CTX, 1), F32)],
        compiler_params=_cparams(("parallel",)),
        name="gather_ctx",
    )(slot, gate, h2)


def _ffn_kernel(xc_ref, xl_ref, gc_ref, gl_ref, wg_ref, wu_ref, wd_ref, yc_ref, yl_ref, accc_ref, accl_ref):
    f = pl.program_id(1)
    wg = wg_ref[...].astype(BF16)
    wu = wu_ref[...].astype(BF16)
    wd = wd_ref[...].astype(BF16)

    def part(x_ref, gate_ref, acc_ref, y_ref):
        x = x_ref[...]
        a = jnp.dot(x, wg, preferred_element_type=F32)
        up = jnp.dot(x, wu, preferred_element_type=F32)
        mid = (a * jax.nn.sigmoid(a) * up).astype(BF16)
        y = jnp.dot(mid, wd, preferred_element_type=F32)

        @pl.when(f == 0)
        def _():
            acc_ref[...] = y

        @pl.when(f > 0)
        def _():
            acc_ref[...] += y

        @pl.when(f == pl.num_programs(1) - 1)
        def _():
            y_ref[...] = (acc_ref[...] * gate_ref[...]).astype(y_ref.dtype)

    part(xc_ref, gc_ref, accc_ref, yc_ref)
    part(xl_ref, gl_ref, accl_ref, yl_ref)


def _expert_ffn(xs_ctx, xs_lat, gs_ctx, gs_lat, w_gate, w_up, w_down, layer):
    tf = 512
    nc, nl = xs_ctx.shape[1], xs_lat.shape[1]
    return pl.pallas_call(
        _ffn_kernel,
        grid=(N_EXPERTS, EXPERT_FF // tf),
        in_specs=[pl.BlockSpec((None, nc, D_MODEL), lambda e, f: (e, 0, 0)),
                  pl.BlockSpec((None, nl, D_MODEL), lambda e, f: (e, 0, 0)),
                  pl.BlockSpec((None, nc, 1), lambda e, f: (e, 0, 0)),
                  pl.BlockSpec((None, nl, 1), lambda e, f: (e, 0, 0)),
                  pl.BlockSpec((None, None, D_MODEL, tf), lambda e, f: (layer, e, 0, f)),
                  pl.BlockSpec((None, None, D_MODEL, tf), lambda e, f: (layer, e, 0, f)),
                  pl.BlockSpec((None, None, tf, D_MODEL), lambda e, f: (layer, e, f, 0))],
        out_specs=[pl.BlockSpec((None, nc, D_MODEL), lambda e, f: (e, 0, 0)),
                   pl.BlockSpec((None, nl, D_MODEL), lambda e, f: (e, 0, 0))],
        out_shape=[jax.ShapeDtypeStruct(xs_ctx.shape, BF16), jax.ShapeDtypeStruct(xs_lat.shape, BF16)],
        scratch_shapes=[pltpu.VMEM((nc, D_MODEL), F32), pltpu.VMEM((nl, D_MODEL), F32)],
        compiler_params=_cparams(("parallel", "arbitrary")),
        name="expert_ffn",
    )(xs_ctx, xs_lat, gs_ctx, gs_lat, w_gate, w_up, w_down)


def _scatter_lat_kernel(offs_ref, slot_ref, y_ref, x_ref, mod_ref, o_ref):
    r = pl.program_id(0)
    t = pl.program_id(1)
    slot_t = slot_ref[...].astype(F32).T
    lane = lax.broadcasted_iota(jnp.int32, (OFFS_STEP, SCATTER_WINDOW), 1)
    ffn = jnp.zeros((OFFS_STEP, D_MODEL), F32)
    for e in range(N_EXPERTS):
        off = offs_ref[e, r * 128 + t]
        base = pl.multiple_of(jnp.minimum((off // 16) * 16, CAP_LAT - SCATTER_WINDOW), 16)
        onehot = jnp.where((lane + base).astype(F32) == slot_t[:, e:e + 1], 1.0, 0.0).astype(BF16)
        ffn += jnp.dot(onehot, y_ref[e, pl.ds(base, SCATTER_WINDOW), :], preferred_element_type=F32)
    o_ref[...] = x_ref[...] + mod_ref[:, 5 * D_MODEL:6 * D_MODEL] * ffn


def _scatter_lat(offs, slot, ys, x, mod_l):
    tt = OFFS_STEP
    nt = DEC_SEQ // tt
    blk0 = N_CTX // tt
    tok = pl.BlockSpec((tt, D_MODEL), lambda r, t, offs: (blk0 + r * nt + t, 0))
    return pl.pallas_call(
        _scatter_lat_kernel,
        grid_spec=pltpu.PrefetchScalarGridSpec(
            num_scalar_prefetch=1,
            grid=(DEC_BATCH, nt),
            in_specs=[pl.BlockSpec((N_EXPERTS, tt), lambda r, t, offs: (0, r * nt + t)),
                      pl.BlockSpec((N_EXPERTS, CAP_LAT, D_MODEL), lambda r, t, offs: (0, r, 0)),
                      tok,
                      pl.BlockSpec((None, 1, 6 * D_MODEL), lambda r, t, offs: (1 + r, 0, 0))],
            out_specs=tok),
        out_shape=jax.ShapeDtypeStruct(x.shape, F32),
        input_output_aliases={3: 0},
        compiler_params=_cparams(("parallel", "parallel")),
        name="scatter_lat",
    )(offs, slot, ys, x, mod_l)


def _scatter_ctx_kernel(slot_ref, y_ref, x_ref, mod_ref, o_ref):
    onehot, _ = _ctx_onehot(slot_ref[...])
    y = jnp.concatenate([y_ref[e] for e in range(N_EXPERTS)], axis=0)
    ffn = lax.dot_general(onehot, y, TN, preferred_element_type=F32)
    o_ref[...] = x_ref[...] + mod_ref[:, 5 * D_MODEL:6 * D_MODEL] * ffn


def _scatter_ctx(slot, ys, x, mod_l):
    tok = pl.BlockSpec((SEQ, D_MODEL), lambda r: (r, 0))
    return pl.pallas_call(
        _scatter_ctx_kernel,
        grid=(BATCH,),
        in_specs=[pl.BlockSpec((N_EXPERTS, SEQ), lambda r: (0, r)),
                  pl.BlockSpec((N_EXPERTS, CAP_CTX, D_MODEL), lambda r: (0, r, 0)),
                  tok,
                  pl.BlockSpec((None, 1, 6 * D_MODEL), lambda r: (0, 0, 0))],
        out_specs=tok,
        out_shape=jax.ShapeDtypeStruct(x.shape, F32),
        input_output_aliases={2: 0},
        compiler_params=_cparams(("parallel",)),
        name="scatter_ctx",
    )(slot, ys, x, mod_l)


def _final_norm_kernel(x_ref, w_ref, o_ref):
    x = x_ref[...]
    o_ref[...] = x * lax.rsqrt(jnp.mean(x * x, axis=-1, keepdims=True) + EPS) * w_ref[...]


def _final_norm(x, w):
    return pl.pallas_call(
        _final_norm_kernel,
        grid=(N_TILES,),
        in_specs=[pl.BlockSpec((TILE, D_MODEL), lambda i: (i, 0)), pl.BlockSpec((1, D_MODEL), lambda i: (0, 0))],
        out_specs=pl.BlockSpec((TILE, D_MODEL), lambda i: (i, 0)),
        out_shape=jax.ShapeDtypeStruct(x.shape, F32),
        compiler_params=_cparams(("parallel",)),
        name="final_norm",
    )(x, w)


def _rope_tables():
    rows = DEC_SEQ // GRID_W
    row = jnp.repeat(jnp.arange(rows, dtype=F32), GRID_W)
    col = jnp.tile(jnp.arange(GRID_W, dtype=F32), rows)
    n_freq = HEAD_DIM // 4
    inv_freq = ROPE_THETA ** (-jnp.arange(n_freq, dtype=F32) / n_freq)
    ang = jnp.concatenate([row[:, None] * inv_freq, col[:, None] * inv_freq], axis=-1)
    cos, sin = jnp.cos(ang), jnp.sin(ang)
    cos_t = jnp.tile(jnp.concatenate([cos, cos], -1), (1, 128 // HEAD_DIM))
    sin_t = jnp.tile(jnp.concatenate([-sin, sin], -1), (1, 128 // HEAD_DIM))
    cos_t = jnp.concatenate([jnp.ones((TILE, 128), F32), cos_t], axis=0)
    sin_t = jnp.concatenate([jnp.zeros((TILE, 128), F32), sin_t], axis=0)
    return cos_t, sin_t


def _s5_initial_rows(state_ssm):
    st = state_ssm.astype(F32)
    re, im = st[..., 0], st[..., 1]
    both = jnp.stack([jnp.concatenate([re, im], -1), jnp.concatenate([im, re], -1)], axis=3)
    return both.transpose(1, 0, 2, 3, 4, 5).reshape(DEPTH, 4 * DEC_BATCH, N_SSM_GROUPS * 128)


def kernel(x_prompt, x_sample, cache_k, cache_v, state_ssm, state_ret, c, c_ctx, w_mod, b_mod, norm1_w, norm2_w, w_in, w_out, qn_w, kn_w, ssm_lambda_re, ssm_lambda_im, ssm_b_re, ssm_b_im, ssm_c_re, ssm_c_im, ssm_log_dt, ssm_d, ssm_w_glu, ret_decay_logit, ret_norm_w, w_router, w_gate, w_up, w_down, final_norm_w):
    x = jnp.concatenate([x_prompt.reshape(N_CTX, D_MODEL), x_sample.reshape(N_LAT, D_MODEL)], axis=0)
    cond_t = jnp.zeros((D_MODEL, 8), F32).at[:, 0].set(c_ctx).at[:, 1:1 + DEC_BATCH].set(c.T)
    mod = _modulation(cond_t, w_mod, b_mod).reshape(DEPTH, 8, 1, 6 * D_MODEL)
    cos_t, sin_t = _rope_tables()
    zero_ret = jnp.zeros((BATCH, 2, RET_HEADS, HEAD_DIM, HEAD_DIM), F32)
    ctx_blocks = N_CTX // DEC_SEQ

    w_in_bf, w_out_bf, w_glu_bf = w_in.astype(BF16), w_out.astype(BF16), ssm_w_glu.astype(BF16)
    s5_toe, s5_inj, s5_ro, s5_a = jax.vmap(_s5_matrices)(ssm_lambda_re, ssm_lambda_im, ssm_b_re, ssm_b_im,
                                                         ssm_c_re, ssm_c_im, ssm_log_dt)
    s5_h0 = _s5_initial_rows(state_ssm)
    ret_dec, ret_mask, ret_cdec = jax.vmap(_retention_tables)(ret_decay_logit)
    wr = jnp.pad(w_router.astype(F32), ((0, 0), (0, 0), (0, 128 - N_EXPERTS)))
    wr_hi = wr.astype(BF16)
    wr_split = jnp.stack([wr_hi, (wr - wr_hi.astype(F32)).astype(BF16)], axis=1)
    qn_t, kn_t = jnp.tile(qn_w, (1, 2)), jnp.tile(kn_w, (1, 2))
    cache_kv = jnp.concatenate([cache_k.reshape(DEC_BATCH, DEPTH, PAST_LEN, KV_WIDTH),
                                cache_v.reshape(DEC_BATCH, DEPTH, PAST_LEN, KV_WIDTH)], axis=-1).astype(BF16)

    ks, vs, ss, rs = [], [], [], []
    for l in range(DEPTH):
        mod_l = mod[l]
        q, kv, ub, rest = _in_projection(x, mod_l, norm1_w[l].reshape(1, -1), w_in_bf[l],
                                     qn_t[l].reshape(1, -1), kn_t[l].reshape(1, -1), cos_t, sin_t)
        ks.append(rest[:N_CTX, C_K:C_K + KV_WIDTH].reshape(BATCH, SEQ, N_KV_HEADS, HEAD_DIM))
        vs.append(rest[:N_CTX, C_V:C_V + KV_WIDTH].reshape(BATCH, SEQ, N_KV_HEADS, HEAD_DIM))

        kv_ctx = kv[:N_CTX].reshape(BATCH, SEQ, 2 * KV_WIDTH)
        kv_lat = jnp.concatenate([kv[N_CTX:].reshape(DEC_BATCH, DEC_SEQ, 2 * KV_WIDTH), cache_kv[:, l]], axis=1)
        attn_ctx = _attention(q, kv_ctx, 0, BATCH, SEQ, SEQ)
        attn_lat = _attention(q, kv_lat, N_CTX // 256, DEC_BATCH, DEC_SEQ, 256)

        y_loc, states = _s5_local(ub.reshape(S5_ROWS, S5_CHUNK * SSM_WIDTH), s5_toe[l], s5_inj[l])
        hf, hb, fin_f, fin_b = _s5_scan(states, s5_a[l], s5_h0[l])
        ys = _s5_out(y_loc, hf, hb, s5_ro[l]).reshape(N_TOK, SSM_WIDTH)
        fin = jnp.stack([fin_f, fin_b], axis=1).reshape(BATCH, 2, N_SSM_GROUPS, 2, SSM_STATE)
        ss.append(fin.transpose(0, 1, 2, 4, 3))

        nw_ret = ret_norm_w[l].reshape(1, -1)
        ret_ctx, fin_ret = _retention(rest, ret_dec[l], ret_mask[l], ret_cdec[l], zero_ret, nw_ret, 0, BATCH, SEQ)
        ret_lat, _ = _retention(rest, ret_dec[l], ret_mask[l], ret_cdec[l], state_ret[:, l].astype(F32), nw_ret,
                                ctx_blocks, DEC_BATCH, DEC_SEQ)
        rs.append(fin_ret)

        x1, h2, aff_t = _out_projection(x, rest, ys, attn_ctx, attn_lat, ret_ctx, ret_lat, mod_l,
                                        ssm_d[l].reshape(1, -1), w_glu_bf[l], w_out_bf[l],
                                        norm2_w[l].reshape(1, -1), wr_split[l])

        slot_ctx, gate_ctx, _ = _route(aff_t, 0, 1, N_CTX, SEQ, CAP_CTX)
        slot_lat, gate_lat, offs_lat = _route(aff_t, N_CTX // DEC_SEQ, DEC_BATCH, DEC_SEQ, DEC_SEQ, CAP_LAT)
        xs_ctx, gs_ctx = _gather_ctx(slot_ctx, gate_ctx, h2)
        xs_lat, gs_lat = _gather_lat(offs_lat, slot_lat, gate_lat, h2)
        y_ctx, y_lat = _expert_ffn(xs_ctx, xs_lat, gs_ctx, gs_lat, w_gate, w_up, w_down, l)
        x = _scatter_ctx(slot_ctx, y_ctx, x1, mod_l)
        x = _scatter_lat(offs_lat, slot_lat, y_lat, x, mod_l)

    y = _final_norm(x, final_norm_w.reshape(1, -1))
    y_prompt = y[:N_CTX].reshape(BATCH, SEQ, D_MODEL)
    y_sample = y[N_CTX:].reshape(DEC_BATCH, DEC_SEQ, D_MODEL)
    return (y_prompt, y_sample, jnp.stack(ks, axis=1), jnp.stack(vs, axis=1),
            jnp.stack(ss, axis=1), jnp.stack(rs, axis=1))
```

```python
import functools

import jax
import jax.numpy as jnp
from jax import lax
from jax.experimental import pallas as pl
from jax.experimental.pallas import tpu as pltpu

F32 = jnp.float32
BF16 = jnp.bfloat16

D_MODEL = 1024
BATCH = 16
SEQ = 256
DEPTH = 4
DEC_BATCH = 2
DEC_SEQ = 4096
PAST_LEN = 256
GRID_W = 64
HEAD_DIM = 64
SSM_WIDTH = 256
SSM_GROUP = 16
N_SSM_GROUPS = 16
SSM_STATE = 64
ATT_WIDTH = 512
N_HEADS = 8
N_KV_HEADS = 2
KV_WIDTH = 128
RET_WIDTH = 256
RET_HEADS = 4
IN_WIDTH = 2048
RET_CHUNK = 128
N_EXPERTS = 16
EXPERT_FF = 1024
ROPE_THETA = 10000.0
EPS = 1e-6

N_CTX = BATCH * SEQ
N_LAT = DEC_BATCH * DEC_SEQ
N_TOK = N_CTX + N_LAT
TILE = 512
LAT_TQ = 512
N_TILES = N_TOK // TILE
CTX_TILES = N_CTX // TILE
LAT_TILES_PER_REQ = DEC_SEQ // TILE
CAP_CTX = 2 * SEQ // N_EXPERTS
CAP_LAT = 2 * DEC_SEQ // N_EXPERTS
S5_CHUNK = 16
CTX_CHUNKS = SEQ // S5_CHUNK
LAT_CHUNKS = DEC_SEQ // S5_CHUNK
S5_ROWS_CTX = BATCH * CTX_CHUNKS
S5_ROWS = S5_ROWS_CTX + DEC_BATCH * LAT_CHUNKS
C_U, C_K, C_V, C_RQ, C_RK, C_RV, C_RG = 0, 256, 384, 512, 768, 1024, 1280
REST_WIDTH = 1536
ROUTE_ITERS = 48
OFFS_STEP = 128
GATHER_BLOCK = 256
SCATTER_WINDOW = 256
VMEM_LIMIT = 56 * 1024 * 1024

TN = (((0,), (0,)), ((), ()))
NT = (((1,), (1,)), ((), ()))


def _cparams(sem):
    return pltpu.CompilerParams(dimension_semantics=sem, vmem_limit_bytes=VMEM_LIMIT)


def _tile_mod_row(i):
    return jnp.where(i < CTX_TILES, 0, 1 + (i - CTX_TILES) // LAT_TILES_PER_REQ)


def _split_dot(v, m):
    hi = v.astype(BF16)
    lo = (v - hi.astype(F32)).astype(BF16)
    return (jnp.dot(hi, m, preferred_element_type=F32)
            + jnp.dot(lo, m, preferred_element_type=F32))


def _group_avg_matrix():
    r = lax.broadcasted_iota(jnp.int32, (128, 128), 0) // HEAD_DIM
    c = lax.broadcasted_iota(jnp.int32, (128, 128), 1) // HEAD_DIM
    return jnp.where(r == c, 1.0 / HEAD_DIM, 0.0).astype(BF16)


def _mod_kernel(ct_ref, w_ref, b_ref, o_ref):
    c = ct_ref[...]
    s = c * jax.nn.sigmoid(c)
    w = w_ref[...]
    rows = [jnp.sum(w * s[:, r:r + 1], axis=0, keepdims=True) for r in range(3)]
    rows.append(jnp.zeros((5, w.shape[1]), F32))
    o_ref[...] = jnp.concatenate(rows, axis=0) + b_ref[...]


def _modulation(cond_t, w_mod, b_mod):
    tn = 512
    n = 6 * D_MODEL
    return pl.pallas_call(
        _mod_kernel,
        grid=(DEPTH, n // tn),
        in_specs=[pl.BlockSpec((D_MODEL, 8), lambda l, j: (0, 0)),
                  pl.BlockSpec((None, D_MODEL, tn), lambda l, j: (l, 0, j)),
                  pl.BlockSpec((None, 1, tn), lambda l, j: (l, 0, j))],
        out_specs=pl.BlockSpec((None, 8, tn), lambda l, j: (l, 0, j)),
        out_shape=jax.ShapeDtypeStruct((DEPTH, 8, n), F32),
        compiler_params=_cparams(("arbitrary", "arbitrary")),
        name="modulation",
    )(cond_t, w_mod, b_mod.reshape(DEPTH, 1, n))


def _inproj_kernel(x_ref, mod_ref, nw_ref, w_ref, qn_ref, kn_ref, cos_ref, sin_ref, q_ref, kv_ref, ub_ref, rest_ref):
    x = x_ref[...]
    shift = mod_ref[:, 0:D_MODEL]
    scale = mod_ref[:, D_MODEL:2 * D_MODEL]
    y = x * lax.rsqrt(jnp.mean(x * x, axis=-1, keepdims=True) + EPS) * nw_ref[...]
    h = y * (1.0 + scale) + shift
    proj = jnp.dot(h.astype(BF16), w_ref[...], preferred_element_type=F32)

    avg = _group_avg_matrix()
    cos = cos_ref[...]
    sin = sin_ref[...]
    first_half = (lax.broadcasted_iota(jnp.int32, (TILE, 128), 1) % HEAD_DIM) < (HEAD_DIM // 2)

    def head_norm(z, wrow):
        return z * lax.rsqrt(_split_dot(z * z, avg) + EPS) * wrow

    def rope(z):
        partner = jnp.where(first_half, pltpu.roll(z, 128 - HEAD_DIM // 2, 1), pltpu.roll(z, HEAD_DIM // 2, 1))
        return z * cos + partner * sin

    def col(off, j):
        return proj[:, off + 128 * j: off + 128 * (j + 1)]

    qn = qn_ref[...]
    for j in range(ATT_WIDTH // 128):
        z = rope(head_norm(col(SSM_WIDTH, j), qn)) * (HEAD_DIM ** -0.5)
        q_ref[:, 128 * j:128 * (j + 1)] = z.astype(q_ref.dtype)
    p_k = SSM_WIDTH + ATT_WIDTH
    rest_ref[:, C_U:C_U + SSM_WIDTH] = proj[:, 0:SSM_WIDTH]
    ub_ref[...] = proj[:, 0:SSM_WIDTH].astype(ub_ref.dtype)
    k = rope(head_norm(col(p_k, 0), kn_ref[...]))
    v = col(p_k + KV_WIDTH, 0)
    rest_ref[:, C_K:C_K + KV_WIDTH] = k
    rest_ref[:, C_V:C_V + KV_WIDTH] = v
    kv_ref[:, 0:KV_WIDTH] = k.astype(kv_ref.dtype)
    kv_ref[:, KV_WIDTH:2 * KV_WIDTH] = v.astype(kv_ref.dtype)
    p_r = p_k + 2 * KV_WIDTH
    for j in range(RET_WIDTH // 128):
        rest_ref[:, C_RQ + 128 * j:C_RQ + 128 * (j + 1)] = rope(col(p_r, j))
        rest_ref[:, C_RK + 128 * j:C_RK + 128 * (j + 1)] = rope(col(p_r + RET_WIDTH, j)) * (HEAD_DIM ** -0.5)
    rest_ref[:, C_RV:C_RV + 2 * RET_WIDTH] = proj[:, p_r + 2 * RET_WIDTH:p_r + 4 * RET_WIDTH]


def _in_projection(x, mod_l, nw, w_in_bf, qn, kn, cos_t, sin_t):
    def rope_blk(i):
        return (jnp.where(i < CTX_TILES, 0, 1 + (i - CTX_TILES) % LAT_TILES_PER_REQ), 0)
    return pl.pallas_call(
        _inproj_kernel,
        grid=(N_TILES,),
        in_specs=[pl.BlockSpec((TILE, D_MODEL), lambda i: (i, 0)),
                  pl.BlockSpec((None, 1, 6 * D_MODEL), lambda i: (_tile_mod_row(i), 0, 0)),
                  pl.BlockSpec((1, D_MODEL), lambda i: (0, 0)),
                  pl.BlockSpec((D_MODEL, IN_WIDTH), lambda i: (0, 0)),
                  pl.BlockSpec((1, 128), lambda i: (0, 0)),
                  pl.BlockSpec((1, 128), lambda i: (0, 0)),
                  pl.BlockSpec((TILE, 128), rope_blk),
                  pl.BlockSpec((TILE, 128), rope_blk)],
        out_specs=[pl.BlockSpec((TILE, ATT_WIDTH), lambda i: (i, 0)),
                   pl.BlockSpec((TILE, 2 * KV_WIDTH), lambda i: (i, 0)),
                   pl.BlockSpec((TILE, SSM_WIDTH), lambda i: (i, 0)),
                   pl.BlockSpec((TILE, REST_WIDTH), lambda i: (i, 0))],
        out_shape=[jax.ShapeDtypeStruct((N_TOK, ATT_WIDTH), BF16),
                   jax.ShapeDtypeStruct((N_TOK, 2 * KV_WIDTH), BF16),
                   jax.ShapeDtypeStruct((N_TOK, SSM_WIDTH), BF16),
                   jax.ShapeDtypeStruct((N_TOK, REST_WIDTH), F32)],
        compiler_params=_cparams(("parallel",)),
        name="in_projection",
    )(x, mod_l, nw, w_in_bf, qn, kn, cos_t, sin_t)


def _attn_kernel(q_ref, k_ref, v_ref, o_ref):
    group = N_HEADS // N_KV_HEADS
    for kv in range(N_KV_HEADS):
        k = k_ref[:, kv * HEAD_DIM:(kv + 1) * HEAD_DIM]
        v = v_ref[:, kv * HEAD_DIM:(kv + 1) * HEAD_DIM]
        for g in range(group):
            cols = slice((kv * group + g) * HEAD_DIM, (kv * group + g + 1) * HEAD_DIM)
            s = lax.dot_general(q_ref[:, cols], k, NT, preferred_element_type=F32)
            m = jnp.max(s, axis=-1, keepdims=True)
            p = jnp.exp(s - m)
            l = jnp.sum(p, axis=-1, keepdims=True)
            o = jnp.dot(p.astype(BF16), v, preferred_element_type=F32) / l
            o_ref[:, cols] = o.astype(o_ref.dtype)


def _attention(q, kv, row_block0, n_req, lq, tq):
    lk = kv.shape[1]
    nq = lq // tq
    return pl.pallas_call(
        _attn_kernel,
        grid=(n_req, nq),
        in_specs=[pl.BlockSpec((tq, ATT_WIDTH), lambda r, i: (row_block0 + r * nq + i, 0)),
                  pl.BlockSpec((None, lk, KV_WIDTH), lambda r, i: (r, 0, 0)),
                  pl.BlockSpec((None, lk, KV_WIDTH), lambda r, i: (r, 0, 1))],
        out_specs=pl.BlockSpec((tq, ATT_WIDTH), lambda r, i: (r * nq + i, 0)),
        out_shape=jax.ShapeDtypeStruct((n_req * lq, ATT_WIDTH), BF16),
        compiler_params=_cparams(("parallel", "parallel")),
        name="attention",
    )(q, kv, kv)


def _s5_local_kernel(u_ref, t_ref, b_ref, y_ref, s_ref):
    u = u_ref[...]
    y_ref[...] = jnp.dot(u, t_ref[...], preferred_element_type=F32)
    s_ref[...] = jnp.dot(u, b_ref[...], preferred_element_type=F32)


def _s5_local(u2, toeplitz, inject):
    tn = 256
    n = S5_CHUNK * SSM_WIDTH
    wspec = pl.BlockSpec((n, tn), lambda i: (0, i))
    ospec = pl.BlockSpec((S5_ROWS, tn), lambda i: (0, i))
    return pl.pallas_call(
        _s5_local_kernel,
        grid=(n // tn,),
        in_specs=[pl.BlockSpec((S5_ROWS, n), lambda i: (0, 0)), wspec, wspec],
        out_specs=[ospec, ospec],
        out_shape=[jax.ShapeDtypeStruct((S5_ROWS, n), F32)] * 2,
        compiler_params=_cparams(("parallel",)),
        name="s5_local",
    )(u2, toeplitz, inject)


def _s5_scan_kernel(sf_ref, sb_ref, a_ref, h0_ref, hf_ref, hb_ref, ff_ref, fb_ref, sfs_ref, sbs_ref):
    a1f, a2f, a1b, a2b = a_ref[0:1, :], a_ref[1:2, :], a_ref[2:3, :], a_ref[3:4, :]
    w = sf_ref.shape[1]
    for c in range(w // 128):
        cols = slice(128 * c, 128 * (c + 1))
        sfs_ref[:, cols] = pltpu.roll(sf_ref[:, cols], SSM_STATE, 1)
        sbs_ref[:, cols] = pltpu.roll(sb_ref[:, cols], SSM_STATE, 1)

    zero = jnp.zeros((1, w), F32)
    for r in range(BATCH):
        hf, hfs, hb, hbs = zero, zero, zero, zero
        for j in range(CTX_CHUNKS):
            row = r * CTX_CHUNKS + j
            hf_ref[row:row + 1, :] = hf
            hf, hfs = (a1f * hf + a2f * hfs + sf_ref[row:row + 1, :], a1f * hfs - a2f * hf + sfs_ref[row:row + 1, :])
            row = r * CTX_CHUNKS + CTX_CHUNKS - 1 - j
            hb_ref[row:row + 1, :] = hb
            hb, hbs = (a1b * hb + a2b * hbs + sb_ref[row:row + 1, :], a1b * hbs - a2b * hb + sbs_ref[row:row + 1, :])
        ff_ref[r:r + 1, :] = hf
        fb_ref[r:r + 1, :] = hb

    def body(j, carry):
        out = []
        for r in range(DEC_BATCH):
            hf, hfs, hb, hbs = carry[4 * r:4 * r + 4]
            row = pl.ds(S5_ROWS_CTX + r * LAT_CHUNKS + j, 1)
            hf_ref[row, :] = hf
            nf = a1f * hf + a2f * hfs + sf_ref[row, :]
            nfs = a1f * hfs - a2f * hf + sfs_ref[row, :]
            row = pl.ds(S5_ROWS_CTX + r * LAT_CHUNKS + LAT_CHUNKS - 1 - j, 1)
            hb_ref[row, :] = hb
            nb = a1b * hb + a2b * hbs + sb_ref[row, :]
            nbs = a1b * hbs - a2b * hb + sbs_ref[row, :]
            out += [nf, nfs, nb, nbs]
        return tuple(out)

    init = tuple(h0_ref[i:i + 1, :] for i in range(4 * DEC_BATCH))
    lax.fori_loop(0, LAT_CHUNKS, body, init)


def _s5_scan(states, a_rows, h0_rows):
    w = 512
    full = N_SSM_GROUPS * 128
    nb = full // w
    sspec = pl.BlockSpec((S5_ROWS, w), lambda i: (0, i))
    fspec = pl.BlockSpec((BATCH, w), lambda i: (0, i))
    return pl.pallas_call(
        _s5_scan_kernel,
        grid=(nb,),
        in_specs=[sspec,
                  pl.BlockSpec((S5_ROWS, w), lambda i: (0, nb + i)),
                  pl.BlockSpec((8, w), lambda i: (0, i)),
                  pl.BlockSpec((8, w), lambda i: (0, i))],
        out_specs=[sspec, sspec, fspec, fspec],
        out_shape=[jax.ShapeDtypeStruct((S5_ROWS, full), F32)] * 2 + [jax.ShapeDtypeStruct((BATCH, full), F32)] * 2,
        scratch_shapes=[pltpu.VMEM((S5_ROWS, w), F32), pltpu.VMEM((S5_ROWS, w), F32)],
        compiler_params=_cparams(("parallel",)),
        name="s5_scan",
    )(states, states, a_rows, h0_rows)


def _s5_out_kernel(y_ref, hf_ref, hb_ref, c_ref, o_ref, h_ref):
    full = N_SSM_GROUPS * 128

    @pl.when(pl.program_id(0) == 0)
    def _():
        h_ref[:, 0:full] = hf_ref[...].astype(BF16)
        h_ref[:, full:2 * full] = hb_ref[...].astype(BF16)

    o_ref[...] = y_ref[...] + jnp.dot(h_ref[...], c_ref[...], preferred_element_type=F32)


def _s5_out(y_loc, hf, hb, readout):
    tn = 256
    n = S5_CHUNK * SSM_WIDTH
    full = N_SSM_GROUPS * 128
    hspec = pl.BlockSpec((S5_ROWS, full), lambda i: (0, 0))
    return pl.pallas_call(
        _s5_out_kernel,
        grid=(n // tn,),
        in_specs=[pl.BlockSpec((S5_ROWS, tn), lambda i: (0, i)), hspec, hspec,
                  pl.BlockSpec((2 * full, tn), lambda i: (0, i))],
        out_specs=pl.BlockSpec((S5_ROWS, tn), lambda i: (0, i)),
        out_shape=jax.ShapeDtypeStruct((S5_ROWS, n), F32),
        scratch_shapes=[pltpu.VMEM((S5_ROWS, 2 * full), BF16)],
        compiler_params=_cparams(("arbitrary",)),
        name="s5_out",
    )(y_loc, hf, hb, readout)


def _s5_matrices(lam_re, lam_im, b_re, b_im, c_re, c_im, log_dt):
    hp = lax.Precision.HIGHEST
    n = S5_CHUNK
    tau = jnp.arange(n + 1, dtype=F32)
    toes, injs, ros, a_rows = [], [], [], []
    for di in range(2):
        lr, li = lam_re[di].astype(F32), lam_im[di].astype(F32)
        dt = jnp.exp(log_dt[di].astype(F32))[:, None]
        mag = jnp.exp(lr * dt * tau[:, None, None])
        ang = li * dt * tau[:, None, None]
        e_re, e_im = mag * jnp.cos(ang), mag * jnp.sin(ang)
        nr, ni = e_re[1] - 1.0, e_im[1]
        den = lr * lr + li * li
        f_re, f_im = (nr * lr + ni * li) / den, (ni * lr - nr * li) / den
        br, bi = b_re[di].astype(F32), b_im[di].astype(F32)
        bb_re = f_re[..., None] * br - f_im[..., None] * bi
        bb_im = f_re[..., None] * bi + f_im[..., None] * br
        cr, ci = c_re[di].astype(F32), c_im[di].astype(F32)
        ce_re = cr[None] * e_re[:n, :, None, :] - ci[None] * e_im[:n, :, None, :]
        ce_im = cr[None] * e_im[:n, :, None, :] + ci[None] * e_re[:n, :, None, :]
        kern = (jnp.einsum('tgcp,gpd->gdtc', ce_re, bb_re, precision=hp)
                - jnp.einsum('tgcp,gpd->gdtc', ce_im, bb_im, precision=hp))
        toes.append(kern.reshape(N_SSM_GROUPS * SSM_GROUP, n * SSM_GROUP))
        pe_re, pe_im = (e_re[:n][::-1], e_im[:n][::-1]) if di == 0 else (e_re[:n], e_im[:n])
        inj_re = pe_re[..., None] * bb_re[None] - pe_im[..., None] * bb_im[None]
        inj_im = pe_re[..., None] * bb_im[None] + pe_im[..., None] * bb_re[None]
        injs.append(jnp.concatenate([inj_re.transpose(0, 1, 3, 2), inj_im.transpose(0, 1, 3, 2)], -1))
        qe_re, qe_im = (e_re[1:], e_im[1:]) if di == 0 else (e_re[1:][::-1], e_im[1:][::-1])
        ro_re = cr[None] * qe_re[:, :, None, :] - ci[None] * qe_im[:, :, None, :]
        ro_im = cr[None] * qe_im[:, :, None, :] + ci[None] * qe_re[:, :, None, :]
        ros.append(jnp.concatenate([ro_re.transpose(1, 3, 0, 2), -ro_im.transpose(1, 3, 0, 2)], axis=1))
        a_re, a_im = e_re[n], e_im[n]
        a_rows.append(jnp.concatenate([a_re, a_re], -1).reshape(1, -1))
        a_rows.append(jnp.concatenate([-a_im, a_im], -1).reshape(1, -1))
    size = n * SSM_WIDTH
    g = N_SSM_GROUPS
    row = jnp.arange(size)[:, None]
    col = jnp.arange(size)[None, :]
    src = jnp.arange(n * SSM_GROUP)[:, None]

    def spread(table, copy, keep):
        wide = jnp.dot(table.astype(BF16), copy.astype(BF16), preferred_element_type=F32)
        return jnp.where(keep, wide, 0.0).astype(BF16)

    copy_tc = (src // SSM_GROUP == col // SSM_WIDTH) & (src % SSM_GROUP == col % SSM_GROUP)
    copy_dp = (src // 128 == col // (g * 128)) & (src % 128 == col % 128)
    lag_c = jnp.arange(2 * n * SSM_GROUP)[None, :, None]
    dst_c = jnp.arange(n * SSM_GROUP)[None, None, :]
    s_idx = jnp.arange(n)[:, None, None]
    lag_t = (lag_c % (n * SSM_GROUP)) // SSM_GROUP
    dst_t = dst_c // SSM_GROUP
    want = jnp.where(lag_c < n * SSM_GROUP, dst_t - s_idx, s_idx - dst_t)
    shift = ((lag_c % SSM_GROUP == dst_c % SSM_GROUP) & (lag_t == want)).astype(F32)
    toe = jnp.einsum('rk,skn->srn', jnp.concatenate(toes, axis=1), shift, precision=hp)
    toe = toe.reshape(size, n * SSM_GROUP)
    toeplitz = spread(toe, copy_tc, (row // SSM_GROUP) % g == (col // SSM_GROUP) % g)
    inj = jnp.stack(injs, axis=3).reshape(size, 2 * 128)
    inject = spread(inj, copy_dp, (row // SSM_GROUP) % g == (col // 128) % g)
    ro = jnp.stack(ros, axis=0).reshape(size, n * SSM_GROUP)
    readout = spread(ro, copy_tc, (row // 128) % g == (col // SSM_GROUP) % g)
    a_rows = jnp.concatenate(a_rows + [jnp.zeros((4, N_SSM_GROUPS * 128), F32)], axis=0)
    return toeplitz, inject, readout, a_rows


def _ret_kernel(q_ref, k_ref, v_ref, g_ref, dec_ref, mask_ref, cd_ref, s0_ref, nw_ref, o_ref, fin_ref,
                kvf_ref, kvb_ref, *, n_chunks):
    hd = HEAD_DIM
    nh = 2

    def local_state(i, _):
        rows = pl.ds(pl.multiple_of(i * RET_CHUNK, RET_CHUNK), RET_CHUNK)
        k = k_ref[rows, :]
        v = v_ref[rows, :].astype(BF16)
        kf = (k * dec_ref[1]).astype(BF16)
        kb = (k * dec_ref[3]).astype(BF16)
        for h in range(nh):
            ls = slice(h * hd, (h + 1) * hd)
            kvf_ref[i, h] = lax.dot_general(kf[:, ls], v[:, ls], TN, preferred_element_type=F32)
            kvb_ref[i, h] = lax.dot_general(kb[:, ls], v[:, ls], TN, preferred_element_type=F32)
        return 0

    lax.fori_loop(0, n_chunks, local_state, 0, unroll=min(4, n_chunks))

    def scan_f(i, s):
        loc = kvf_ref[i]
        kvf_ref[i] = s
        return cd_ref[0] * s + loc

    def scan_b(i, s):
        j = n_chunks - 1 - i
        loc = kvb_ref[j]
        kvb_ref[j] = s
        return cd_ref[1] * s + loc

    fin_ref[0] = lax.fori_loop(0, n_chunks, scan_f, s0_ref[0], unroll=min(4, n_chunks))
    fin_ref[1] = lax.fori_loop(0, n_chunks, scan_b, s0_ref[1], unroll=min(4, n_chunks))

    avg = _group_avg_matrix()

    def outputs(i, _):
        rows = pl.ds(pl.multiple_of(i * RET_CHUNK, RET_CHUNK), RET_CHUNK)
        q = q_ref[rows, :]
        qb = q.astype(BF16)
        kb = k_ref[rows, :].astype(BF16)
        v = v_ref[rows, :].astype(BF16)
        qf = (q * dec_ref[0]).astype(BF16)
        qr = (q * dec_ref[2]).astype(BF16)
        outs = []
        for h in range(nh):
            ls = slice(h * hd, (h + 1) * hd)
            inner = lax.dot_general(qb[:, ls], kb[:, ls], NT, preferred_element_type=F32) * mask_ref[h]
            o = jnp.dot(inner.astype(BF16), v[:, ls], preferred_element_type=F32)
            o += jnp.dot(qf[:, ls], kvf_ref[i, h].astype(BF16), preferred_element_type=F32)
            o += jnp.dot(qr[:, ls], kvb_ref[i, h].astype(BF16), preferred_element_type=F32)
            outs.append(o)
        o = jnp.concatenate(outs, axis=1)
        d = o - _split_dot(o, avg)
        o = d * lax.rsqrt(_split_dot(d * d, avg) + EPS) * nw_ref[...]
        g = g_ref[rows, :]
        o_ref[rows, :] = (g * jax.nn.sigmoid(g) * o).astype(o_ref.dtype)
        return 0

    lax.fori_loop(0, n_chunks, outputs, 0, unroll=2)


def _retention(rest, dec, mask, cdec, s0, nw, row_block0, n_req, length):
    n_chunks = length // RET_CHUNK
    hp = RET_HEADS // 2

    def tok(cb):
        return pl.BlockSpec((length, 128), lambda r, p: (row_block0 + r, cb + p))

    state = pl.BlockSpec((None, 2, 2, HEAD_DIM, HEAD_DIM), lambda r, p: (r, 0, p, 0, 0))
    return pl.pallas_call(
        functools.partial(_ret_kernel, n_chunks=n_chunks),
        grid=(n_req, hp),
        in_specs=[tok(C_RQ // 128), tok(C_RK // 128), tok(C_RV // 128), tok(C_RG // 128),
                  pl.BlockSpec((4, RET_CHUNK, 128), lambda r, p: (0, 0, p)),
                  pl.BlockSpec((2, RET_CHUNK, RET_CHUNK), lambda r, p: (p, 0, 0)),
                  pl.BlockSpec((2, 2, HEAD_DIM, HEAD_DIM), lambda r, p: (0, p, 0, 0)),
                  state,
                  pl.BlockSpec((1, 128), lambda r, p: (0, p))],
        out_specs=[pl.BlockSpec((length, 128), lambda r, p: (r, p)), state],
        out_shape=[jax.ShapeDtypeStruct((n_req * length, RET_WIDTH), BF16),
                   jax.ShapeDtypeStruct((n_req, 2, RET_HEADS, HEAD_DIM, HEAD_DIM), F32)],
        scratch_shapes=[pltpu.VMEM((n_chunks, 2, HEAD_DIM, HEAD_DIM), F32),
                        pltpu.VMEM((n_chunks, 2, HEAD_DIM, HEAD_DIM), F32)],
        compiler_params=_cparams(("parallel", "parallel")),
        name="retention",
    )(rest, rest, rest, rest, dec, mask, cdec, s0, nw)


def _retention_tables(decay_logit):
    lg = jax.nn.log_sigmoid(decay_logit.astype(F32))
    idx = jnp.arange(RET_CHUNK, dtype=F32)
    rel = idx[:, None] - idx[None, :]
    d_f = jnp.where(rel >= 0, jnp.exp(lg[0][:, None, None] * jnp.maximum(rel, 0.0)), 0.0)
    d_b = jnp.where(rel <= 0, jnp.exp(lg[1][:, None, None] * jnp.maximum(-rel, 0.0)), 0.0)
    mask = d_f + d_b

    def lanes(t):
        return jnp.repeat(t.T, HEAD_DIM, axis=1)

    dec = jnp.stack([lanes(jnp.exp(lg[0][:, None] * (idx + 1.0))),
                     lanes(jnp.exp(lg[0][:, None] * (RET_CHUNK - 1.0 - idx))),
                     lanes(jnp.exp(lg[1][:, None] * (RET_CHUNK - idx))),
                     lanes(jnp.exp(lg[1][:, None] * idx))], axis=0)
    cdec = jnp.broadcast_to(jnp.exp(lg * RET_CHUNK)[:, :, None, None], (2, RET_HEADS, HEAD_DIM, HEAD_DIM))
    return dec, mask, cdec


def _outproj_kernel(x_ref, rest_ref, ys_ref, atc_ref, atl_ref, rtc_ref, rtl_ref, mod_ref, d_ref, wglu_ref, wout_ref,
                    nw_ref, wr_ref, x1_ref, h2_ref, aff_ref):
    y = ys_ref[...] + rest_ref[...] * d_ref[...]
    y = jax.nn.gelu(y)
    y = y * jax.nn.sigmoid(jnp.dot(y.astype(BF16), wglu_ref[...], preferred_element_type=F32))
    is_ctx = pl.program_id(0) < CTX_TILES
    attn = jnp.where(is_ctx, atc_ref[...], atl_ref[...])
    ret = jnp.where(is_ctx, rtc_ref[...], rtl_ref[...])
    mix = (jnp.dot(y.astype(BF16), wout_ref[0:SSM_WIDTH, :], preferred_element_type=F32)
           + jnp.dot(attn, wout_ref[SSM_WIDTH:SSM_WIDTH + ATT_WIDTH, :], preferred_element_type=F32)
           + jnp.dot(ret, wout_ref[SSM_WIDTH + ATT_WIDTH:, :], preferred_element_type=F32))
    gate1 = mod_ref[:, 2 * D_MODEL:3 * D_MODEL]
    shift2 = mod_ref[:, 3 * D_MODEL:4 * D_MODEL]
    scale2 = mod_ref[:, 4 * D_MODEL:5 * D_MODEL]
    x1 = x_ref[...] + gate1 * mix
    x1_ref[...] = x1
    h2 = (x1 * lax.rsqrt(jnp.mean(x1 * x1, axis=-1, keepdims=True) + EPS) * nw_ref[...]) * (1.0 + scale2) + shift2
    h2_ref[...] = h2.astype(h2_ref.dtype)
    hi = h2.astype(BF16)
    lo = (h2 - hi.astype(F32)).astype(BF16)
    logits = (jnp.dot(hi, wr_ref[0], preferred_element_type=F32)
              + jnp.dot(lo, wr_ref[0], preferred_element_type=F32)
              + jnp.dot(hi, wr_ref[1], preferred_element_type=F32))
    valid = lax.broadcasted_iota(jnp.int32, logits.shape, 1) < N_EXPERTS
    logits = jnp.where(valid, logits, -1e30)
    e = jnp.exp(logits - jnp.max(logits, axis=-1, keepdims=True))
    aff = e / jnp.sum(e, axis=-1, keepdims=True)
    aff_ref[...] = aff.T


def _out_projection(x, rest, ys, attn_ctx, attn_lat, ret_ctx, ret_lat, mod_l, d_row, wglu_bf, wout_bf, nw2, wr_split):
    ctx_blk = lambda i: (jnp.minimum(i, CTX_TILES - 1), 0)
    lat_blk = lambda i: (jnp.maximum(i - CTX_TILES, 0), 0)
    return pl.pallas_call(
        _outproj_kernel,
        grid=(N_TILES,),
        in_specs=[pl.BlockSpec((TILE, D_MODEL), lambda i: (i, 0)),
                  pl.BlockSpec((TILE, SSM_WIDTH), lambda i: (i, C_U // SSM_WIDTH)),
                  pl.BlockSpec((TILE, SSM_WIDTH), lambda i: (i, 0)),
                  pl.BlockSpec((TILE, ATT_WIDTH), ctx_blk),
                  pl.BlockSpec((TILE, ATT_WIDTH), lat_blk),
                  pl.BlockSpec((TILE, RET_WIDTH), ctx_blk),
                  pl.BlockSpec((TILE, RET_WIDTH), lat_blk),
                  pl.BlockSpec((None, 1, 6 * D_MODEL), lambda i: (_tile_mod_row(i), 0, 0)),
                  pl.BlockSpec((1, SSM_WIDTH), lambda i: (0, 0)),
                  pl.BlockSpec((SSM_WIDTH, SSM_WIDTH), lambda i: (0, 0)),
                  pl.BlockSpec((D_MODEL, D_MODEL), lambda i: (0, 0)),
                  pl.BlockSpec((1, D_MODEL), lambda i: (0, 0)),
                  pl.BlockSpec((2, D_MODEL, 128), lambda i: (0, 0, 0))],
        out_specs=[pl.BlockSpec((TILE, D_MODEL), lambda i: (i, 0)),
                   pl.BlockSpec((TILE, D_MODEL), lambda i: (i, 0)),
                   pl.BlockSpec((128, TILE), lambda i: (0, i))],
        out_shape=[jax.ShapeDtypeStruct((N_TOK, D_MODEL), F32),
                   jax.ShapeDtypeStruct((N_TOK, D_MODEL), BF16),
                   jax.ShapeDtypeStruct((128, N_TOK), F32)],
        compiler_params=_cparams(("parallel",)),
        name="out_projection",
    )(x, rest, ys, attn_ctx, attn_lat, ret_ctx, ret_lat, mod_l, d_row, wglu_bf, wout_bf, nw2, wr_split)


def _lane_cumsum(x01):
    rows, n = x01.shape
    r = lax.broadcasted_iota(jnp.int32, (256, 256), 0)
    c = lax.broadcasted_iota(jnp.int32, (256, 256), 1)
    tri = jnp.where(r <= c, 1.0, 0.0).astype(BF16)
    off = jnp.zeros((rows, 1), F32)
    parts = []
    for j in range(n // 256):
        cs = jnp.dot(x01[:, 256 * j:256 * (j + 1)].astype(BF16), tri, preferred_element_type=F32) + off
        parts.append(cs)
        off = cs[:, 255:256]
    return jnp.concatenate(parts, axis=1)


def _count(m):
    return jnp.sum(jnp.where(m, 1.0, 0.0), axis=1, keepdims=True)


def _route_kernel(aff_ref, slot_ref, gate_ref, offs_ref, *, cap, seg):
    n_seg = aff_ref.shape[1] // seg
    segs = [slice(i * seg, (i + 1) * seg) for i in range(n_seg)]
    tiny = float(jnp.finfo(jnp.float32).tiny)

    def step(_, bounds):
        out = []
        for i in range(n_seg):
            lo, hi = bounds[2 * i], bounds[2 * i + 1]
            mid = jnp.where(lo > 0.0, jnp.sqrt(lo) * jnp.sqrt(hi), jnp.maximum(hi * (2.0 ** -16), tiny))
            mid = jnp.minimum(jnp.maximum(mid, lo), hi)
            ok = _count(aff_ref[:, segs[i]] >= mid) >= cap
            out += [jnp.where(ok, mid, lo), jnp.where(ok, hi, mid)]
        return tuple(out)

    init = (jnp.zeros((N_EXPERTS, 1), F32), jnp.full((N_EXPERTS, 1), 2.0, F32)) * n_seg
    bounds = lax.fori_loop(0, ROUTE_ITERS, step, init)

    for i in range(n_seg):
        a = aff_ref[:, segs[i]]
        lo, hi = bounds[2 * i], bounds[2 * i + 1]
        above = a >= hi
        band = (a >= lo) & (a < hi)
        sel = above | (band & (_lane_cumsum(jnp.where(band, 1.0, 0.0)) <= cap - _count(above)))
        taken = _lane_cumsum(jnp.where(sel, 1.0, 0.0))
        slot_ref[:, segs[i]] = jnp.where(sel, taken - 1.0, -1.0).astype(jnp.int32)
        gate_ref[:, segs[i]] = jnp.where(sel, a, 0.0)
    if n_seg == 1:
        n_off = seg // OFFS_STEP
        before = [jnp.zeros((N_EXPERTS, 1), F32)] + [taken[:, OFFS_STEP * k - 1:OFFS_STEP * k] for k in range(1, n_off)]
        before.append(jnp.zeros((N_EXPERTS, 128 - n_off), F32))
        offs_ref[...] = jnp.concatenate(before, axis=1).astype(jnp.int32)
    else:
        offs_ref[...] = jnp.zeros(offs_ref.shape, jnp.int32)


def _route(aff_t, col_block0, n_blocks, width, seg, cap):
    return pl.pallas_call(
        functools.partial(_route_kernel, cap=cap, seg=seg),
        grid=(n_blocks,),
        in_specs=[pl.BlockSpec((N_EXPERTS, width), lambda i: (0, col_block0 + i))],
        out_specs=[pl.BlockSpec((N_EXPERTS, width), lambda i: (0, i))] * 2
                  + [pl.BlockSpec((N_EXPERTS, 128), lambda i: (0, i))],
        out_shape=[jax.ShapeDtypeStruct((N_EXPERTS, n_blocks * width), jnp.int32),
                   jax.ShapeDtypeStruct((N_EXPERTS, n_blocks * width), F32),
                   jax.ShapeDtypeStruct((N_EXPERTS, n_blocks * 128), jnp.int32)],
        compiler_params=_cparams(("parallel",)),
        name="route",
    )(aff_t)


def _gather_lat_kernel(offs_ref, slot_ref, gate_ref, h_ref, o_ref, g_ref, acc_ref, gacc_ref):
    r = pl.program_id(0)
    e = pl.program_id(1)
    win = GATHER_BLOCK + 16
    acc_ref[...] = jnp.zeros(acc_ref.shape, F32)
    gacc_ref[...] = jnp.zeros(gacc_ref.shape, F32)
    rows = lax.broadcasted_iota(jnp.int32, (win, GATHER_BLOCK), 0)
    for b in range(DEC_SEQ // GATHER_BLOCK):
        off = offs_ref[e, r * 128 + b * (GATHER_BLOCK // OFFS_STEP)]
        base = pl.multiple_of((off // 8) * 8, 8)
        toks = slice(b * GATHER_BLOCK, (b + 1) * GATHER_BLOCK)
        hit = (rows + base) == slot_ref[pl.ds(e, 1), toks]
        onehot = jnp.where(hit, 1.0, 0.0).astype(BF16)
        acc_ref[pl.ds(base, win), :] += jnp.dot(onehot, h_ref[toks, :], preferred_element_type=F32)
        gacc_ref[pl.ds(base, win), :] += jnp.sum(jnp.where(hit, gate_ref[pl.ds(e, 1), toks], 0.0), axis=1, keepdims=True)
    o_ref[...] = acc_ref[0:CAP_LAT, :].astype(o_ref.dtype)
    g_ref[...] = gacc_ref[0:CAP_LAT, :]


def _gather_lat(offs, slot, gate, h2):
    rows = pl.BlockSpec((N_EXPERTS, DEC_SEQ), lambda r, e, offs: (0, r))
    acc_rows = CAP_LAT + GATHER_BLOCK + 16
    return pl.pallas_call(
        _gather_lat_kernel,
        grid_spec=pltpu.PrefetchScalarGridSpec(
            num_scalar_prefetch=1,
            grid=(DEC_BATCH, N_EXPERTS),
            in_specs=[rows, rows,
                      pl.BlockSpec((DEC_SEQ, D_MODEL), lambda r, e, offs: (N_CTX // DEC_SEQ + r, 0))],
            out_specs=[pl.BlockSpec((None, CAP_LAT, D_MODEL), lambda r, e, offs: (e, r, 0)),
                       pl.BlockSpec((None, CAP_LAT, 1), lambda r, e, offs: (e, r, 0))],
            scratch_shapes=[pltpu.VMEM((acc_rows, D_MODEL), F32), pltpu.VMEM((acc_rows, 1), F32)]),
        out_shape=[jax.ShapeDtypeStruct((N_EXPERTS, DEC_BATCH * CAP_LAT, D_MODEL), BF16),
                   jax.ShapeDtypeStruct((N_EXPERTS, DEC_BATCH * CAP_LAT, 1), F32)],
        compiler_params=_cparams(("parallel", "arbitrary")),
        name="gather_lat",
    )(offs, slot, gate, h2)


def _ctx_onehot(slot):
    rows = lax.broadcasted_iota(jnp.int32, (CAP_CTX, SEQ), 0)
    hits = [rows == slot[e:e + 1, :] for e in range(N_EXPERTS)]
    onehot = jnp.concatenate([jnp.where(h, 1.0, 0.0) for h in hits], axis=0).astype(BF16)
    return onehot, hits


def _gather_ctx_kernel(slot_ref, gate_ref, h_ref, o_ref, g_ref):
    onehot, hits = _ctx_onehot(slot_ref[...])
    xs = jnp.dot(onehot, h_ref[...], preferred_element_type=F32).astype(o_ref.dtype)
    gate = gate_ref[...]
    for e in range(N_EXPERTS):
        o_ref[e] = xs[e * CAP_CTX:(e + 1) * CAP_CTX]
        g_ref[e] = jnp.sum(jnp.where(hits[e], gate[e:e + 1, :], 0.0), axis=1, keepdims=True)


def _gather_ctx(slot, gate, h2):
    rows = pl.BlockSpec((N_EXPERTS, SEQ), lambda r: (0, r))
    return pl.pallas_call(
        _gather_ctx_kernel,
        grid=(BATCH,),
        in_specs=[rows, rows, pl.BlockSpec((SEQ, D_MODEL), lambda r: (r, 0))],
        out_specs=[pl.BlockSpec((N_EXPERTS, CAP_CTX, D_MODEL), lambda r: (0, r, 0)),
                   pl.BlockSpec((N_EXPERTS, CAP_CTX, 1), lambda r: (0, r, 0))],
        out_shape=[jax.ShapeDtypeStruct((N_EXPERTS, BATCH * CAP_CTX, D_MODEL), BF16),
                   jax.ShapeDtypeStruct((N_EXPERTS, BATCH * CAP_CTX, 1), F32)],
        compiler_params=_cparams(("parallel",)),
        name="gather_ctx",
    )(slot, gate, h2)


def _ffn_kernel(xc_ref, xl_ref, gc_ref, gl_ref, wg_ref, wu_ref, wd_ref, yc_ref, yl_ref, accc_ref, accl_ref):
    f = pl.program_id(1)
    wg = wg_ref[...].astype(BF16)
    wu = wu_ref[...].astype(BF16)
    wd = wd_ref[...].astype(BF16)

    def part(x_ref, gate_ref, acc_ref, y_ref):
        x = x_ref[...]
        a = jnp.dot(x, wg, preferred_element_type=F32)
        up = jnp.dot(x, wu, preferred_element_type=F32)
        mid = (a * jax.nn.sigmoid(a) * up).astype(BF16)
        y = jnp.dot(mid, wd, preferred_element_type=F32)

        @pl.when(f == 0)
        def _():
            acc_ref[...] = y

        @pl.when(f > 0)
        def _():
            acc_ref[...] += y

        @pl.when(f == pl.num_programs(1) - 1)
        def _():
            y_ref[...] = (acc_ref[...] * gate_ref[...]).astype(y_ref.dtype)

    part(xc_ref, gc_ref, accc_ref, yc_ref)
    part(xl_ref, gl_ref, accl_ref, yl_ref)


def _expert_ffn(xs_ctx, xs_lat, gs_ctx, gs_lat, w_gate, w_up, w_down, layer):
    tf = 512
    nc, nl = xs_ctx.shape[1], xs_lat.shape[1]
    return pl.pallas_call(
        _ffn_kernel,
        grid=(N_EXPERTS, EXPERT_FF // tf),
        in_specs=[pl.BlockSpec((None, nc, D_MODEL), lambda e, f: (e, 0, 0)),
                  pl.BlockSpec((None, nl, D_MODEL), lambda e, f: (e, 0, 0)),
                  pl.BlockSpec((None, nc, 1), lambda e, f: (e, 0, 0)),
                  pl.BlockSpec((None, nl, 1), lambda e, f: (e, 0, 0)),
                  pl.BlockSpec((None, None, D_MODEL, tf), lambda e, f: (layer, e, 0, f)),
                  pl.BlockSpec((None, None, D_MODEL, tf), lambda e, f: (layer, e, 0, f)),
                  pl.BlockSpec((None, None, tf, D_MODEL), lambda e, f: (layer, e, f, 0))],
        out_specs=[pl.BlockSpec((None, nc, D_MODEL), lambda e, f: (e, 0, 0)),
                   pl.BlockSpec((None, nl, D_MODEL), lambda e, f: (e, 0, 0))],
        out_shape=[jax.ShapeDtypeStruct(xs_ctx.shape, BF16), jax.ShapeDtypeStruct(xs_lat.shape, BF16)],
        scratch_shapes=[pltpu.VMEM((nc, D_MODEL), F32), pltpu.VMEM((nl, D_MODEL), F32)],
        compiler_params=_cparams(("parallel", "arbitrary")),
        name="expert_ffn",
    )(xs_ctx, xs_lat, gs_ctx, gs_lat, w_gate, w_up, w_down)


def _scatter_lat_kernel(offs_ref, slot_ref, y_ref, x_ref, mod_ref, o_ref):
    r = pl.program_id(0)
    t = pl.program_id(1)
    slot_t = slot_ref[...].astype(F32).T
    lane = lax.broadcasted_iota(jnp.int32, (OFFS_STEP, SCATTER_WINDOW), 1)
    ffn = jnp.zeros((OFFS_STEP, D_MODEL), F32)
    for e in range(N_EXPERTS):
        off = offs_ref[e, r * 128 + t]
        base = pl.multiple_of(jnp.minimum((off // 16) * 16, CAP_LAT - SCATTER_WINDOW), 16)
        onehot = jnp.where((lane + base).astype(F32) == slot_t[:, e:e + 1], 1.0, 0.0).astype(BF16)
        ffn += jnp.dot(onehot, y_ref[e, pl.ds(base, SCATTER_WINDOW), :], preferred_element_type=F32)
    o_ref[...] = x_ref[...] + mod_ref[:, 5 * D_MODEL:6 * D_MODEL] * ffn


def _scatter_lat(offs, slot, ys, x, mod_l):
    tt = OFFS_STEP
    nt = DEC_SEQ // tt
    blk0 = N_CTX // tt
    tok = pl.BlockSpec((tt, D_MODEL), lambda r, t, offs: (blk0 + r * nt + t, 0))
    return pl.pallas_call(
        _scatter_lat_kernel,
        grid_spec=pltpu.PrefetchScalarGridSpec(
            num_scalar_prefetch=1,
            grid=(DEC_BATCH, nt),
            in_specs=[pl.BlockSpec((N_EXPERTS, tt), lambda r, t, offs: (0, r * nt + t)),
                      pl.BlockSpec((N_EXPERTS, CAP_LAT, D_MODEL), lambda r, t, offs: (0, r, 0)),
                      tok,
                      pl.BlockSpec((None, 1, 6 * D_MODEL), lambda r, t, offs: (1 + r, 0, 0))],
            out_specs=tok),
        out_shape=jax.ShapeDtypeStruct(x.shape, F32),
        input_output_aliases={3: 0},
        compiler_params=_cparams(("parallel", "parallel")),
        name="scatter_lat",
    )(offs, slot, ys, x, mod_l)


def _scatter_ctx_kernel(slot_ref, y_ref, x_ref, mod_ref, o_ref):
    onehot, _ = _ctx_onehot(slot_ref[...])
    y = jnp.concatenate([y_ref[e] for e in range(N_EXPERTS)], axis=0)
    ffn = lax.dot_general(onehot, y, TN, preferred_element_type=F32)
    o_ref[...] = x_ref[...] + mod_ref[:, 5 * D_MODEL:6 * D_MODEL] * ffn


def _scatter_ctx(slot, ys, x, mod_l):
    tok = pl.BlockSpec((SEQ, D_MODEL), lambda r: (r, 0))
    return pl.pallas_call(
        _scatter_ctx_kernel,
        grid=(BATCH,),
        in_specs=[pl.BlockSpec((N_EXPERTS, SEQ), lambda r: (0, r)),
                  pl.BlockSpec((N_EXPERTS, CAP_CTX, D_MODEL), lambda r: (0, r, 0)),
                  tok,
                  pl.BlockSpec((None, 1, 6 * D_MODEL), lambda r: (0, 0, 0))],
        out_specs=tok,
        out_shape=jax.ShapeDtypeStruct(x.shape, F32),
        input_output_aliases={2: 0},
        compiler_params=_cparams(("parallel",)),
        name="scatter_ctx",
    )(slot, ys, x, mod_l)


def _final_norm_kernel(x_ref, w_ref, o_ref):
    x = x_ref[...]
    o_ref[...] = x * lax.rsqrt(jnp.mean(x * x, axis=-1, keepdims=True) + EPS) * w_ref[...]


def _final_norm(x, w):
    return pl.pallas_call(
        _final_norm_kernel,
        grid=(N_TILES,),
        in_specs=[pl.BlockSpec((TILE, D_MODEL), lambda i: (i, 0)), pl.BlockSpec((1, D_MODEL), lambda i: (0, 0))],
        out_specs=pl.BlockSpec((TILE, D_MODEL), lambda i: (i, 0)),
        out_shape=jax.ShapeDtypeStruct(x.shape, F32),
        compiler_params=_cparams(("parallel",)),
        name="final_norm",
    )(x, w)


def _rope_tables():
    rows = DEC_SEQ // GRID_W
    row = jnp.repeat(jnp.arange(rows, dtype=F32), GRID_W)
    col = jnp.tile(jnp.arange(GRID_W, dtype=F32), rows)
    n_freq = HEAD_DIM // 4
    inv_freq = ROPE_THETA ** (-jnp.arange(n_freq, dtype=F32) / n_freq)
    ang = jnp.concatenate([row[:, None] * inv_freq, col[:, None] * inv_freq], axis=-1)
    cos, sin = jnp.cos(ang), jnp.sin(ang)
    cos_t = jnp.tile(jnp.concatenate([cos, cos], -1), (1, 128 // HEAD_DIM))
    sin_t = jnp.tile(jnp.concatenate([-sin, sin], -1), (1, 128 // HEAD_DIM))
    cos_t = jnp.concatenate([jnp.ones((TILE, 128), F32), cos_t], axis=0)
    sin_t = jnp.concatenate([jnp.zeros((TILE, 128), F32), sin_t], axis=0)
    return cos_t, sin_t


def _s5_initial_rows(state_ssm):
    st = state_ssm.astype(F32)
    re, im = st[..., 0], st[..., 1]
    both = jnp.stack([jnp.concatenate([re, im], -1), jnp.concatenate([im, re], -1)], axis=3)
    return both.transpose(1, 0, 2, 3, 4, 5).reshape(DEPTH, 4 * DEC_BATCH, N_SSM_GROUPS * 128)


def kernel(x_prompt, x_sample, cache_k, cache_v, state_ssm, state_ret, c, c_ctx, w_mod, b_mod, norm1_w, norm2_w, w_in, w_out, qn_w, kn_w, ssm_lambda_re, ssm_lambda_im, ssm_b_re, ssm_b_im, ssm_c_re, ssm_c_im, ssm_log_dt, ssm_d, ssm_w_glu, ret_decay_logit, ret_norm_w, w_router, w_gate, w_up, w_down, final_norm_w):
    x = jnp.concatenate([x_prompt.reshape(N_CTX, D_MODEL), x_sample.reshape(N_LAT, D_MODEL)], axis=0)
    cond_t = jnp.zeros((D_MODEL, 8), F32).at[:, 0].set(c_ctx).at[:, 1:1 + DEC_BATCH].set(c.T)
    mod = _modulation(cond_t, w_mod, b_mod).reshape(DEPTH, 8, 1, 6 * D_MODEL)
    cos_t, sin_t = _rope_tables()
    zero_ret = jnp.zeros((BATCH, 2, RET_HEADS, HEAD_DIM, HEAD_DIM), F32)
    ctx_blocks = N_CTX // DEC_SEQ

    w_in_bf, w_out_bf, w_glu_bf = w_in.astype(BF16), w_out.astype(BF16), ssm_w_glu.astype(BF16)
    s5_toe, s5_inj, s5_ro, s5_a = jax.vmap(_s5_matrices)(ssm_lambda_re, ssm_lambda_im, ssm_b_re, ssm_b_im,
                                                         ssm_c_re, ssm_c_im, ssm_log_dt)
    s5_h0 = _s5_initial_rows(state_ssm)
    ret_dec, ret_mask, ret_cdec = jax.vmap(_retention_tables)(ret_decay_logit)
    wr = jnp.pad(w_router.astype(F32), ((0, 0), (0, 0), (0, 128 - N_EXPERTS)))
    wr_hi = wr.astype(BF16)
    wr_split = jnp.stack([wr_hi, (wr - wr_hi.astype(F32)).astype(BF16)], axis=1)
    qn_t, kn_t = jnp.tile(qn_w, (1, 2)), jnp.tile(kn_w, (1, 2))
    cache_kv = jnp.concatenate([cache_k.reshape(DEC_BATCH, DEPTH, PAST_LEN, KV_WIDTH),
                                cache_v.reshape(DEC_BATCH, DEPTH, PAST_LEN, KV_WIDTH)], axis=-1).astype(BF16)

    ks, vs, ss, rs = [], [], [], []
    for l in range(DEPTH):
        mod_l = mod[l]
        q, kv, ub, rest = _in_projection(x, mod_l, norm1_w[l].reshape(1, -1), w_in_bf[l],
                                     qn_t[l].reshape(1, -1), kn_t[l].reshape(1, -1), cos_t, sin_t)
        ks.append(rest[:N_CTX, C_K:C_K + KV_WIDTH].reshape(BATCH, SEQ, N_KV_HEADS, HEAD_DIM))
        vs.append(rest[:N_CTX, C_V:C_V + KV_WIDTH].reshape(BATCH, SEQ, N_KV_HEADS, HEAD_DIM))

        kv_ctx = kv[:N_CTX].reshape(BATCH, SEQ, 2 * KV_WIDTH)
        kv_lat = jnp.concatenate([kv[N_CTX:].reshape(DEC_BATCH, DEC_SEQ, 2 * KV_WIDTH), cache_kv[:, l]], axis=1)
        attn_ctx = _attention(q, kv_ctx, 0, BATCH, SEQ, SEQ)
        attn_lat = _attention(q, kv_lat, N_CTX // LAT_TQ, DEC_BATCH, DEC_SEQ, LAT_TQ)

        y_loc, states = _s5_local(ub.reshape(S5_ROWS, S5_CHUNK * SSM_WIDTH), s5_toe[l], s5_inj[l])
        hf, hb, fin_f, fin_b = _s5_scan(states, s5_a[l], s5_h0[l])
        ys = _s5_out(y_loc, hf, hb, s5_ro[l]).reshape(N_TOK, SSM_WIDTH)
        fin = jnp.stack([fin_f, fin_b], axis=1).reshape(BATCH, 2, N_SSM_GROUPS, 2, SSM_STATE)
        ss.append(fin.transpose(0, 1, 2, 4, 3))

        nw_ret = ret_norm_w[l].reshape(1, -1)
        ret_ctx, fin_ret = _retention(rest, ret_dec[l], ret_mask[l], ret_cdec[l], zero_ret, nw_ret, 0, BATCH, SEQ)
        ret_lat, _ = _retention(rest, ret_dec[l], ret_mask[l], ret_cdec[l], state_ret[:, l].astype(F32), nw_ret,
                                ctx_blocks, DEC_BATCH, DEC_SEQ)
        rs.append(fin_ret)

        x1, h2, aff_t = _out_projection(x, rest, ys, attn_ctx, attn_lat, ret_ctx, ret_lat, mod_l,
                                        ssm_d[l].reshape(1, -1), w_glu_bf[l], w_out_bf[l],
                                        norm2_w[l].reshape(1, -1), wr_split[l])

        slot_ctx, gate_ctx, _ = _route(aff_t, 0, 1, N_CTX, SEQ, CAP_CTX)
        slot_lat, gate_lat, offs_lat = _route(aff_t, N_CTX // DEC_SEQ, DEC_BATCH, DEC_SEQ, DEC_SEQ, CAP_LAT)
        xs_ctx, gs_ctx = _gather_ctx(slot_ctx, gate_ctx, h2)
        xs_lat, gs_lat = _gather_lat(offs_lat, slot_lat, gate_lat, h2)
        y_ctx, y_lat = _expert_ffn(xs_ctx, xs_lat, gs_ctx, gs_lat, w_gate, w_up, w_down, l)
        x = _scatter_ctx(slot_ctx, y_ctx, x1, mod_l)
        x = _scatter_lat(offs_lat, slot_lat, y_lat, x, mod_l)

    y = _final_norm(x, final_norm_w.reshape(1, -1))
    y_prompt = y[:N_CTX].reshape(BATCH, SEQ, D_MODEL)
    y_sample = y[N_CTX:].reshape(DEC_BATCH, DEC_SEQ, D_MODEL)
    return (y_prompt, y_sample, jnp.stack(ks, axis=1), jnp.stack(vs, axis=1),
            jnp.stack(ss, axis=1), jnp.stack(rs, axis=1))
```

```python
import functools

import jax
import jax.numpy as jnp
from jax import lax
from jax.experimental import pallas as pl
from jax.experimental.pallas import tpu as pltpu

F32 = jnp.float32
BF16 = jnp.bfloat16

D_MODEL = 1024
BATCH = 16
SEQ = 256
DEPTH = 4
DEC_BATCH = 2
DEC_SEQ = 4096
PAST_LEN = 256
GRID_W = 64
HEAD_DIM = 64
SSM_WIDTH = 256
SSM_GROUP = 16
N_SSM_GROUPS = 16
SSM_STATE = 64
ATT_WIDTH = 512
N_HEADS = 8
N_KV_HEADS = 2
KV_WIDTH = 128
RET_WIDTH = 256
RET_HEADS = 4
IN_WIDTH = 2048
RET_CHUNK = 128
N_EXPERTS = 16
EXPERT_FF = 1024
ROPE_THETA = 10000.0
EPS = 1e-6

N_CTX = BATCH * SEQ
N_LAT = DEC_BATCH * DEC_SEQ
N_TOK = N_CTX + N_LAT
TILE = 512
OUT_TILE = 256
LAT_TQ = 512
N_TILES = N_TOK // TILE
CTX_TILES = N_CTX // TILE
LAT_TILES_PER_REQ = DEC_SEQ // TILE
CAP_CTX = 2 * SEQ // N_EXPERTS
CAP_LAT = 2 * DEC_SEQ // N_EXPERTS
S5_CHUNK = 16
CTX_CHUNKS = SEQ // S5_CHUNK
LAT_CHUNKS = DEC_SEQ // S5_CHUNK
S5_ROWS_CTX = BATCH * CTX_CHUNKS
S5_ROWS = S5_ROWS_CTX + DEC_BATCH * LAT_CHUNKS
C_U, C_K, C_V, C_RQ, C_RK, C_RV, C_RG = 0, 256, 384, 512, 768, 1024, 1280
REST_WIDTH = 1536
ROUTE_ITERS = 48
OFFS_STEP = 128
GATHER_BLOCK = 256
SCATTER_WINDOW = 256
VMEM_LIMIT = 56 * 1024 * 1024

TN = (((0,), (0,)), ((), ()))
NT = (((1,), (1,)), ((), ()))


def _cparams(sem):
    return pltpu.CompilerParams(dimension_semantics=sem, vmem_limit_bytes=VMEM_LIMIT)


def _tile_mod_row(i):
    return jnp.where(i < CTX_TILES, 0, 1 + (i - CTX_TILES) // LAT_TILES_PER_REQ)


def _split_dot(v, m):
    hi = v.astype(BF16)
    lo = (v - hi.astype(F32)).astype(BF16)
    return (jnp.dot(hi, m, preferred_element_type=F32)
            + jnp.dot(lo, m, preferred_element_type=F32))


def _group_avg_matrix():
    r = lax.broadcasted_iota(jnp.int32, (128, 128), 0) // HEAD_DIM
    c = lax.broadcasted_iota(jnp.int32, (128, 128), 1) // HEAD_DIM
    return jnp.where(r == c, 1.0 / HEAD_DIM, 0.0).astype(BF16)


def _mod_kernel(ct_ref, w_ref, b_ref, o_ref):
    c = ct_ref[...]
    s = c * jax.nn.sigmoid(c)
    w = w_ref[...]
    rows = [jnp.sum(w * s[:, r:r + 1], axis=0, keepdims=True) for r in range(3)]
    rows.append(jnp.zeros((5, w.shape[1]), F32))
    o_ref[...] = jnp.concatenate(rows, axis=0) + b_ref[...]


def _modulation(cond_t, w_mod, b_mod):
    tn = 512
    n = 6 * D_MODEL
    return pl.pallas_call(
        _mod_kernel,
        grid=(DEPTH, n // tn),
        in_specs=[pl.BlockSpec((D_MODEL, 8), lambda l, j: (0, 0)),
                  pl.BlockSpec((None, D_MODEL, tn), lambda l, j: (l, 0, j)),
                  pl.BlockSpec((None, 1, tn), lambda l, j: (l, 0, j))],
        out_specs=pl.BlockSpec((None, 8, tn), lambda l, j: (l, 0, j)),
        out_shape=jax.ShapeDtypeStruct((DEPTH, 8, n), F32),
        compiler_params=_cparams(("arbitrary", "arbitrary")),
        name="modulation",
    )(cond_t, w_mod, b_mod.reshape(DEPTH, 1, n))


def _inproj_kernel(x_ref, mod_ref, nw_ref, w_ref, qn_ref, kn_ref, cos_ref, sin_ref, q_ref, kv_ref, ub_ref, rest_ref):
    x = x_ref[...]
    shift = mod_ref[:, 0:D_MODEL]
    scale = mod_ref[:, D_MODEL:2 * D_MODEL]
    y = x * lax.rsqrt(jnp.mean(x * x, axis=-1, keepdims=True) + EPS) * nw_ref[...]
    h = y * (1.0 + scale) + shift
    proj = jnp.dot(h.astype(BF16), w_ref[...], preferred_element_type=F32)

    avg = _group_avg_matrix()
    cos = cos_ref[...]
    sin = sin_ref[...]
    first_half = (lax.broadcasted_iota(jnp.int32, (TILE, 128), 1) % HEAD_DIM) < (HEAD_DIM // 2)

    def head_norm(z, wrow):
        return z * lax.rsqrt(_split_dot(z * z, avg) + EPS) * wrow

    def rope(z):
        partner = jnp.where(first_half, pltpu.roll(z, 128 - HEAD_DIM // 2, 1), pltpu.roll(z, HEAD_DIM // 2, 1))
        return z * cos + partner * sin

    def col(off, j):
        return proj[:, off + 128 * j: off + 128 * (j + 1)]

    qn = qn_ref[...]
    for j in range(ATT_WIDTH // 128):
        z = rope(head_norm(col(SSM_WIDTH, j), qn)) * (HEAD_DIM ** -0.5)
        q_ref[:, 128 * j:128 * (j + 1)] = z.astype(q_ref.dtype)
    p_k = SSM_WIDTH + ATT_WIDTH
    rest_ref[:, C_U:C_U + SSM_WIDTH] = proj[:, 0:SSM_WIDTH]
    ub_ref[...] = proj[:, 0:SSM_WIDTH].astype(ub_ref.dtype)
    k = rope(head_norm(col(p_k, 0), kn_ref[...]))
    v = col(p_k + KV_WIDTH, 0)
    rest_ref[:, C_K:C_K + KV_WIDTH] = k
    rest_ref[:, C_V:C_V + KV_WIDTH] = v
    kv_ref[:, 0:KV_WIDTH] = k.astype(kv_ref.dtype)
    kv_ref[:, KV_WIDTH:2 * KV_WIDTH] = v.astype(kv_ref.dtype)
    p_r = p_k + 2 * KV_WIDTH
    for j in range(RET_WIDTH // 128):
        rest_ref[:, C_RQ + 128 * j:C_RQ + 128 * (j + 1)] = rope(col(p_r, j))
        rest_ref[:, C_RK + 128 * j:C_RK + 128 * (j + 1)] = rope(col(p_r + RET_WIDTH, j)) * (HEAD_DIM ** -0.5)
    rest_ref[:, C_RV:C_RV + 2 * RET_WIDTH] = proj[:, p_r + 2 * RET_WIDTH:p_r + 4 * RET_WIDTH]


def _in_projection(x, mod_l, nw, w_in_bf, qn, kn, cos_t, sin_t):
    def rope_blk(i):
        return (jnp.where(i < CTX_TILES, 0, 1 + (i - CTX_TILES) % LAT_TILES_PER_REQ), 0)
    return pl.pallas_call(
        _inproj_kernel,
        grid=(N_TILES,),
        in_specs=[pl.BlockSpec((TILE, D_MODEL), lambda i: (i, 0)),
                  pl.BlockSpec((None, 1, 6 * D_MODEL), lambda i: (_tile_mod_row(i), 0, 0)),
                  pl.BlockSpec((1, D_MODEL), lambda i: (0, 0)),
                  pl.BlockSpec((D_MODEL, IN_WIDTH), lambda i: (0, 0)),
                  pl.BlockSpec((1, 128), lambda i: (0, 0)),
                  pl.BlockSpec((1, 128), lambda i: (0, 0)),
                  pl.BlockSpec((TILE, 128), rope_blk),
                  pl.BlockSpec((TILE, 128), rope_blk)],
        out_specs=[pl.BlockSpec((TILE, ATT_WIDTH), lambda i: (i, 0)),
                   pl.BlockSpec((TILE, 2 * KV_WIDTH), lambda i: (i, 0)),
                   pl.BlockSpec((TILE, SSM_WIDTH), lambda i: (i, 0)),
                   pl.BlockSpec((TILE, REST_WIDTH), lambda i: (i, 0))],
        out_shape=[jax.ShapeDtypeStruct((N_TOK, ATT_WIDTH), BF16),
                   jax.ShapeDtypeStruct((N_TOK, 2 * KV_WIDTH), BF16),
                   jax.ShapeDtypeStruct((N_TOK, SSM_WIDTH), BF16),
                   jax.ShapeDtypeStruct((N_TOK, REST_WIDTH), F32)],
        compiler_params=_cparams(("parallel",)),
        name="in_projection",
    )(x, mod_l, nw, w_in_bf, qn, kn, cos_t, sin_t)


def _attn_kernel(q_ref, k_ref, v_ref, o_ref):
    group = N_HEADS // N_KV_HEADS
    for kv in range(N_KV_HEADS):
        k = k_ref[:, kv * HEAD_DIM:(kv + 1) * HEAD_DIM]
        v = v_ref[:, kv * HEAD_DIM:(kv + 1) * HEAD_DIM]
        for g in range(group):
            cols = slice((kv * group + g) * HEAD_DIM, (kv * group + g + 1) * HEAD_DIM)
            s = lax.dot_general(q_ref[:, cols], k, NT, preferred_element_type=F32)
            m = jnp.max(s, axis=-1, keepdims=True)
            p = jnp.exp(s - m)
            l = jnp.sum(p, axis=-1, keepdims=True)
            o = jnp.dot(p.astype(BF16), v, preferred_element_type=F32) / l
            o_ref[:, cols] = o.astype(o_ref.dtype)


def _attention(q, kv, row_block0, n_req, lq, tq):
    lk = kv.shape[1]
    nq = lq // tq
    return pl.pallas_call(
        _attn_kernel,
        grid=(n_req, nq),
        in_specs=[pl.BlockSpec((tq, ATT_WIDTH), lambda r, i: (row_block0 + r * nq + i, 0)),
                  pl.BlockSpec((None, lk, KV_WIDTH), lambda r, i: (r, 0, 0)),
                  pl.BlockSpec((None, lk, KV_WIDTH), lambda r, i: (r, 0, 1))],
        out_specs=pl.BlockSpec((tq, ATT_WIDTH), lambda r, i: (r * nq + i, 0)),
        out_shape=jax.ShapeDtypeStruct((n_req * lq, ATT_WIDTH), BF16),
        compiler_params=_cparams(("parallel", "parallel")),
        name="attention",
    )(q, kv, kv)


def _s5_local_kernel(u_ref, t_ref, b_ref, y_ref, s_ref):
    u = u_ref[...]
    y_ref[...] = jnp.dot(u, t_ref[...], preferred_element_type=F32)
    s_ref[...] = jnp.dot(u, b_ref[...], preferred_element_type=F32)


def _s5_local(u2, toeplitz, inject, layer):
    tn = 256
    n = S5_CHUNK * SSM_WIDTH
    wspec = pl.BlockSpec((None, n, tn), lambda i: (layer, 0, i))
    ospec = pl.BlockSpec((S5_ROWS, tn), lambda i: (0, i))
    return pl.pallas_call(
        _s5_local_kernel,
        grid=(n // tn,),
        in_specs=[pl.BlockSpec((S5_ROWS, n), lambda i: (0, 0)), wspec, wspec],
        out_specs=[ospec, ospec],
        out_shape=[jax.ShapeDtypeStruct((S5_ROWS, n), F32)] * 2,
        compiler_params=_cparams(("parallel",)),
        name="s5_local",
    )(u2, toeplitz, inject)


def _s5_scan_kernel(sf_ref, sb_ref, a_ref, h0_ref, hf_ref, hb_ref, ff_ref, fb_ref, sfs_ref, sbs_ref):
    a1f, a2f, a1b, a2b = a_ref[0:1, :], a_ref[1:2, :], a_ref[2:3, :], a_ref[3:4, :]
    w = sf_ref.shape[1]
    for c in range(w // 128):
        cols = slice(128 * c, 128 * (c + 1))
        sfs_ref[:, cols] = pltpu.roll(sf_ref[:, cols], SSM_STATE, 1)
        sbs_ref[:, cols] = pltpu.roll(sb_ref[:, cols], SSM_STATE, 1)

    zero = jnp.zeros((1, w), F32)
    for r in range(BATCH):
        hf, hfs, hb, hbs = zero, zero, zero, zero
        for j in range(CTX_CHUNKS):
            row = r * CTX_CHUNKS + j
            hf_ref[row:row + 1, :] = hf
            hf, hfs = (a1f * hf + a2f * hfs + sf_ref[row:row + 1, :], a1f * hfs - a2f * hf + sfs_ref[row:row + 1, :])
            row = r * CTX_CHUNKS + CTX_CHUNKS - 1 - j
            hb_ref[row:row + 1, :] = hb
            hb, hbs = (a1b * hb + a2b * hbs + sb_ref[row:row + 1, :], a1b * hbs - a2b * hb + sbs_ref[row:row + 1, :])
        ff_ref[r:r + 1, :] = hf
        fb_ref[r:r + 1, :] = hb

    def body(j, carry):
        out = []
        for r in range(DEC_BATCH):
            hf, hfs, hb, hbs = carry[4 * r:4 * r + 4]
            row = pl.ds(S5_ROWS_CTX + r * LAT_CHUNKS + j, 1)
            hf_ref[row, :] = hf
            nf = a1f * hf + a2f * hfs + sf_ref[row, :]
            nfs = a1f * hfs - a2f * hf + sfs_ref[row, :]
            row = pl.ds(S5_ROWS_CTX + r * LAT_CHUNKS + LAT_CHUNKS - 1 - j, 1)
            hb_ref[row, :] = hb
            nb = a1b * hb + a2b * hbs + sb_ref[row, :]
            nbs = a1b * hbs - a2b * hb + sbs_ref[row, :]
            out += [nf, nfs, nb, nbs]
        return tuple(out)

    init = tuple(h0_ref[i:i + 1, :] for i in range(4 * DEC_BATCH))
    lax.fori_loop(0, LAT_CHUNKS, body, init)


def _s5_scan(states, a_rows, h0_rows):
    w = 512
    full = N_SSM_GROUPS * 128
    nb = full // w
    sspec = pl.BlockSpec((S5_ROWS, w), lambda i: (0, i))
    fspec = pl.BlockSpec((BATCH, w), lambda i: (0, i))
    return pl.pallas_call(
        _s5_scan_kernel,
        grid=(nb,),
        in_specs=[sspec,
                  pl.BlockSpec((S5_ROWS, w), lambda i: (0, nb + i)),
                  pl.BlockSpec((8, w), lambda i: (0, i)),
                  pl.BlockSpec((8, w), lambda i: (0, i))],
        out_specs=[sspec, sspec, fspec, fspec],
        out_shape=[jax.ShapeDtypeStruct((S5_ROWS, full), F32)] * 2 + [jax.ShapeDtypeStruct((BATCH, full), F32)] * 2,
        scratch_shapes=[pltpu.VMEM((S5_ROWS, w), F32), pltpu.VMEM((S5_ROWS, w), F32)],
        compiler_params=_cparams(("parallel",)),
        name="s5_scan",
    )(states, states, a_rows, h0_rows)


def _s5_out_kernel(y_ref, hf_ref, hb_ref, c_ref, o_ref, h_ref):
    full = N_SSM_GROUPS * 128

    @pl.when(pl.program_id(0) == 0)
    def _():
        h_ref[:, 0:full] = hf_ref[...].astype(BF16)
        h_ref[:, full:2 * full] = hb_ref[...].astype(BF16)

    o_ref[...] = y_ref[...] + jnp.dot(h_ref[...], c_ref[...], preferred_element_type=F32)


def _s5_out(y_loc, hf, hb, readout, layer):
    tn = 256
    n = S5_CHUNK * SSM_WIDTH
    full = N_SSM_GROUPS * 128
    hspec = pl.BlockSpec((S5_ROWS, full), lambda i: (0, 0))
    return pl.pallas_call(
        _s5_out_kernel,
        grid=(n // tn,),
        in_specs=[pl.BlockSpec((S5_ROWS, tn), lambda i: (0, i)), hspec, hspec,
                  pl.BlockSpec((None, 2 * full, tn), lambda i: (layer, 0, i))],
        out_specs=pl.BlockSpec((S5_ROWS, tn), lambda i: (0, i)),
        out_shape=jax.ShapeDtypeStruct((S5_ROWS, n), F32),
        scratch_shapes=[pltpu.VMEM((S5_ROWS, 2 * full), BF16)],
        compiler_params=_cparams(("arbitrary",)),
        name="s5_out",
    )(y_loc, hf, hb, readout)


def _s5_matrices(lam_re, lam_im, b_re, b_im, c_re, c_im, log_dt):
    hp = lax.Precision.HIGHEST
    n = S5_CHUNK
    tau = jnp.arange(n + 1, dtype=F32)
    toes, injs, ros, a_rows = [], [], [], []
    for di in range(2):
        lr, li = lam_re[di].astype(F32), lam_im[di].astype(F32)
        dt = jnp.exp(log_dt[di].astype(F32))[:, None]
        mag = jnp.exp(lr * dt * tau[:, None, None])
        ang = li * dt * tau[:, None, None]
        e_re, e_im = mag * jnp.cos(ang), mag * jnp.sin(ang)
        nr, ni = e_re[1] - 1.0, e_im[1]
        den = lr * lr + li * li
        f_re, f_im = (nr * lr + ni * li) / den, (ni * lr - nr * li) / den
        br, bi = b_re[di].astype(F32), b_im[di].astype(F32)
        bb_re = f_re[..., None] * br - f_im[..., None] * bi
        bb_im = f_re[..., None] * bi + f_im[..., None] * br
        cr, ci = c_re[di].astype(F32), c_im[di].astype(F32)
        ce_re = cr[None] * e_re[:n, :, None, :] - ci[None] * e_im[:n, :, None, :]
        ce_im = cr[None] * e_im[:n, :, None, :] + ci[None] * e_re[:n, :, None, :]
        kern = (jnp.einsum('tgcp,gpd->gdtc', ce_re, bb_re, precision=hp)
                - jnp.einsum('tgcp,gpd->gdtc', ce_im, bb_im, precision=hp))
        toes.append(kern.reshape(N_SSM_GROUPS * SSM_GROUP, n * SSM_GROUP))
        pe_re, pe_im = (e_re[:n][::-1], e_im[:n][::-1]) if di == 0 else (e_re[:n], e_im[:n])
        inj_re = pe_re[..., None] * bb_re[None] - pe_im[..., None] * bb_im[None]
        inj_im = pe_re[..., None] * bb_im[None] + pe_im[..., None] * bb_re[None]
        injs.append(jnp.concatenate([inj_re.transpose(0, 1, 3, 2), inj_im.transpose(0, 1, 3, 2)], -1))
        qe_re, qe_im = (e_re[1:], e_im[1:]) if di == 0 else (e_re[1:][::-1], e_im[1:][::-1])
        ro_re = cr[None] * qe_re[:, :, None, :] - ci[None] * qe_im[:, :, None, :]
        ro_im = cr[None] * qe_im[:, :, None, :] + ci[None] * qe_re[:, :, None, :]
        ros.append(jnp.concatenate([ro_re.transpose(1, 3, 0, 2), -ro_im.transpose(1, 3, 0, 2)], axis=1))
        a_re, a_im = e_re[n], e_im[n]
        a_rows.append(jnp.concatenate([a_re, a_re], -1).reshape(1, -1))
        a_rows.append(jnp.concatenate([-a_im, a_im], -1).reshape(1, -1))
    size = n * SSM_WIDTH
    g = N_SSM_GROUPS
    row = jnp.arange(size)[:, None]
    col = jnp.arange(size)[None, :]
    src = jnp.arange(n * SSM_GROUP)[:, None]

    def spread(table, copy, keep):
        wide = jnp.dot(table.astype(BF16), copy.astype(BF16), preferred_element_type=BF16)
        return jnp.where(keep, wide, jnp.zeros((), BF16))

    copy_tc = (src // SSM_GROUP == col // SSM_WIDTH) & (src % SSM_GROUP == col % SSM_GROUP)
    copy_dp = (src // 128 == col // (g * 128)) & (src % 128 == col % 128)
    lag_c = jnp.arange(2 * n * SSM_GROUP)[None, :, None]
    dst_c = jnp.arange(n * SSM_GROUP)[None, None, :]
    s_idx = jnp.arange(n)[:, None, None]
    lag_t = (lag_c % (n * SSM_GROUP)) // SSM_GROUP
    dst_t = dst_c // SSM_GROUP
    want = jnp.where(lag_c < n * SSM_GROUP, dst_t - s_idx, s_idx - dst_t)
    shift = ((lag_c % SSM_GROUP == dst_c % SSM_GROUP) & (lag_t == want)).astype(F32)
    toe = jnp.einsum('rk,skn->srn', jnp.concatenate(toes, axis=1), shift, precision=hp)
    toe = toe.reshape(size, n * SSM_GROUP)
    toeplitz = spread(toe, copy_tc, (row // SSM_GROUP) % g == (col // SSM_GROUP) % g)
    inj = jnp.stack(injs, axis=3).reshape(size, 2 * 128)
    inject = spread(inj, copy_dp, (row // SSM_GROUP) % g == (col // 128) % g)
    ro = jnp.stack(ros, axis=0).reshape(size, n * SSM_GROUP)
    readout = spread(ro, copy_tc, (row // 128) % g == (col // SSM_GROUP) % g)
    a_rows = jnp.concatenate(a_rows + [jnp.zeros((4, N_SSM_GROUPS * 128), F32)], axis=0)
    return toeplitz, inject, readout, a_rows


def _ret_kernel(q_ref, k_ref, v_ref, g_ref, dec_ref, mask_ref, cd_ref, s0_ref, nw_ref, o_ref, fin_ref,
                kvf_ref, kvb_ref, *, n_chunks):
    hd = HEAD_DIM
    nh = 2

    def local_state(i, _):
        rows = pl.ds(pl.multiple_of(i * RET_CHUNK, RET_CHUNK), RET_CHUNK)
        k = k_ref[rows, :]
        v = v_ref[rows, :].astype(BF16)
        kf = (k * dec_ref[1]).astype(BF16)
        kb = (k * dec_ref[3]).astype(BF16)
        for h in range(nh):
            ls = slice(h * hd, (h + 1) * hd)
            kvf_ref[i, h] = lax.dot_general(kf[:, ls], v[:, ls], TN, preferred_element_type=F32)
            kvb_ref[i, h] = lax.dot_general(kb[:, ls], v[:, ls], TN, preferred_element_type=F32)
        return 0

    lax.fori_loop(0, n_chunks, local_state, 0, unroll=min(4, n_chunks))

    def scan_f(i, s):
        loc = kvf_ref[i]
        kvf_ref[i] = s
        return cd_ref[0] * s + loc

    def scan_b(i, s):
        j = n_chunks - 1 - i
        loc = kvb_ref[j]
        kvb_ref[j] = s
        return cd_ref[1] * s + loc

    fin_ref[0] = lax.fori_loop(0, n_chunks, scan_f, s0_ref[0], unroll=min(4, n_chunks))
    fin_ref[1] = lax.fori_loop(0, n_chunks, scan_b, s0_ref[1], unroll=min(4, n_chunks))

    avg = _group_avg_matrix()

    def outputs(i, _):
        rows = pl.ds(pl.multiple_of(i * RET_CHUNK, RET_CHUNK), RET_CHUNK)
        q = q_ref[rows, :]
        qb = q.astype(BF16)
        kb = k_ref[rows, :].astype(BF16)
        v = v_ref[rows, :].astype(BF16)
        qf = (q * dec_ref[0]).astype(BF16)
        qr = (q * dec_ref[2]).astype(BF16)
        outs = []
        for h in range(nh):
            ls = slice(h * hd, (h + 1) * hd)
            inner = lax.dot_general(qb[:, ls], kb[:, ls], NT, preferred_element_type=F32) * mask_ref[h]
            o = jnp.dot(inner.astype(BF16), v[:, ls], preferred_element_type=F32)
            o += jnp.dot(qf[:, ls], kvf_ref[i, h].astype(BF16), preferred_element_type=F32)
            o += jnp.dot(qr[:, ls], kvb_ref[i, h].astype(BF16), preferred_element_type=F32)
            outs.append(o)
        o = jnp.concatenate(outs, axis=1)
        d = o - _split_dot(o, avg)
        o = d * lax.rsqrt(_split_dot(d * d, avg) + EPS) * nw_ref[...]
        g = g_ref[rows, :]
        o_ref[rows, :] = (g * jax.nn.sigmoid(g) * o).astype(o_ref.dtype)
        return 0

    lax.fori_loop(0, n_chunks, outputs, 0, unroll=2)


def _retention(rest, dec, mask, cdec, s0, nw, row_block0, n_req, length):
    n_chunks = length // RET_CHUNK
    hp = RET_HEADS // 2

    def tok(cb):
        return pl.BlockSpec((length, 128), lambda r, p: (row_block0 + r, cb + p))

    state = pl.BlockSpec((None, 2, 2, HEAD_DIM, HEAD_DIM), lambda r, p: (r, 0, p, 0, 0))
    return pl.pallas_call(
        functools.partial(_ret_kernel, n_chunks=n_chunks),
        grid=(n_req, hp),
        in_specs=[tok(C_RQ // 128), tok(C_RK // 128), tok(C_RV // 128), tok(C_RG // 128),
                  pl.BlockSpec((4, RET_CHUNK, 128), lambda r, p: (0, 0, p)),
                  pl.BlockSpec((2, RET_CHUNK, RET_CHUNK), lambda r, p: (p, 0, 0)),
                  pl.BlockSpec((2, 2, HEAD_DIM, HEAD_DIM), lambda r, p: (0, p, 0, 0)),
                  state,
                  pl.BlockSpec((1, 128), lambda r, p: (0, p))],
        out_specs=[pl.BlockSpec((length, 128), lambda r, p: (r, p)), state],
        out_shape=[jax.ShapeDtypeStruct((n_req * length, RET_WIDTH), BF16),
                   jax.ShapeDtypeStruct((n_req, 2, RET_HEADS, HEAD_DIM, HEAD_DIM), F32)],
        scratch_shapes=[pltpu.VMEM((n_chunks, 2, HEAD_DIM, HEAD_DIM), F32),
                        pltpu.VMEM((n_chunks, 2, HEAD_DIM, HEAD_DIM), F32)],
        compiler_params=_cparams(("parallel", "parallel")),
        name="retention",
    )(rest, rest, rest, rest, dec, mask, cdec, s0, nw)


def _retention_tables(decay_logit):
    lg = jax.nn.log_sigmoid(decay_logit.astype(F32))
    idx = jnp.arange(RET_CHUNK, dtype=F32)
    rel = idx[:, None] - idx[None, :]
    d_f = jnp.where(rel >= 0, jnp.exp(lg[0][:, None, None] * jnp.maximum(rel, 0.0)), 0.0)
    d_b = jnp.where(rel <= 0, jnp.exp(lg[1][:, None, None] * jnp.maximum(-rel, 0.0)), 0.0)
    mask = d_f + d_b

    def lanes(t):
        return jnp.repeat(t.T, HEAD_DIM, axis=1)

    dec = jnp.stack([lanes(jnp.exp(lg[0][:, None] * (idx + 1.0))),
                     lanes(jnp.exp(lg[0][:, None] * (RET_CHUNK - 1.0 - idx))),
                     lanes(jnp.exp(lg[1][:, None] * (RET_CHUNK - idx))),
                     lanes(jnp.exp(lg[1][:, None] * idx))], axis=0)
    cdec = jnp.broadcast_to(jnp.exp(lg * RET_CHUNK)[:, :, None, None], (2, RET_HEADS, HEAD_DIM, HEAD_DIM))
    return dec, mask, cdec


def _outproj_kernel(x_ref, rest_ref, ys_ref, atc_ref, atl_ref, rtc_ref, rtl_ref, mod_ref, d_ref, wglu_ref, wout_ref,
                    nw_ref, wr_ref, x1_ref, h2_ref, aff_ref):
    y = ys_ref[...] + rest_ref[...] * d_ref[...]
    y = jax.nn.gelu(y)
    y = y * jax.nn.sigmoid(jnp.dot(y.astype(BF16), wglu_ref[...], preferred_element_type=F32))
    is_ctx = pl.program_id(0) < N_CTX // OUT_TILE
    attn = jnp.where(is_ctx, atc_ref[...], atl_ref[...])
    ret = jnp.where(is_ctx, rtc_ref[...], rtl_ref[...])
    mix = (jnp.dot(y.astype(BF16), wout_ref[0:SSM_WIDTH, :], preferred_element_type=F32)
           + jnp.dot(attn, wout_ref[SSM_WIDTH:SSM_WIDTH + ATT_WIDTH, :], preferred_element_type=F32)
           + jnp.dot(ret, wout_ref[SSM_WIDTH + ATT_WIDTH:, :], preferred_element_type=F32))
    gate1 = mod_ref[:, 2 * D_MODEL:3 * D_MODEL]
    shift2 = mod_ref[:, 3 * D_MODEL:4 * D_MODEL]
    scale2 = mod_ref[:, 4 * D_MODEL:5 * D_MODEL]
    x1 = x_ref[...] + gate1 * mix
    x1_ref[...] = x1
    h2 = (x1 * lax.rsqrt(jnp.mean(x1 * x1, axis=-1, keepdims=True) + EPS) * nw_ref[...]) * (1.0 + scale2) + shift2
    h2_ref[...] = h2.astype(h2_ref.dtype)
    hi = h2.astype(BF16)
    lo = (h2 - hi.astype(F32)).astype(BF16)
    logits = (jnp.dot(hi, wr_ref[0], preferred_element_type=F32)
              + jnp.dot(lo, wr_ref[0], preferred_element_type=F32)
              + jnp.dot(hi, wr_ref[1], preferred_element_type=F32))
    valid = lax.broadcasted_iota(jnp.int32, logits.shape, 1) < N_EXPERTS
    logits = jnp.where(valid, logits, -1e30)
    e = jnp.exp(logits - jnp.max(logits, axis=-1, keepdims=True))
    aff = e / jnp.sum(e, axis=-1, keepdims=True)
    aff_ref[...] = aff.T


def _out_projection(x, rest, ys, attn_ctx, attn_lat, ret_ctx, ret_lat, mod_l, d_row, wglu_bf, wout_bf, nw2, wr_split):
    tile = OUT_TILE
    ctx_tiles = N_CTX // tile
    ctx_blk = lambda i: (jnp.minimum(i, ctx_tiles - 1), 0)
    lat_blk = lambda i: (jnp.maximum(i - ctx_tiles, 0), 0)
    mod_row = lambda i: (jnp.where(i < ctx_tiles, 0, 1 + (i - ctx_tiles) // (DEC_SEQ // tile)), 0, 0)
    return pl.pallas_call(
        _outproj_kernel,
        grid=(N_TOK // tile,),
        in_specs=[pl.BlockSpec((tile, D_MODEL), lambda i: (i, 0)),
                  pl.BlockSpec((tile, SSM_WIDTH), lambda i: (i, C_U // SSM_WIDTH)),
                  pl.BlockSpec((tile, SSM_WIDTH), lambda i: (i, 0)),
                  pl.BlockSpec((tile, ATT_WIDTH), ctx_blk),
                  pl.BlockSpec((tile, ATT_WIDTH), lat_blk),
                  pl.BlockSpec((tile, RET_WIDTH), ctx_blk),
                  pl.BlockSpec((tile, RET_WIDTH), lat_blk),
                  pl.BlockSpec((None, 1, 6 * D_MODEL), mod_row),
                  pl.BlockSpec((1, SSM_WIDTH), lambda i: (0, 0)),
                  pl.BlockSpec((SSM_WIDTH, SSM_WIDTH), lambda i: (0, 0)),
                  pl.BlockSpec((D_MODEL, D_MODEL), lambda i: (0, 0)),
                  pl.BlockSpec((1, D_MODEL), lambda i: (0, 0)),
                  pl.BlockSpec((2, D_MODEL, 128), lambda i: (0, 0, 0))],
        out_specs=[pl.BlockSpec((tile, D_MODEL), lambda i: (i, 0)),
                   pl.BlockSpec((tile, D_MODEL), lambda i: (i, 0)),
                   pl.BlockSpec((128, tile), lambda i: (0, i))],
        out_shape=[jax.ShapeDtypeStruct((N_TOK, D_MODEL), F32),
                   jax.ShapeDtypeStruct((N_TOK, D_MODEL), BF16),
                   jax.ShapeDtypeStruct((128, N_TOK), F32)],
        compiler_params=_cparams(("parallel",)),
        name="out_projection",
    )(x, rest, ys, attn_ctx, attn_lat, ret_ctx, ret_lat, mod_l, d_row, wglu_bf, wout_bf, nw2, wr_split)


def _lane_cumsum(x01):
    rows, n = x01.shape
    r = lax.broadcasted_iota(jnp.int32, (256, 256), 0)
    c = lax.broadcasted_iota(jnp.int32, (256, 256), 1)
    tri = jnp.where(r <= c, 1.0, 0.0).astype(BF16)
    off = jnp.zeros((rows, 1), F32)
    parts = []
    for j in range(n // 256):
        cs = jnp.dot(x01[:, 256 * j:256 * (j + 1)].astype(BF16), tri, preferred_element_type=F32) + off
        parts.append(cs)
        off = cs[:, 255:256]
    return jnp.concatenate(parts, axis=1)


def _count(m):
    return jnp.sum(jnp.where(m, 1.0, 0.0), axis=1, keepdims=True)


def _route_kernel(aff_ref, slot_ref, gate_ref, offs_ref, *, cap, seg):
    n_seg = aff_ref.shape[1] // seg
    segs = [slice(i * seg, (i + 1) * seg) for i in range(n_seg)]
    tiny = float(jnp.finfo(jnp.float32).tiny)

    def step(_, bounds):
        out = []
        for i in range(n_seg):
            lo, hi = bounds[2 * i], bounds[2 * i + 1]
            mid = jnp.where(lo > 0.0, jnp.sqrt(lo) * jnp.sqrt(hi), jnp.maximum(hi * (2.0 ** -16), tiny))
            mid = jnp.minimum(jnp.maximum(mid, lo), hi)
            ok = _count(aff_ref[:, segs[i]] >= mid) >= cap
            out += [jnp.where(ok, mid, lo), jnp.where(ok, hi, mid)]
        return tuple(out)

    init = (jnp.zeros((N_EXPERTS, 1), F32), jnp.full((N_EXPERTS, 1), 2.0, F32)) * n_seg
    bounds = lax.fori_loop(0, ROUTE_ITERS, step, init)

    for i in range(n_seg):
        a = aff_ref[:, segs[i]]
        lo, hi = bounds[2 * i], bounds[2 * i + 1]
        above = a >= hi
        band = (a >= lo) & (a < hi)
        sel = above | (band & (_lane_cumsum(jnp.where(band, 1.0, 0.0)) <= cap - _count(above)))
        taken = _lane_cumsum(jnp.where(sel, 1.0, 0.0))
        slot_ref[:, segs[i]] = jnp.where(sel, taken - 1.0, -1.0).astype(jnp.int32)
        gate_ref[:, segs[i]] = jnp.where(sel, a, 0.0)
    if n_seg == 1:
        n_off = seg // OFFS_STEP
        before = [jnp.zeros((N_EXPERTS, 1), F32)] + [taken[:, OFFS_STEP * k - 1:OFFS_STEP * k] for k in range(1, n_off)]
        before.append(jnp.zeros((N_EXPERTS, 128 - n_off), F32))
        offs_ref[...] = jnp.concatenate(before, axis=1).astype(jnp.int32)
    else:
        offs_ref[...] = jnp.zeros(offs_ref.shape, jnp.int32)


def _route(aff_t, col_block0, n_blocks, width, seg, cap):
    return pl.pallas_call(
        functools.partial(_route_kernel, cap=cap, seg=seg),
        grid=(n_blocks,),
        in_specs=[pl.BlockSpec((N_EXPERTS, width), lambda i: (0, col_block0 + i))],
        out_specs=[pl.BlockSpec((N_EXPERTS, width), lambda i: (0, i))] * 2
                  + [pl.BlockSpec((N_EXPERTS, 128), lambda i: (0, i))],
        out_shape=[jax.ShapeDtypeStruct((N_EXPERTS, n_blocks * width), jnp.int32),
                   jax.ShapeDtypeStruct((N_EXPERTS, n_blocks * width), F32),
                   jax.ShapeDtypeStruct((N_EXPERTS, n_blocks * 128), jnp.int32)],
        compiler_params=_cparams(("parallel",)),
        name="route",
    )(aff_t)


def _gather_lat_kernel(offs_ref, slot_ref, gate_ref, h_ref, o_ref, g_ref, acc_ref, gacc_ref):
    r = pl.program_id(0)
    e = pl.program_id(1)
    win = GATHER_BLOCK + 16
    acc_ref[...] = jnp.zeros(acc_ref.shape, F32)
    gacc_ref[...] = jnp.zeros(gacc_ref.shape, F32)
    rows = lax.broadcasted_iota(jnp.int32, (win, GATHER_BLOCK), 0)
    for b in range(DEC_SEQ // GATHER_BLOCK):
        off = offs_ref[e, r * 128 + b * (GATHER_BLOCK // OFFS_STEP)]
        base = pl.multiple_of((off // 8) * 8, 8)
        toks = slice(b * GATHER_BLOCK, (b + 1) * GATHER_BLOCK)
        hit = (rows + base) == slot_ref[pl.ds(e, 1), toks]
        onehot = jnp.where(hit, 1.0, 0.0).astype(BF16)
        acc_ref[pl.ds(base, win), :] += jnp.dot(onehot, h_ref[toks, :], preferred_element_type=F32)
        gacc_ref[pl.ds(base, win), :] += jnp.sum(jnp.where(hit, gate_ref[pl.ds(e, 1), toks], 0.0), axis=1, keepdims=True)
    o_ref[...] = acc_ref[0:CAP_LAT, :].astype(o_ref.dtype)
    g_ref[...] = gacc_ref[0:CAP_LAT, :]


def _gather_lat(offs, slot, gate, h2):
    rows = pl.BlockSpec((N_EXPERTS, DEC_SEQ), lambda r, e, offs: (0, r))
    acc_rows = CAP_LAT + GATHER_BLOCK + 16
    return pl.pallas_call(
        _gather_lat_kernel,
        grid_spec=pltpu.PrefetchScalarGridSpec(
            num_scalar_prefetch=1,
            grid=(DEC_BATCH, N_EXPERTS),
            in_specs=[rows, rows,
                      pl.BlockSpec((DEC_SEQ, D_MODEL), lambda r, e, offs: (N_CTX // DEC_SEQ + r, 0))],
            out_specs=[pl.BlockSpec((None, CAP_LAT, D_MODEL), lambda r, e, offs: (e, r, 0)),
                       pl.BlockSpec((None, CAP_LAT, 1), lambda r, e, offs: (e, r, 0))],
            scratch_shapes=[pltpu.VMEM((acc_rows, D_MODEL), F32), pltpu.VMEM((acc_rows, 1), F32)]),
        out_shape=[jax.ShapeDtypeStruct((N_EXPERTS, DEC_BATCH * CAP_LAT, D_MODEL), BF16),
                   jax.ShapeDtypeStruct((N_EXPERTS, DEC_BATCH * CAP_LAT, 1), F32)],
        compiler_params=_cparams(("parallel", "arbitrary")),
        name="gather_lat",
    )(offs, slot, gate, h2)


def _ctx_onehot(slot):
    rows = lax.broadcasted_iota(jnp.int32, (CAP_CTX, SEQ), 0)
    hits = [rows == slot[e:e + 1, :] for e in range(N_EXPERTS)]
    onehot = jnp.concatenate([jnp.where(h, 1.0, 0.0) for h in hits], axis=0).astype(BF16)
    return onehot, hits


def _gather_ctx_kernel(slot_ref, gate_ref, h_ref, o_ref, g_ref):
    onehot, hits = _ctx_onehot(slot_ref[...])
    xs = jnp.dot(onehot, h_ref[...], preferred_element_type=F32).astype(o_ref.dtype)
    gate = gate_ref[...]
    for e in range(N_EXPERTS):
        o_ref[e] = xs[e * CAP_CTX:(e + 1) * CAP_CTX]
        g_ref[e] = jnp.sum(jnp.where(hits[e], gate[e:e + 1, :], 0.0), axis=1, keepdims=True)


def _gather_ctx(slot, gate, h2):
    rows = pl.BlockSpec((N_EXPERTS, SEQ), lambda r: (0, r))
    return pl.pallas_call(
        _gather_ctx_kernel,
        grid=(BATCH,),
        in_specs=[rows, rows, pl.BlockSpec((SEQ, D_MODEL), lambda r: (r, 0))],
        out_specs=[pl.BlockSpec((N_EXPERTS, CAP_CTX, D_MODEL), lambda r: (0, r, 0)),
                   pl.BlockSpec((N_EXPERTS, CAP_CTX, 1), lambda r: (0, r, 0))],
        out_shape=[jax.ShapeDtypeStruct((N_EXPERTS, BATCH * CAP_CTX, D_MODEL), BF16),
                   jax.ShapeDtypeStruct((N_EXPERTS, BATCH * CAP_CTX, 1), F32)],
        compiler_params=_cparams(("parallel",)),
        name="gather_ctx",
    )(slot, gate, h2)


def _ffn_kernel(xc_ref, xl_ref, gc_ref, gl_ref, wg_ref, wu_ref, wd_ref, yc_ref, yl_ref, accc_ref, accl_ref):
    f = pl.program_id(1)
    wg = wg_ref[...].astype(BF16)
    wu = wu_ref[...].astype(BF16)
    wd = wd_ref[...].astype(BF16)

    def part(x_ref, gate_ref, acc_ref, y_ref):
        x = x_ref[...]
        a = jnp.dot(x, wg, preferred_element_type=F32)
        up = jnp.dot(x, wu, preferred_element_type=F32)
        mid = (a * jax.nn.sigmoid(a) * up).astype(BF16)
        y = jnp.dot(mid, wd, preferred_element_type=F32)

        @pl.when(f == 0)
        def _():
            acc_ref[...] = y

        @pl.when(f > 0)
        def _():
            acc_ref[...] += y

        @pl.when(f == pl.num_programs(1) - 1)
        def _():
            y_ref[...] = (acc_ref[...] * gate_ref[...]).astype(y_ref.dtype)

    part(xc_ref, gc_ref, accc_ref, yc_ref)
    part(xl_ref, gl_ref, accl_ref, yl_ref)


def _expert_ffn(xs_ctx, xs_lat, gs_ctx, gs_lat, w_gate, w_up, w_down, layer):
    tf = 512
    nc, nl = xs_ctx.shape[1], xs_lat.shape[1]
    return pl.pallas_call(
        _ffn_kernel,
        grid=(N_EXPERTS, EXPERT_FF // tf),
        in_specs=[pl.BlockSpec((None, nc, D_MODEL), lambda e, f: (e, 0, 0)),
                  pl.BlockSpec((None, nl, D_MODEL), lambda e, f: (e, 0, 0)),
                  pl.BlockSpec((None, nc, 1), lambda e, f: (e, 0, 0)),
                  pl.BlockSpec((None, nl, 1), lambda e, f: (e, 0, 0)),
                  pl.BlockSpec((None, None, D_MODEL, tf), lambda e, f: (layer, e, 0, f)),
                  pl.BlockSpec((None, None, D_MODEL, tf), lambda e, f: (layer, e, 0, f)),
                  pl.BlockSpec((None, None, tf, D_MODEL), lambda e, f: (layer, e, f, 0))],
        out_specs=[pl.BlockSpec((None, nc, D_MODEL), lambda e, f: (e, 0, 0)),
                   pl.BlockSpec((None, nl, D_MODEL), lambda e, f: (e, 0, 0))],
        out_shape=[jax.ShapeDtypeStruct(xs_ctx.shape, BF16), jax.ShapeDtypeStruct(xs_lat.shape, BF16)],
        scratch_shapes=[pltpu.VMEM((nc, D_MODEL), F32), pltpu.VMEM((nl, D_MODEL), F32)],
        compiler_params=_cparams(("parallel", "arbitrary")),
        name="expert_ffn",
    )(xs_ctx, xs_lat, gs_ctx, gs_lat, w_gate, w_up, w_down)


def _scatter_lat_kernel(offs_ref, slot_ref, y_ref, x_ref, mod_ref, o_ref):
    r = pl.program_id(0)
    t = pl.program_id(1)
    slot_t = slot_ref[...].astype(F32).T
    lane = lax.broadcasted_iota(jnp.int32, (OFFS_STEP, SCATTER_WINDOW), 1)
    ffn = jnp.zeros((OFFS_STEP, D_MODEL), F32)
    for e in range(N_EXPERTS):
        off = offs_ref[e, r * 128 + t]
        base = pl.multiple_of(jnp.minimum((off // 16) * 16, CAP_LAT - SCATTER_WINDOW), 16)
        onehot = jnp.where((lane + base).astype(F32) == slot_t[:, e:e + 1], 1.0, 0.0).astype(BF16)
        ffn += jnp.dot(onehot, y_ref[e, pl.ds(base, SCATTER_WINDOW), :], preferred_element_type=F32)
    o_ref[...] = x_ref[...] + mod_ref[:, 5 * D_MODEL:6 * D_MODEL] * ffn


def _scatter_lat(offs, slot, ys, x, mod_l):
    tt = OFFS_STEP
    nt = DEC_SEQ // tt
    blk0 = N_CTX // tt
    tok = pl.BlockSpec((tt, D_MODEL), lambda r, t, offs: (blk0 + r * nt + t, 0))
    return pl.pallas_call(
        _scatter_lat_kernel,
        grid_spec=pltpu.PrefetchScalarGridSpec(
            num_scalar_prefetch=1,
            grid=(DEC_BATCH, nt),
            in_specs=[pl.BlockSpec((N_EXPERTS, tt), lambda r, t, offs: (0, r * nt + t)),
                      pl.BlockSpec((N_EXPERTS, CAP_LAT, D_MODEL), lambda r, t, offs: (0, r, 0)),
                      tok,
                      pl.BlockSpec((None, 1, 6 * D_MODEL), lambda r, t, offs: (1 + r, 0, 0))],
            out_specs=tok),
        out_shape=jax.ShapeDtypeStruct(x.shape, F32),
        input_output_aliases={3: 0},
        compiler_params=_cparams(("parallel", "parallel")),
        name="scatter_lat",
    )(offs, slot, ys, x, mod_l)


def _scatter_ctx_kernel(slot_ref, y_ref, x_ref, mod_ref, o_ref):
    onehot, _ = _ctx_onehot(slot_ref[...])
    y = jnp.concatenate([y_ref[e] for e in range(N_EXPERTS)], axis=0)
    ffn = lax.dot_general(onehot, y, TN, preferred_element_type=F32)
    o_ref[...] = x_ref[...] + mod_ref[:, 5 * D_MODEL:6 * D_MODEL] * ffn


def _scatter_ctx(slot, ys, x, mod_l):
    tok = pl.BlockSpec((SEQ, D_MODEL), lambda r: (r, 0))
    return pl.pallas_call(
        _scatter_ctx_kernel,
        grid=(BATCH,),
        in_specs=[pl.BlockSpec((N_EXPERTS, SEQ), lambda r: (0, r)),
                  pl.BlockSpec((N_EXPERTS, CAP_CTX, D_MODEL), lambda r: (0, r, 0)),
                  tok,
                  pl.BlockSpec((None, 1, 6 * D_MODEL), lambda r: (0, 0, 0))],
        out_specs=tok,
        out_shape=jax.ShapeDtypeStruct(x.shape, F32),
        input_output_aliases={2: 0},
        compiler_params=_cparams(("parallel",)),
        name="scatter_ctx",
    )(slot, ys, x, mod_l)


def _final_norm_kernel(x_ref, w_ref, o_ref):
    x = x_ref[...]
    o_ref[...] = x * lax.rsqrt(jnp.mean(x * x, axis=-1, keepdims=True) + EPS) * w_ref[...]


def _final_norm(x, w):
    return pl.pallas_call(
        _final_norm_kernel,
        grid=(N_TILES,),
        in_specs=[pl.BlockSpec((TILE, D_MODEL), lambda i: (i, 0)), pl.BlockSpec((1, D_MODEL), lambda i: (0, 0))],
        out_specs=pl.BlockSpec((TILE, D_MODEL), lambda i: (i, 0)),
        out_shape=jax.ShapeDtypeStruct(x.shape, F32),
        compiler_params=_cparams(("parallel",)),
        name="final_norm",
    )(x, w)


def _rope_tables():
    rows = DEC_SEQ // GRID_W
    row = jnp.repeat(jnp.arange(rows, dtype=F32), GRID_W)
    col = jnp.tile(jnp.arange(GRID_W, dtype=F32), rows)
    n_freq = HEAD_DIM // 4
    inv_freq = ROPE_THETA ** (-jnp.arange(n_freq, dtype=F32) / n_freq)
    ang = jnp.concatenate([row[:, None] * inv_freq, col[:, None] * inv_freq], axis=-1)
    cos, sin = jnp.cos(ang), jnp.sin(ang)
    cos_t = jnp.tile(jnp.concatenate([cos, cos], -1), (1, 128 // HEAD_DIM))
    sin_t = jnp.tile(jnp.concatenate([-sin, sin], -1), (1, 128 // HEAD_DIM))
    cos_t = jnp.concatenate([jnp.ones((TILE, 128), F32), cos_t], axis=0)
    sin_t = jnp.concatenate([jnp.zeros((TILE, 128), F32), sin_t], axis=0)
    return cos_t, sin_t


def _s5_initial_rows(state_ssm):
    st = state_ssm.astype(F32)
    re, im = st[..., 0], st[..., 1]
    both = jnp.stack([jnp.concatenate([re, im], -1), jnp.concatenate([im, re], -1)], axis=3)
    return both.transpose(1, 0, 2, 3, 4, 5).reshape(DEPTH, 4 * DEC_BATCH, N_SSM_GROUPS * 128)


def kernel(x_prompt, x_sample, cache_k, cache_v, state_ssm, state_ret, c, c_ctx, w_mod, b_mod, norm1_w, norm2_w, w_in, w_out, qn_w, kn_w, ssm_lambda_re, ssm_lambda_im, ssm_b_re, ssm_b_im, ssm_c_re, ssm_c_im, ssm_log_dt, ssm_d, ssm_w_glu, ret_decay_logit, ret_norm_w, w_router, w_gate, w_up, w_down, final_norm_w):
    x = jnp.concatenate([x_prompt.reshape(N_CTX, D_MODEL), x_sample.reshape(N_LAT, D_MODEL)], axis=0)
    cond_t = jnp.zeros((D_MODEL, 8), F32).at[:, 0].set(c_ctx).at[:, 1:1 + DEC_BATCH].set(c.T)
    mod = _modulation(cond_t, w_mod, b_mod).reshape(DEPTH, 8, 1, 6 * D_MODEL)
    cos_t, sin_t = _rope_tables()
    zero_ret = jnp.zeros((BATCH, 2, RET_HEADS, HEAD_DIM, HEAD_DIM), F32)
    ctx_blocks = N_CTX // DEC_SEQ

    w_in_bf, w_out_bf, w_glu_bf = w_in.astype(BF16), w_out.astype(BF16), ssm_w_glu.astype(BF16)
    s5_toe, s5_inj, s5_ro, s5_a = jax.vmap(_s5_matrices)(ssm_lambda_re, ssm_lambda_im, ssm_b_re, ssm_b_im,
                                                         ssm_c_re, ssm_c_im, ssm_log_dt)
    s5_h0 = _s5_initial_rows(state_ssm)
    ret_dec, ret_mask, ret_cdec = jax.vmap(_retention_tables)(ret_decay_logit)
    wr = jnp.pad(w_router.astype(F32), ((0, 0), (0, 0), (0, 128 - N_EXPERTS)))
    wr_hi = wr.astype(BF16)
    wr_split = jnp.stack([wr_hi, (wr - wr_hi.astype(F32)).astype(BF16)], axis=1)
    qn_t, kn_t = jnp.tile(qn_w, (1, 2)), jnp.tile(kn_w, (1, 2))
    cache_kv = jnp.concatenate([cache_k.reshape(DEC_BATCH, DEPTH, PAST_LEN, KV_WIDTH),
                                cache_v.reshape(DEC_BATCH, DEPTH, PAST_LEN, KV_WIDTH)], axis=-1).astype(BF16)

    ks, vs, ss, rs = [], [], [], []
    for l in range(DEPTH):
        mod_l = mod[l]
        q, kv, ub, rest = _in_projection(x, mod_l, norm1_w[l].reshape(1, -1), w_in_bf[l],
                                     qn_t[l].reshape(1, -1), kn_t[l].reshape(1, -1), cos_t, sin_t)
        ks.append(rest[:N_CTX, C_K:C_K + KV_WIDTH].reshape(BATCH, SEQ, N_KV_HEADS, HEAD_DIM))
        vs.append(rest[:N_CTX, C_V:C_V + KV_WIDTH].reshape(BATCH, SEQ, N_KV_HEADS, HEAD_DIM))

        kv_ctx = kv[:N_CTX].reshape(BATCH, SEQ, 2 * KV_WIDTH)
        kv_lat = jnp.concatenate([kv[N_CTX:].reshape(DEC_BATCH, DEC_SEQ, 2 * KV_WIDTH), cache_kv[:, l]], axis=1)
        attn_ctx = _attention(q, kv_ctx, 0, BATCH, SEQ, SEQ)
        attn_lat = _attention(q, kv_lat, N_CTX // LAT_TQ, DEC_BATCH, DEC_SEQ, LAT_TQ)

        y_loc, states = _s5_local(ub.reshape(S5_ROWS, S5_CHUNK * SSM_WIDTH), s5_toe, s5_inj, l)
        hf, hb, fin_f, fin_b = _s5_scan(states, s5_a[l], s5_h0[l])
        ys = _s5_out(y_loc, hf, hb, s5_ro, l).reshape(N_TOK, SSM_WIDTH)
        fin = jnp.stack([fin_f, fin_b], axis=1).reshape(BATCH, 2, N_SSM_GROUPS, 2, SSM_STATE)
        ss.append(fin.transpose(0, 1, 2, 4, 3))

        nw_ret = ret_norm_w[l].reshape(1, -1)
        ret_ctx, fin_ret = _retention(rest, ret_dec[l], ret_mask[l], ret_cdec[l], zero_ret, nw_ret, 0, BATCH, SEQ)
        ret_lat, _ = _retention(rest, ret_dec[l], ret_mask[l], ret_cdec[l], state_ret[:, l].astype(F32), nw_ret,
                                ctx_blocks, DEC_BATCH, DEC_SEQ)
        rs.append(fin_ret)

        x1, h2, aff_t = _out_projection(x, rest, ys, attn_ctx, attn_lat, ret_ctx, ret_lat, mod_l,
                                        ssm_d[l].reshape(1, -1), w_glu_bf[l], w_out_bf[l],
                                        norm2_w[l].reshape(1, -1), wr_split[l])

        slot_ctx, gate_ctx, _ = _route(aff_t, 0, 1, N_CTX, SEQ, CAP_CTX)
        slot_lat, gate_lat, offs_lat = _route(aff_t, N_CTX // DEC_SEQ, DEC_BATCH, DEC_SEQ, DEC_SEQ, CAP_LAT)
        xs_ctx, gs_ctx = _gather_ctx(slot_ctx, gate_ctx, h2)
        xs_lat, gs_lat = _gather_lat(offs_lat, slot_lat, gate_lat, h2)
        y_ctx, y_lat = _expert_ffn(xs_ctx, xs_lat, gs_ctx, gs_lat, w_gate, w_up, w_down, l)
        x = _scatter_ctx(slot_ctx, y_ctx, x1, mod_l)
        x = _scatter_lat(offs_lat, slot_lat, y_lat, x, mod_l)

    y = _final_norm(x, final_norm_w.reshape(1, -1))
    y_prompt = y[:N_CTX].reshape(BATCH, SEQ, D_MODEL)
    y_sample = y[N_CTX:].reshape(DEC_BATCH, DEC_SEQ, D_MODEL)
    return (y_prompt, y_sample, jnp.stack(ks, axis=1), jnp.stack(vs, axis=1),
            jnp.stack(ss, axis=1), jnp.stack(rs, axis=1))
```

```python
import functools

import jax
import jax.numpy as jnp
from jax import lax
from jax.experimental import pallas as pl
from jax.experimental.pallas import tpu as pltpu

F32 = jnp.float32
BF16 = jnp.bfloat16

D_MODEL = 1024
BATCH = 16
SEQ = 256
DEPTH = 4
DEC_BATCH = 2
DEC_SEQ = 4096
PAST_LEN = 256
GRID_W = 64
HEAD_DIM = 64
SSM_WIDTH = 256
SSM_GROUP = 16
N_SSM_GROUPS = 16
SSM_STATE = 64
ATT_WIDTH = 512
N_HEADS = 8
N_KV_HEADS = 2
KV_WIDTH = 128
RET_WIDTH = 256
RET_HEADS = 4
IN_WIDTH = 2048
RET_CHUNK = 128
N_EXPERTS = 16
EXPERT_FF = 1024
ROPE_THETA = 10000.0
EPS = 1e-6

N_CTX = BATCH * SEQ
N_LAT = DEC_BATCH * DEC_SEQ
N_TOK = N_CTX + N_LAT
TILE = 512
OUT_TILE = 256
LAT_TQ = 512
N_TILES = N_TOK // TILE
CTX_TILES = N_CTX // TILE
LAT_TILES_PER_REQ = DEC_SEQ // TILE
CAP_CTX = 2 * SEQ // N_EXPERTS
CAP_LAT = 2 * DEC_SEQ // N_EXPERTS
S5_CHUNK = 16
CTX_CHUNKS = SEQ // S5_CHUNK
LAT_CHUNKS = DEC_SEQ // S5_CHUNK
S5_ROWS_CTX = BATCH * CTX_CHUNKS
S5_ROWS = S5_ROWS_CTX + DEC_BATCH * LAT_CHUNKS
C_U, C_K, C_V, C_RQ, C_RK, C_RV, C_RG = 0, 256, 384, 512, 768, 1024, 1280
REST_WIDTH = 1536
ROUTE_ITERS = 48
OFFS_STEP = 128
GATHER_BLOCK = 256
SCATTER_WINDOW = 256
VMEM_LIMIT = 56 * 1024 * 1024

TN = (((0,), (0,)), ((), ()))
NT = (((1,), (1,)), ((), ()))


def _cparams(sem):
    return pltpu.CompilerParams(dimension_semantics=sem, vmem_limit_bytes=VMEM_LIMIT)


def _tile_mod_row(i):
    return jnp.where(i < CTX_TILES, 0, 1 + (i - CTX_TILES) // LAT_TILES_PER_REQ)


def _split_dot(v, m):
    hi = v.astype(BF16)
    lo = (v - hi.astype(F32)).astype(BF16)
    return (jnp.dot(hi, m, preferred_element_type=F32)
            + jnp.dot(lo, m, preferred_element_type=F32))


def _group_avg_matrix():
    r = lax.broadcasted_iota(jnp.int32, (128, 128), 0) // HEAD_DIM
    c = lax.broadcasted_iota(jnp.int32, (128, 128), 1) // HEAD_DIM
    return jnp.where(r == c, 1.0 / HEAD_DIM, 0.0).astype(BF16)


def _mod_kernel(ct_ref, w_ref, b_ref, o_ref):
    c = ct_ref[...]
    s = c * jax.nn.sigmoid(c)
    w = w_ref[...]
    rows = [jnp.sum(w * s[:, r:r + 1], axis=0, keepdims=True) for r in range(3)]
    rows.append(jnp.zeros((5, w.shape[1]), F32))
    o_ref[...] = jnp.concatenate(rows, axis=0) + b_ref[...]


def _modulation(cond_t, w_mod, b_mod):
    tn = 512
    n = 6 * D_MODEL
    return pl.pallas_call(
        _mod_kernel,
        grid=(DEPTH, n // tn),
        in_specs=[pl.BlockSpec((D_MODEL, 8), lambda l, j: (0, 0)),
                  pl.BlockSpec((None, D_MODEL, tn), lambda l, j: (l, 0, j)),
                  pl.BlockSpec((None, 1, tn), lambda l, j: (l, 0, j))],
        out_specs=pl.BlockSpec((None, 8, tn), lambda l, j: (l, 0, j)),
        out_shape=jax.ShapeDtypeStruct((DEPTH, 8, n), F32),
        compiler_params=_cparams(("arbitrary", "arbitrary")),
        name="modulation",
    )(cond_t, w_mod, b_mod.reshape(DEPTH, 1, n))


def _inproj_kernel(x_ref, mod_ref, nw_ref, w_ref, qn_ref, kn_ref, cos_ref, sin_ref, q_ref, kv_ref, ub_ref, rest_ref):
    x = x_ref[...]
    shift = mod_ref[:, 0:D_MODEL]
    scale = mod_ref[:, D_MODEL:2 * D_MODEL]
    y = x * lax.rsqrt(jnp.mean(x * x, axis=-1, keepdims=True) + EPS) * nw_ref[...]
    h = y * (1.0 + scale) + shift
    proj = jnp.dot(h.astype(BF16), w_ref[...], preferred_element_type=F32)

    avg = _group_avg_matrix()
    cos = cos_ref[...]
    sin = sin_ref[...]
    first_half = (lax.broadcasted_iota(jnp.int32, (TILE, 128), 1) % HEAD_DIM) < (HEAD_DIM // 2)

    def head_norm(z, wrow):
        return z * lax.rsqrt(_split_dot(z * z, avg) + EPS) * wrow

    def rope(z):
        partner = jnp.where(first_half, pltpu.roll(z, 128 - HEAD_DIM // 2, 1), pltpu.roll(z, HEAD_DIM // 2, 1))
        return z * cos + partner * sin

    def col(off, j):
        return proj[:, off + 128 * j: off + 128 * (j + 1)]

    qn = qn_ref[...]
    for j in range(ATT_WIDTH // 128):
        z = rope(head_norm(col(SSM_WIDTH, j), qn)) * (HEAD_DIM ** -0.5)
        q_ref[:, 128 * j:128 * (j + 1)] = z.astype(q_ref.dtype)
    p_k = SSM_WIDTH + ATT_WIDTH
    rest_ref[:, C_U:C_U + SSM_WIDTH] = proj[:, 0:SSM_WIDTH]
    ub_ref[...] = proj[:, 0:SSM_WIDTH].astype(ub_ref.dtype)
    k = rope(head_norm(col(p_k, 0), kn_ref[...]))
    v = col(p_k + KV_WIDTH, 0)
    rest_ref[:, C_K:C_K + KV_WIDTH] = k
    rest_ref[:, C_V:C_V + KV_WIDTH] = v
    kv_ref[:, 0:KV_WIDTH] = k.astype(kv_ref.dtype)
    kv_ref[:, KV_WIDTH:2 * KV_WIDTH] = v.astype(kv_ref.dtype)
    p_r = p_k + 2 * KV_WIDTH
    for j in range(RET_WIDTH // 128):
        rest_ref[:, C_RQ + 128 * j:C_RQ + 128 * (j + 1)] = rope(col(p_r, j))
        rest_ref[:, C_RK + 128 * j:C_RK + 128 * (j + 1)] = rope(col(p_r + RET_WIDTH, j)) * (HEAD_DIM ** -0.5)
    rest_ref[:, C_RV:C_RV + 2 * RET_WIDTH] = proj[:, p_r + 2 * RET_WIDTH:p_r + 4 * RET_WIDTH]


def _in_projection(x, mod_l, nw, w_in_bf, qn, kn, cos_t, sin_t):
    def rope_blk(i):
        return (jnp.where(i < CTX_TILES, 0, 1 + (i - CTX_TILES) % LAT_TILES_PER_REQ), 0)
    return pl.pallas_call(
        _inproj_kernel,
        grid=(N_TILES,),
        in_specs=[pl.BlockSpec((TILE, D_MODEL), lambda i: (i, 0)),
                  pl.BlockSpec((None, 1, 6 * D_MODEL), lambda i: (_tile_mod_row(i), 0, 0)),
                  pl.BlockSpec((1, D_MODEL), lambda i: (0, 0)),
                  pl.BlockSpec((D_MODEL, IN_WIDTH), lambda i: (0, 0)),
                  pl.BlockSpec((1, 128), lambda i: (0, 0)),
                  pl.BlockSpec((1, 128), lambda i: (0, 0)),
                  pl.BlockSpec((TILE, 128), rope_blk),
                  pl.BlockSpec((TILE, 128), rope_blk)],
        out_specs=[pl.BlockSpec((TILE, ATT_WIDTH), lambda i: (i, 0)),
                   pl.BlockSpec((TILE, 2 * KV_WIDTH), lambda i: (i, 0)),
                   pl.BlockSpec((TILE, SSM_WIDTH), lambda i: (i, 0)),
                   pl.BlockSpec((TILE, REST_WIDTH), lambda i: (i, 0))],
        out_shape=[jax.ShapeDtypeStruct((N_TOK, ATT_WIDTH), BF16),
                   jax.ShapeDtypeStruct((N_TOK, 2 * KV_WIDTH), BF16),
                   jax.ShapeDtypeStruct((N_TOK, SSM_WIDTH), BF16),
                   jax.ShapeDtypeStruct((N_TOK, REST_WIDTH), F32)],
        compiler_params=_cparams(("parallel",)),
        name="in_projection",
    )(x, mod_l, nw, w_in_bf, qn, kn, cos_t, sin_t)


def _attn_kernel(q_ref, k_ref, v_ref, o_ref):
    group = N_HEADS // N_KV_HEADS
    for kv in range(N_KV_HEADS):
        k = k_ref[:, kv * HEAD_DIM:(kv + 1) * HEAD_DIM]
        v = v_ref[:, kv * HEAD_DIM:(kv + 1) * HEAD_DIM]
        for g in range(group):
            cols = slice((kv * group + g) * HEAD_DIM, (kv * group + g + 1) * HEAD_DIM)
            s = lax.dot_general(q_ref[:, cols], k, NT, preferred_element_type=F32)
            m = jnp.max(s, axis=-1, keepdims=True)
            p = jnp.exp(s - m)
            l = jnp.sum(p, axis=-1, keepdims=True)
            o = jnp.dot(p.astype(BF16), v, preferred_element_type=F32) / l
            o_ref[:, cols] = o.astype(o_ref.dtype)


def _attention(q, kv, row_block0, n_req, lq, tq):
    lk = kv.shape[1]
    nq = lq // tq
    return pl.pallas_call(
        _attn_kernel,
        grid=(n_req, nq),
        in_specs=[pl.BlockSpec((tq, ATT_WIDTH), lambda r, i: (row_block0 + r * nq + i, 0)),
                  pl.BlockSpec((None, lk, KV_WIDTH), lambda r, i: (r, 0, 0)),
                  pl.BlockSpec((None, lk, KV_WIDTH), lambda r, i: (r, 0, 1))],
        out_specs=pl.BlockSpec((tq, ATT_WIDTH), lambda r, i: (r * nq + i, 0)),
        out_shape=jax.ShapeDtypeStruct((n_req * lq, ATT_WIDTH), BF16),
        compiler_params=_cparams(("parallel", "parallel")),
        name="attention",
    )(q, kv, kv)


def _s5_local_kernel(u_ref, t_ref, b_ref, y_ref, s_ref):
    u = u_ref[...]
    y_ref[...] = jnp.dot(u, t_ref[...], preferred_element_type=F32)
    s_ref[...] = jnp.dot(u, b_ref[...], preferred_element_type=F32)


def _s5_local(u2, toeplitz, inject, layer):
    tn = 256
    n = S5_CHUNK * SSM_WIDTH
    wspec = pl.BlockSpec((None, n, tn), lambda i: (layer, 0, i))
    ospec = pl.BlockSpec((S5_ROWS, tn), lambda i: (0, i))
    return pl.pallas_call(
        _s5_local_kernel,
        grid=(n // tn,),
        in_specs=[pl.BlockSpec((S5_ROWS, n), lambda i: (0, 0)), wspec, wspec],
        out_specs=[ospec, ospec],
        out_shape=[jax.ShapeDtypeStruct((S5_ROWS, n), F32)] * 2,
        compiler_params=_cparams(("parallel",)),
        name="s5_local",
    )(u2, toeplitz, inject)


def _s5_scan_kernel(sf_ref, sb_ref, a_ref, h0_ref, hf_ref, hb_ref, ff_ref, fb_ref, sfs_ref, sbs_ref):
    a1f, a2f, a1b, a2b = a_ref[0:1, :], a_ref[1:2, :], a_ref[2:3, :], a_ref[3:4, :]
    w = sf_ref.shape[1]
    for c in range(w // 128):
        cols = slice(128 * c, 128 * (c + 1))
        sfs_ref[:, cols] = pltpu.roll(sf_ref[:, cols], SSM_STATE, 1)
        sbs_ref[:, cols] = pltpu.roll(sb_ref[:, cols], SSM_STATE, 1)

    zero = jnp.zeros((1, w), F32)
    for r in range(BATCH):
        hf, hfs, hb, hbs = zero, zero, zero, zero
        for j in range(CTX_CHUNKS):
            row = r * CTX_CHUNKS + j
            hf_ref[row:row + 1, :] = hf
            hf, hfs = (a1f * hf + a2f * hfs + sf_ref[row:row + 1, :], a1f * hfs - a2f * hf + sfs_ref[row:row + 1, :])
            row = r * CTX_CHUNKS + CTX_CHUNKS - 1 - j
            hb_ref[row:row + 1, :] = hb
            hb, hbs = (a1b * hb + a2b * hbs + sb_ref[row:row + 1, :], a1b * hbs - a2b * hb + sbs_ref[row:row + 1, :])
        ff_ref[r:r + 1, :] = hf
        fb_ref[r:r + 1, :] = hb

    def body(j, carry):
        out = []
        for r in range(DEC_BATCH):
            hf, hfs, hb, hbs = carry[4 * r:4 * r + 4]
            row = pl.ds(S5_ROWS_CTX + r * LAT_CHUNKS + j, 1)
            hf_ref[row, :] = hf
            nf = a1f * hf + a2f * hfs + sf_ref[row, :]
            nfs = a1f * hfs - a2f * hf + sfs_ref[row, :]
            row = pl.ds(S5_ROWS_CTX + r * LAT_CHUNKS + LAT_CHUNKS - 1 - j, 1)
            hb_ref[row, :] = hb
            nb = a1b * hb + a2b * hbs + sb_ref[row, :]
            nbs = a1b * hbs - a2b * hb + sbs_ref[row, :]
            out += [nf, nfs, nb, nbs]
        return tuple(out)

    init = tuple(h0_ref[i:i + 1, :] for i in range(4 * DEC_BATCH))
    lax.fori_loop(0, LAT_CHUNKS, body, init)


def _s5_scan(states, a_rows, h0_rows):
    w = 512
    full = N_SSM_GROUPS * 128
    nb = full // w
    sspec = pl.BlockSpec((S5_ROWS, w), lambda i: (0, i))
    fspec = pl.BlockSpec((BATCH, w), lambda i: (0, i))
    return pl.pallas_call(
        _s5_scan_kernel,
        grid=(nb,),
        in_specs=[sspec,
                  pl.BlockSpec((S5_ROWS, w), lambda i: (0, nb + i)),
                  pl.BlockSpec((8, w), lambda i: (0, i)),
                  pl.BlockSpec((8, w), lambda i: (0, i))],
        out_specs=[sspec, sspec, fspec, fspec],
        out_shape=[jax.ShapeDtypeStruct((S5_ROWS, full), F32)] * 2 + [jax.ShapeDtypeStruct((BATCH, full), F32)] * 2,
        scratch_shapes=[pltpu.VMEM((S5_ROWS, w), F32), pltpu.VMEM((S5_ROWS, w), F32)],
        compiler_params=_cparams(("parallel",)),
        name="s5_scan",
    )(states, states, a_rows, h0_rows)


def _s5_out_kernel(y_ref, hf_ref, hb_ref, c_ref, o_ref, h_ref):
    full = N_SSM_GROUPS * 128

    @pl.when(pl.program_id(0) == 0)
    def _():
        h_ref[:, 0:full] = hf_ref[...].astype(BF16)
        h_ref[:, full:2 * full] = hb_ref[...].astype(BF16)

    o_ref[...] = y_ref[...] + jnp.dot(h_ref[...], c_ref[...], preferred_element_type=F32)


def _s5_out(y_loc, hf, hb, readout, layer):
    tn = 256
    n = S5_CHUNK * SSM_WIDTH
    full = N_SSM_GROUPS * 128
    hspec = pl.BlockSpec((S5_ROWS, full), lambda i: (0, 0))
    return pl.pallas_call(
        _s5_out_kernel,
        grid=(n // tn,),
        in_specs=[pl.BlockSpec((S5_ROWS, tn), lambda i: (0, i)), hspec, hspec,
                  pl.BlockSpec((None, 2 * full, tn), lambda i: (layer, 0, i))],
        out_specs=pl.BlockSpec((S5_ROWS, tn), lambda i: (0, i)),
        out_shape=jax.ShapeDtypeStruct((S5_ROWS, n), F32),
        scratch_shapes=[pltpu.VMEM((S5_ROWS, 2 * full), BF16)],
        compiler_params=_cparams(("arbitrary",)),
        name="s5_out",
    )(y_loc, hf, hb, readout)


def _s5_matrices(lam_re, lam_im, b_re, b_im, c_re, c_im, log_dt):
    hp = lax.Precision.HIGHEST
    n = S5_CHUNK
    tau = jnp.arange(n + 1, dtype=F32)
    toes, injs, ros, a_rows = [], [], [], []
    for di in range(2):
        lr, li = lam_re[di].astype(F32), lam_im[di].astype(F32)
        dt = jnp.exp(log_dt[di].astype(F32))[:, None]
        mag = jnp.exp(lr * dt * tau[:, None, None])
        ang = li * dt * tau[:, None, None]
        e_re, e_im = mag * jnp.cos(ang), mag * jnp.sin(ang)
        nr, ni = e_re[1] - 1.0, e_im[1]
        den = lr * lr + li * li
        f_re, f_im = (nr * lr + ni * li) / den, (ni * lr - nr * li) / den
        br, bi = b_re[di].astype(F32), b_im[di].astype(F32)
        bb_re = f_re[..., None] * br - f_im[..., None] * bi
        bb_im = f_re[..., None] * bi + f_im[..., None] * br
        cr, ci = c_re[di].astype(F32), c_im[di].astype(F32)
        ce_re = cr[None] * e_re[:n, :, None, :] - ci[None] * e_im[:n, :, None, :]
        ce_im = cr[None] * e_im[:n, :, None, :] + ci[None] * e_re[:n, :, None, :]
        kern = (jnp.einsum('tgcp,gpd->gdtc', ce_re, bb_re, precision=hp)
                - jnp.einsum('tgcp,gpd->gdtc', ce_im, bb_im, precision=hp))
        toes.append(kern.reshape(N_SSM_GROUPS * SSM_GROUP, n * SSM_GROUP))
        pe_re, pe_im = (e_re[:n][::-1], e_im[:n][::-1]) if di == 0 else (e_re[:n], e_im[:n])
        inj_re = pe_re[..., None] * bb_re[None] - pe_im[..., None] * bb_im[None]
        inj_im = pe_re[..., None] * bb_im[None] + pe_im[..., None] * bb_re[None]
        injs.append(jnp.concatenate([inj_re.transpose(0, 1, 3, 2), inj_im.transpose(0, 1, 3, 2)], -1))
        qe_re, qe_im = (e_re[1:], e_im[1:]) if di == 0 else (e_re[1:][::-1], e_im[1:][::-1])
        ro_re = cr[None] * qe_re[:, :, None, :] - ci[None] * qe_im[:, :, None, :]
        ro_im = cr[None] * qe_im[:, :, None, :] + ci[None] * qe_re[:, :, None, :]
        ros.append(jnp.concatenate([ro_re.transpose(1, 3, 0, 2), -ro_im.transpose(1, 3, 0, 2)], axis=1))
        a_re, a_im = e_re[n], e_im[n]
        a_rows.append(jnp.concatenate([a_re, a_re], -1).reshape(1, -1))
        a_rows.append(jnp.concatenate([-a_im, a_im], -1).reshape(1, -1))
    size = n * SSM_WIDTH
    g = N_SSM_GROUPS
    row = jnp.arange(size)[:, None]
    col = jnp.arange(size)[None, :]
    src = jnp.arange(n * SSM_GROUP)[:, None]

    def spread(table, copy, keep):
        wide = jnp.dot(table.astype(BF16), copy.astype(BF16), preferred_element_type=BF16)
        return jnp.where(keep, wide, jnp.zeros((), BF16))

    copy_tc = (src // SSM_GROUP == col // SSM_WIDTH) & (src % SSM_GROUP == col % SSM_GROUP)
    copy_dp = (src // 128 == col // (g * 128)) & (src % 128 == col % 128)
    lag_c = jnp.arange(2 * n * SSM_GROUP)[None, :, None]
    dst_c = jnp.arange(n * SSM_GROUP)[None, None, :]
    s_idx = jnp.arange(n)[:, None, None]
    lag_t = (lag_c % (n * SSM_GROUP)) // SSM_GROUP
    dst_t = dst_c // SSM_GROUP
    want = jnp.where(lag_c < n * SSM_GROUP, dst_t - s_idx, s_idx - dst_t)
    shift = ((lag_c % SSM_GROUP == dst_c % SSM_GROUP) & (lag_t == want)).astype(F32)
    toe = jnp.einsum('rk,skn->srn', jnp.concatenate(toes, axis=1), shift, precision=hp)
    toe = toe.reshape(size, n * SSM_GROUP)
    toeplitz = spread(toe, copy_tc, (row // SSM_GROUP) % g == (col // SSM_GROUP) % g)
    inj = jnp.stack(injs, axis=3).reshape(size, 2 * 128)
    inject = spread(inj, copy_dp, (row // SSM_GROUP) % g == (col // 128) % g)
    ro = jnp.stack(ros, axis=0).reshape(size, n * SSM_GROUP)
    readout = spread(ro, copy_tc, (row // 128) % g == (col // SSM_GROUP) % g)
    a_rows = jnp.concatenate(a_rows + [jnp.zeros((4, N_SSM_GROUPS * 128), F32)], axis=0)
    return toeplitz, inject, readout, a_rows


def _ret_kernel(q_ref, k_ref, v_ref, g_ref, dec_ref, mask_ref, cd_ref, s0_ref, nw_ref, o_ref, fin_ref,
                kvf_ref, kvb_ref, *, n_chunks):
    hd = HEAD_DIM
    nh = 2

    def local_state(i, _):
        rows = pl.ds(pl.multiple_of(i * RET_CHUNK, RET_CHUNK), RET_CHUNK)
        k = k_ref[rows, :]
        v = v_ref[rows, :].astype(BF16)
        kf = (k * dec_ref[1]).astype(BF16)
        kb = (k * dec_ref[3]).astype(BF16)
        for h in range(nh):
            ls = slice(h * hd, (h + 1) * hd)
            kvf_ref[i, h] = lax.dot_general(kf[:, ls], v[:, ls], TN, preferred_element_type=F32)
            kvb_ref[i, h] = lax.dot_general(kb[:, ls], v[:, ls], TN, preferred_element_type=F32)
        return 0

    lax.fori_loop(0, n_chunks, local_state, 0, unroll=min(4, n_chunks))

    def scan_f(i, s):
        loc = kvf_ref[i]
        kvf_ref[i] = s
        return cd_ref[0] * s + loc

    def scan_b(i, s):
        j = n_chunks - 1 - i
        loc = kvb_ref[j]
        kvb_ref[j] = s
        return cd_ref[1] * s + loc

    fin_ref[0] = lax.fori_loop(0, n_chunks, scan_f, s0_ref[0], unroll=min(4, n_chunks))
    fin_ref[1] = lax.fori_loop(0, n_chunks, scan_b, s0_ref[1], unroll=min(4, n_chunks))

    avg = _group_avg_matrix()

    def outputs(i, _):
        rows = pl.ds(pl.multiple_of(i * RET_CHUNK, RET_CHUNK), RET_CHUNK)
        q = q_ref[rows, :]
        qb = q.astype(BF16)
        kb = k_ref[rows, :].astype(BF16)
        v = v_ref[rows, :].astype(BF16)
        qf = (q * dec_ref[0]).astype(BF16)
        qr = (q * dec_ref[2]).astype(BF16)
        outs = []
        for h in range(nh):
            ls = slice(h * hd, (h + 1) * hd)
            inner = lax.dot_general(qb[:, ls], kb[:, ls], NT, preferred_element_type=F32) * mask_ref[h]
            o = jnp.dot(inner.astype(BF16), v[:, ls], preferred_element_type=F32)
            o += jnp.dot(qf[:, ls], kvf_ref[i, h].astype(BF16), preferred_element_type=F32)
            o += jnp.dot(qr[:, ls], kvb_ref[i, h].astype(BF16), preferred_element_type=F32)
            outs.append(o)
        o = jnp.concatenate(outs, axis=1)
        d = o - _split_dot(o, avg)
        o = d * lax.rsqrt(_split_dot(d * d, avg) + EPS) * nw_ref[...]
        g = g_ref[rows, :]
        o_ref[rows, :] = (g * jax.nn.sigmoid(g) * o).astype(o_ref.dtype)
        return 0

    lax.fori_loop(0, n_chunks, outputs, 0, unroll=min(4, n_chunks))


def _retention(rest, dec, mask, cdec, s0, nw, row_block0, n_req, length):
    n_chunks = length // RET_CHUNK
    hp = RET_HEADS // 2

    def tok(cb):
        return pl.BlockSpec((length, 128), lambda r, p: (row_block0 + r, cb + p))

    state = pl.BlockSpec((None, 2, 2, HEAD_DIM, HEAD_DIM), lambda r, p: (r, 0, p, 0, 0))
    return pl.pallas_call(
        functools.partial(_ret_kernel, n_chunks=n_chunks),
        grid=(n_req, hp),
        in_specs=[tok(C_RQ // 128), tok(C_RK // 128), tok(C_RV // 128), tok(C_RG // 128),
                  pl.BlockSpec((4, RET_CHUNK, 128), lambda r, p: (0, 0, p)),
                  pl.BlockSpec((2, RET_CHUNK, RET_CHUNK), lambda r, p: (p, 0, 0)),
                  pl.BlockSpec((2, 2, HEAD_DIM, HEAD_DIM), lambda r, p: (0, p, 0, 0)),
                  state,
                  pl.BlockSpec((1, 128), lambda r, p: (0, p))],
        out_specs=[pl.BlockSpec((length, 128), lambda r, p: (r, p)), state],
        out_shape=[jax.ShapeDtypeStruct((n_req * length, RET_WIDTH), BF16),
                   jax.ShapeDtypeStruct((n_req, 2, RET_HEADS, HEAD_DIM, HEAD_DIM), F32)],
        scratch_shapes=[pltpu.VMEM((n_chunks, 2, HEAD_DIM, HEAD_DIM), F32),
                        pltpu.VMEM((n_chunks, 2, HEAD_DIM, HEAD_DIM), F32)],
        compiler_params=_cparams(("parallel", "parallel")),
        name="retention",
    )(rest, rest, rest, rest, dec, mask, cdec, s0, nw)


def _retention_tables(decay_logit):
    lg = jax.nn.log_sigmoid(decay_logit.astype(F32))
    idx = jnp.arange(RET_CHUNK, dtype=F32)
    rel = idx[:, None] - idx[None, :]
    d_f = jnp.where(rel >= 0, jnp.exp(lg[0][:, None, None] * jnp.maximum(rel, 0.0)), 0.0)
    d_b = jnp.where(rel <= 0, jnp.exp(lg[1][:, None, None] * jnp.maximum(-rel, 0.0)), 0.0)
    mask = d_f + d_b

    def lanes(t):
        return jnp.repeat(t.T, HEAD_DIM, axis=1)

    dec = jnp.stack([lanes(jnp.exp(lg[0][:, None] * (idx + 1.0))),
                     lanes(jnp.exp(lg[0][:, None] * (RET_CHUNK - 1.0 - idx))),
                     lanes(jnp.exp(lg[1][:, None] * (RET_CHUNK - idx))),
                     lanes(jnp.exp(lg[1][:, None] * idx))], axis=0)
    cdec = jnp.broadcast_to(jnp.exp(lg * RET_CHUNK)[:, :, None, None], (2, RET_HEADS, HEAD_DIM, HEAD_DIM))
    return dec, mask, cdec


def _outproj_kernel(x_ref, rest_ref, ys_ref, atc_ref, atl_ref, rtc_ref, rtl_ref, mod_ref, d_ref, wglu_ref, wout_ref,
                    nw_ref, wr_ref, x1_ref, h2_ref, aff_ref):
    y = ys_ref[...] + rest_ref[...] * d_ref[...]
    y = jax.nn.gelu(y)
    y = y * jax.nn.sigmoid(jnp.dot(y.astype(BF16), wglu_ref[...], preferred_element_type=F32))
    is_ctx = pl.program_id(0) < N_CTX // OUT_TILE
    attn = jnp.where(is_ctx, atc_ref[...], atl_ref[...])
    ret = jnp.where(is_ctx, rtc_ref[...], rtl_ref[...])
    mix = (jnp.dot(y.astype(BF16), wout_ref[0:SSM_WIDTH, :], preferred_element_type=F32)
           + jnp.dot(attn, wout_ref[SSM_WIDTH:SSM_WIDTH + ATT_WIDTH, :], preferred_element_type=F32)
           + jnp.dot(ret, wout_ref[SSM_WIDTH + ATT_WIDTH:, :], preferred_element_type=F32))
    gate1 = mod_ref[:, 2 * D_MODEL:3 * D_MODEL]
    shift2 = mod_ref[:, 3 * D_MODEL:4 * D_MODEL]
    scale2 = mod_ref[:, 4 * D_MODEL:5 * D_MODEL]
    x1 = x_ref[...] + gate1 * mix
    x1_ref[...] = x1
    h2 = (x1 * lax.rsqrt(jnp.mean(x1 * x1, axis=-1, keepdims=True) + EPS) * nw_ref[...]) * (1.0 + scale2) + shift2
    h2_ref[...] = h2.astype(h2_ref.dtype)
    hi = h2.astype(BF16)
    lo = (h2 - hi.astype(F32)).astype(BF16)
    logits = (jnp.dot(hi, wr_ref[0], preferred_element_type=F32)
              + jnp.dot(lo, wr_ref[0], preferred_element_type=F32)
              + jnp.dot(hi, wr_ref[1], preferred_element_type=F32))
    valid = lax.broadcasted_iota(jnp.int32, logits.shape, 1) < N_EXPERTS
    logits = jnp.where(valid, logits, -1e30)
    e = jnp.exp(logits - jnp.max(logits, axis=-1, keepdims=True))
    aff = e / jnp.sum(e, axis=-1, keepdims=True)
    aff_ref[...] = aff.T


def _out_projection(x, rest, ys, attn_ctx, attn_lat, ret_ctx, ret_lat, mod_l, d_row, wglu_bf, wout_bf, nw2, wr_split):
    tile = OUT_TILE
    ctx_tiles = N_CTX // tile
    ctx_blk = lambda i: (jnp.minimum(i, ctx_tiles - 1), 0)
    lat_blk = lambda i: (jnp.maximum(i - ctx_tiles, 0), 0)
    mod_row = lambda i: (jnp.where(i < ctx_tiles, 0, 1 + (i - ctx_tiles) // (DEC_SEQ // tile)), 0, 0)
    return pl.pallas_call(
        _outproj_kernel,
        grid=(N_TOK // tile,),
        in_specs=[pl.BlockSpec((tile, D_MODEL), lambda i: (i, 0)),
                  pl.BlockSpec((tile, SSM_WIDTH), lambda i: (i, C_U // SSM_WIDTH)),
                  pl.BlockSpec((tile, SSM_WIDTH), lambda i: (i, 0)),
                  pl.BlockSpec((tile, ATT_WIDTH), ctx_blk),
                  pl.BlockSpec((tile, ATT_WIDTH), lat_blk),
                  pl.BlockSpec((tile, RET_WIDTH), ctx_blk),
                  pl.BlockSpec((tile, RET_WIDTH), lat_blk),
                  pl.BlockSpec((None, 1, 6 * D_MODEL), mod_row),
                  pl.BlockSpec((1, SSM_WIDTH), lambda i: (0, 0)),
                  pl.BlockSpec((SSM_WIDTH, SSM_WIDTH), lambda i: (0, 0)),
                  pl.BlockSpec((D_MODEL, D_MODEL), lambda i: (0, 0)),
                  pl.BlockSpec((1, D_MODEL), lambda i: (0, 0)),
                  pl.BlockSpec((2, D_MODEL, 128), lambda i: (0, 0, 0))],
        out_specs=[pl.BlockSpec((tile, D_MODEL), lambda i: (i, 0)),
                   pl.BlockSpec((tile, D_MODEL), lambda i: (i, 0)),
                   pl.BlockSpec((128, tile), lambda i: (0, i))],
        out_shape=[jax.ShapeDtypeStruct((N_TOK, D_MODEL), F32),
                   jax.ShapeDtypeStruct((N_TOK, D_MODEL), BF16),
                   jax.ShapeDtypeStruct((128, N_TOK), F32)],
        compiler_params=_cparams(("parallel",)),
        name="out_projection",
    )(x, rest, ys, attn_ctx, attn_lat, ret_ctx, ret_lat, mod_l, d_row, wglu_bf, wout_bf, nw2, wr_split)


def _lane_cumsum(x01):
    rows, n = x01.shape
    r = lax.broadcasted_iota(jnp.int32, (256, 256), 0)
    c = lax.broadcasted_iota(jnp.int32, (256, 256), 1)
    tri = jnp.where(r <= c, 1.0, 0.0).astype(BF16)
    off = jnp.zeros((rows, 1), F32)
    parts = []
    for j in range(n // 256):
        cs = jnp.dot(x01[:, 256 * j:256 * (j + 1)].astype(BF16), tri, preferred_element_type=F32) + off
        parts.append(cs)
        off = cs[:, 255:256]
    return jnp.concatenate(parts, axis=1)


def _count(m):
    return jnp.sum(jnp.where(m, 1.0, 0.0), axis=1, keepdims=True)


def _route_kernel(aff_ref, slot_ref, gate_ref, offs_ref, *, cap, seg):
    n_seg = aff_ref.shape[1] // seg
    segs = [slice(i * seg, (i + 1) * seg) for i in range(n_seg)]
    tiny = float(jnp.finfo(jnp.float32).tiny)

    def step(_, bounds):
        out = []
        for i in range(n_seg):
            lo, hi = bounds[2 * i], bounds[2 * i + 1]
            mid = jnp.where(lo > 0.0, jnp.sqrt(lo) * jnp.sqrt(hi), jnp.maximum(hi * (2.0 ** -16), tiny))
            mid = jnp.minimum(jnp.maximum(mid, lo), hi)
            ok = _count(aff_ref[:, segs[i]] >= mid) >= cap
            out += [jnp.where(ok, mid, lo), jnp.where(ok, hi, mid)]
        return tuple(out)

    init = (jnp.zeros((N_EXPERTS, 1), F32), jnp.full((N_EXPERTS, 1), 2.0, F32)) * n_seg
    bounds = lax.fori_loop(0, ROUTE_ITERS, step, init)

    for i in range(n_seg):
        a = aff_ref[:, segs[i]]
        lo, hi = bounds[2 * i], bounds[2 * i + 1]
        above = a >= hi
        band = (a >= lo) & (a < hi)
        sel = above | (band & (_lane_cumsum(jnp.where(band, 1.0, 0.0)) <= cap - _count(above)))
        taken = _lane_cumsum(jnp.where(sel, 1.0, 0.0))
        slot_ref[:, segs[i]] = jnp.where(sel, taken - 1.0, -1.0).astype(jnp.int32)
        gate_ref[:, segs[i]] = jnp.where(sel, a, 0.0)
    if n_seg == 1:
        n_off = seg // OFFS_STEP
        before = [jnp.zeros((N_EXPERTS, 1), F32)] + [taken[:, OFFS_STEP * k - 1:OFFS_STEP * k] for k in range(1, n_off)]
        before.append(jnp.zeros((N_EXPERTS, 128 - n_off), F32))
        offs_ref[...] = jnp.concatenate(before, axis=1).astype(jnp.int32)
    else:
        offs_ref[...] = jnp.zeros(offs_ref.shape, jnp.int32)


def _route(aff_t, col_block0, n_blocks, width, seg, cap):
    return pl.pallas_call(
        functools.partial(_route_kernel, cap=cap, seg=seg),
        grid=(n_blocks,),
        in_specs=[pl.BlockSpec((N_EXPERTS, width), lambda i: (0, col_block0 + i))],
        out_specs=[pl.BlockSpec((N_EXPERTS, width), lambda i: (0, i))] * 2
                  + [pl.BlockSpec((N_EXPERTS, 128), lambda i: (0, i))],
        out_shape=[jax.ShapeDtypeStruct((N_EXPERTS, n_blocks * width), jnp.int32),
                   jax.ShapeDtypeStruct((N_EXPERTS, n_blocks * width), F32),
                   jax.ShapeDtypeStruct((N_EXPERTS, n_blocks * 128), jnp.int32)],
        compiler_params=_cparams(("parallel",)),
        name="route",
    )(aff_t)


def _gather_lat_kernel(offs_ref, slot_ref, gate_ref, h_ref, o_ref, g_ref, acc_ref, gacc_ref):
    r = pl.program_id(0)
    e = pl.program_id(1)
    win = GATHER_BLOCK + 16
    acc_ref[...] = jnp.zeros(acc_ref.shape, F32)
    gacc_ref[...] = jnp.zeros(gacc_ref.shape, F32)
    rows = lax.broadcasted_iota(jnp.int32, (win, GATHER_BLOCK), 0)
    for b in range(DEC_SEQ // GATHER_BLOCK):
        off = offs_ref[e, r * 128 + b * (GATHER_BLOCK // OFFS_STEP)]
        base = pl.multiple_of((off // 8) * 8, 8)
        toks = slice(b * GATHER_BLOCK, (b + 1) * GATHER_BLOCK)
        hit = (rows + base) == slot_ref[pl.ds(e, 1), toks]
        onehot = jnp.where(hit, 1.0, 0.0).astype(BF16)
        acc_ref[pl.ds(base, win), :] += jnp.dot(onehot, h_ref[toks, :], preferred_element_type=F32)
        gacc_ref[pl.ds(base, win), :] += jnp.sum(jnp.where(hit, gate_ref[pl.ds(e, 1), toks], 0.0), axis=1, keepdims=True)
    o_ref[...] = acc_ref[0:CAP_LAT, :].astype(o_ref.dtype)
    g_ref[...] = gacc_ref[0:CAP_LAT, :]


def _gather_lat(offs, slot, gate, h2):
    rows = pl.BlockSpec((N_EXPERTS, DEC_SEQ), lambda r, e, offs: (0, r))
    acc_rows = CAP_LAT + GATHER_BLOCK + 16
    return pl.pallas_call(
        _gather_lat_kernel,
        grid_spec=pltpu.PrefetchScalarGridSpec(
            num_scalar_prefetch=1,
            grid=(DEC_BATCH, N_EXPERTS),
            in_specs=[rows, rows,
                      pl.BlockSpec((DEC_SEQ, D_MODEL), lambda r, e, offs: (N_CTX // DEC_SEQ + r, 0))],
            out_specs=[pl.BlockSpec((None, CAP_LAT, D_MODEL), lambda r, e, offs: (e, r, 0)),
                       pl.BlockSpec((None, CAP_LAT, 1), lambda r, e, offs: (e, r, 0))],
            scratch_shapes=[pltpu.VMEM((acc_rows, D_MODEL), F32), pltpu.VMEM((acc_rows, 1), F32)]),
        out_shape=[jax.ShapeDtypeStruct((N_EXPERTS, DEC_BATCH * CAP_LAT, D_MODEL), BF16),
                   jax.ShapeDtypeStruct((N_EXPERTS, DEC_BATCH * CAP_LAT, 1), F32)],
        compiler_params=_cparams(("parallel", "arbitrary")),
        name="gather_lat",
    )(offs, slot, gate, h2)


def _ctx_onehot(slot):
    rows = lax.broadcasted_iota(jnp.int32, (CAP_CTX, SEQ), 0)
    hits = [rows == slot[e:e + 1, :] for e in range(N_EXPERTS)]
    onehot = jnp.concatenate([jnp.where(h, 1.0, 0.0) for h in hits], axis=0).astype(BF16)
    return onehot, hits


def _gather_ctx_kernel(slot_ref, gate_ref, h_ref, o_ref, g_ref):
    onehot, hits = _ctx_onehot(slot_ref[...])
    xs = jnp.dot(onehot, h_ref[...], preferred_element_type=F32).astype(o_ref.dtype)
    gate = gate_ref[...]
    for e in range(N_EXPERTS):
        o_ref[e] = xs[e * CAP_CTX:(e + 1) * CAP_CTX]
        g_ref[e] = jnp.sum(jnp.where(hits[e], gate[e:e + 1, :], 0.0), axis=1, keepdims=True)


def _gather_ctx(slot, gate, h2):
    rows = pl.BlockSpec((N_EXPERTS, SEQ), lambda r: (0, r))
    return pl.pallas_call(
        _gather_ctx_kernel,
        grid=(BATCH,),
        in_specs=[rows, rows, pl.BlockSpec((SEQ, D_MODEL), lambda r: (r, 0))],
        out_specs=[pl.BlockSpec((N_EXPERTS, CAP_CTX, D_MODEL), lambda r: (0, r, 0)),
                   pl.BlockSpec((N_EXPERTS, CAP_CTX, 1), lambda r: (0, r, 0))],
        out_shape=[jax.ShapeDtypeStruct((N_EXPERTS, BATCH * CAP_CTX, D_MODEL), BF16),
                   jax.ShapeDtypeStruct((N_EXPERTS, BATCH * CAP_CTX, 1), F32)],
        compiler_params=_cparams(("parallel",)),
        name="gather_ctx",
    )(slot, gate, h2)


def _ffn_kernel(xc_ref, xl_ref, gc_ref, gl_ref, wg_ref, wu_ref, wd_ref, yc_ref, yl_ref, accc_ref, accl_ref):
    f = pl.program_id(1)
    wg = wg_ref[...].astype(BF16)
    wu = wu_ref[...].astype(BF16)
    wd = wd_ref[...].astype(BF16)

    def part(x_ref, gate_ref, acc_ref, y_ref):
        x = x_ref[...]
        a = jnp.dot(x, wg, preferred_element_type=F32)
        up = jnp.dot(x, wu, preferred_element_type=F32)
        mid = (a * jax.nn.sigmoid(a) * up).astype(BF16)
        y = jnp.dot(mid, wd, preferred_element_type=F32)

        @pl.when(f == 0)
        def _():
            acc_ref[...] = y

        @pl.when(f > 0)
        def _():
            acc_ref[...] += y

        @pl.when(f == pl.num_programs(1) - 1)
        def _():
            y_ref[...] = (acc_ref[...] * gate_ref[...]).astype(y_ref.dtype)

    part(xc_ref, gc_ref, accc_ref, yc_ref)
    part(xl_ref, gl_ref, accl_ref, yl_ref)


def _expert_ffn(xs_ctx, xs_lat, gs_ctx, gs_lat, w_gate, w_up, w_down, layer):
    tf = 512
    nc, nl = xs_ctx.shape[1], xs_lat.shape[1]
    return pl.pallas_call(
        _ffn_kernel,
        grid=(N_EXPERTS, EXPERT_FF // tf),
        in_specs=[pl.BlockSpec((None, nc, D_MODEL), lambda e, f: (e, 0, 0)),
                  pl.BlockSpec((None, nl, D_MODEL), lambda e, f: (e, 0, 0)),
                  pl.BlockSpec((None, nc, 1), lambda e, f: (e, 0, 0)),
                  pl.BlockSpec((None, nl, 1), lambda e, f: (e, 0, 0)),
                  pl.BlockSpec((None, None, D_MODEL, tf), lambda e, f: (layer, e, 0, f)),
                  pl.BlockSpec((None, None, D_MODEL, tf), lambda e, f: (layer, e, 0, f)),
                  pl.BlockSpec((None, None, tf, D_MODEL), lambda e, f: (layer, e, f, 0))],
        out_specs=[pl.BlockSpec((None, nc, D_MODEL), lambda e, f: (e, 0, 0)),
                   pl.BlockSpec((None, nl, D_MODEL), lambda e, f: (e, 0, 0))],
        out_shape=[jax.ShapeDtypeStruct(xs_ctx.shape, BF16), jax.ShapeDtypeStruct(xs_lat.shape, BF16)],
        scratch_shapes=[pltpu.VMEM((nc, D_MODEL), F32), pltpu.VMEM((nl, D_MODEL), F32)],
        compiler_params=_cparams(("parallel", "arbitrary")),
        name="expert_ffn",
    )(xs_ctx, xs_lat, gs_ctx, gs_lat, w_gate, w_up, w_down)


def _scatter_lat_kernel(offs_ref, slot_ref, y_ref, x_ref, mod_ref, o_ref):
    r = pl.program_id(0)
    t = pl.program_id(1)
    slot_t = slot_ref[...].astype(F32).T
    lane = lax.broadcasted_iota(jnp.int32, (OFFS_STEP, SCATTER_WINDOW), 1)
    ffn = jnp.zeros((OFFS_STEP, D_MODEL), F32)
    for e in range(N_EXPERTS):
        off = offs_ref[e, r * 128 + t]
        base = pl.multiple_of(jnp.minimum((off // 16) * 16, CAP_LAT - SCATTER_WINDOW), 16)
        onehot = jnp.where((lane + base).astype(F32) == slot_t[:, e:e + 1], 1.0, 0.0).astype(BF16)
        ffn += jnp.dot(onehot, y_ref[e, pl.ds(base, SCATTER_WINDOW), :], preferred_element_type=F32)
    o_ref[...] = x_ref[...] + mod_ref[:, 5 * D_MODEL:6 * D_MODEL] * ffn


def _scatter_lat(offs, slot, ys, x, mod_l):
    tt = OFFS_STEP
    nt = DEC_SEQ // tt
    blk0 = N_CTX // tt
    tok = pl.BlockSpec((tt, D_MODEL), lambda r, t, offs: (blk0 + r * nt + t, 0))
    return pl.pallas_call(
        _scatter_lat_kernel,
        grid_spec=pltpu.PrefetchScalarGridSpec(
            num_scalar_prefetch=1,
            grid=(DEC_BATCH, nt),
            in_specs=[pl.BlockSpec((N_EXPERTS, tt), lambda r, t, offs: (0, r * nt + t)),
                      pl.BlockSpec((N_EXPERTS, CAP_LAT, D_MODEL), lambda r, t, offs: (0, r, 0)),
                      tok,
                      pl.BlockSpec((None, 1, 6 * D_MODEL), lambda r, t, offs: (1 + r, 0, 0))],
            out_specs=tok),
        out_shape=jax.ShapeDtypeStruct(x.shape, F32),
        input_output_aliases={3: 0},
        compiler_params=_cparams(("parallel", "parallel")),
        name="scatter_lat",
    )(offs, slot, ys, x, mod_l)


def _scatter_ctx_kernel(slot_ref, y_ref, x_ref, mod_ref, o_ref):
    onehot, _ = _ctx_onehot(slot_ref[...])
    y = jnp.concatenate([y_ref[e] for e in range(N_EXPERTS)], axis=0)
    ffn = lax.dot_general(onehot, y, TN, preferred_element_type=F32)
    o_ref[...] = x_ref[...] + mod_ref[:, 5 * D_MODEL:6 * D_MODEL] * ffn


def _scatter_ctx(slot, ys, x, mod_l):
    tok = pl.BlockSpec((SEQ, D_MODEL), lambda r: (r, 0))
    return pl.pallas_call(
        _scatter_ctx_kernel,
        grid=(BATCH,),
        in_specs=[pl.BlockSpec((N_EXPERTS, SEQ), lambda r: (0, r)),
                  pl.BlockSpec((N_EXPERTS, CAP_CTX, D_MODEL), lambda r: (0, r, 0)),
                  tok,
                  pl.BlockSpec((None, 1, 6 * D_MODEL), lambda r: (0, 0, 0))],
        out_specs=tok,
        out_shape=jax.ShapeDtypeStruct(x.shape, F32),
        input_output_aliases={2: 0},
        compiler_params=_cparams(("parallel",)),
        name="scatter_ctx",
    )(slot, ys, x, mod_l)


def _final_norm_kernel(x_ref, w_ref, o_ref):
    x = x_ref[...]
    o_ref[...] = x * lax.rsqrt(jnp.mean(x * x, axis=-1, keepdims=True) + EPS) * w_ref[...]


def _final_norm(x, w):
    return pl.pallas_call(
        _final_norm_kernel,
        grid=(N_TILES,),
        in_specs=[pl.BlockSpec((TILE, D_MODEL), lambda i: (i, 0)), pl.BlockSpec((1, D_MODEL), lambda i: (0, 0))],
        out_specs=pl.BlockSpec((TILE, D_MODEL), lambda i: (i, 0)),
        out_shape=jax.ShapeDtypeStruct(x.shape, F32),
        compiler_params=_cparams(("parallel",)),
        name="final_norm",
    )(x, w)


def _rope_tables():
    rows = DEC_SEQ // GRID_W
    row = jnp.repeat(jnp.arange(rows, dtype=F32), GRID_W)
    col = jnp.tile(jnp.arange(GRID_W, dtype=F32), rows)
    n_freq = HEAD_DIM // 4
    inv_freq = ROPE_THETA ** (-jnp.arange(n_freq, dtype=F32) / n_freq)
    ang = jnp.concatenate([row[:, None] * inv_freq, col[:, None] * inv_freq], axis=-1)
    cos, sin = jnp.cos(ang), jnp.sin(ang)
    cos_t = jnp.tile(jnp.concatenate([cos, cos], -1), (1, 128 // HEAD_DIM))
    sin_t = jnp.tile(jnp.concatenate([-sin, sin], -1), (1, 128 // HEAD_DIM))
    cos_t = jnp.concatenate([jnp.ones((TILE, 128), F32), cos_t], axis=0)
    sin_t = jnp.concatenate([jnp.zeros((TILE, 128), F32), sin_t], axis=0)
    return cos_t, sin_t


def _s5_initial_rows(state_ssm):
    st = state_ssm.astype(F32)
    re, im = st[..., 0], st[..., 1]
    both = jnp.stack([jnp.concatenate([re, im], -1), jnp.concatenate([im, re], -1)], axis=3)
    return both.transpose(1, 0, 2, 3, 4, 5).reshape(DEPTH, 4 * DEC_BATCH, N_SSM_GROUPS * 128)


def kernel(x_prompt, x_sample, cache_k, cache_v, state_ssm, state_ret, c, c_ctx, w_mod, b_mod, norm1_w, norm2_w, w_in, w_out, qn_w, kn_w, ssm_lambda_re, ssm_lambda_im, ssm_b_re, ssm_b_im, ssm_c_re, ssm_c_im, ssm_log_dt, ssm_d, ssm_w_glu, ret_decay_logit, ret_norm_w, w_router, w_gate, w_up, w_down, final_norm_w):
    x = jnp.concatenate([x_prompt.reshape(N_CTX, D_MODEL), x_sample.reshape(N_LAT, D_MODEL)], axis=0)
    cond_t = jnp.zeros((D_MODEL, 8), F32).at[:, 0].set(c_ctx).at[:, 1:1 + DEC_BATCH].set(c.T)
    mod = _modulation(cond_t, w_mod, b_mod).reshape(DEPTH, 8, 1, 6 * D_MODEL)
    cos_t, sin_t = _rope_tables()
    zero_ret = jnp.zeros((BATCH, 2, RET_HEADS, HEAD_DIM, HEAD_DIM), F32)
    ctx_blocks = N_CTX // DEC_SEQ

    w_in_bf, w_out_bf, w_glu_bf = w_in.astype(BF16), w_out.astype(BF16), ssm_w_glu.astype(BF16)
    s5_toe, s5_inj, s5_ro, s5_a = jax.vmap(_s5_matrices)(ssm_lambda_re, ssm_lambda_im, ssm_b_re, ssm_b_im,
                                                         ssm_c_re, ssm_c_im, ssm_log_dt)
    s5_h0 = _s5_initial_rows(state_ssm)
    ret_dec, ret_mask, ret_cdec = jax.vmap(_retention_tables)(ret_decay_logit)
    wr = jnp.pad(w_router.astype(F32), ((0, 0), (0, 0), (0, 128 - N_EXPERTS)))
    wr_hi = wr.astype(BF16)
    wr_split = jnp.stack([wr_hi, (wr - wr_hi.astype(F32)).astype(BF16)], axis=1)
    qn_t, kn_t = jnp.tile(qn_w, (1, 2)), jnp.tile(kn_w, (1, 2))
    cache_kv = jnp.concatenate([cache_k.reshape(DEC_BATCH, DEPTH, PAST_LEN, KV_WIDTH),
                                cache_v.reshape(DEC_BATCH, DEPTH, PAST_LEN, KV_WIDTH)], axis=-1).astype(BF16)

    ks, vs, ss, rs = [], [], [], []
    for l in range(DEPTH):
        mod_l = mod[l]
        q, kv, ub, rest = _in_projection(x, mod_l, norm1_w[l].reshape(1, -1), w_in_bf[l],
                                     qn_t[l].reshape(1, -1), kn_t[l].reshape(1, -1), cos_t, sin_t)
        ks.append(rest[:N_CTX, C_K:C_K + KV_WIDTH].reshape(BATCH, SEQ, N_KV_HEADS, HEAD_DIM))
        vs.append(rest[:N_CTX, C_V:C_V + KV_WIDTH].reshape(BATCH, SEQ, N_KV_HEADS, HEAD_DIM))

        kv_ctx = kv[:N_CTX].reshape(BATCH, SEQ, 2 * KV_WIDTH)
        kv_lat = jnp.concatenate([kv[N_CTX:].reshape(DEC_BATCH, DEC_SEQ, 2 * KV_WIDTH), cache_kv[:, l]], axis=1)
        attn_ctx = _attention(q, kv_ctx, 0, BATCH, SEQ, SEQ)
        attn_lat = _attention(q, kv_lat, N_CTX // LAT_TQ, DEC_BATCH, DEC_SEQ, LAT_TQ)

        y_loc, states = _s5_local(ub.reshape(S5_ROWS, S5_CHUNK * SSM_WIDTH), s5_toe, s5_inj, l)
        hf, hb, fin_f, fin_b = _s5_scan(states, s5_a[l], s5_h0[l])
        ys = _s5_out(y_loc, hf, hb, s5_ro, l).reshape(N_TOK, SSM_WIDTH)
        fin = jnp.stack([fin_f, fin_b], axis=1).reshape(BATCH, 2, N_SSM_GROUPS, 2, SSM_STATE)
        ss.append(fin.transpose(0, 1, 2, 4, 3))

        nw_ret = ret_norm_w[l].reshape(1, -1)
        ret_ctx, fin_ret = _retention(rest, ret_dec[l], ret_mask[l], ret_cdec[l], zero_ret, nw_ret, 0, BATCH, SEQ)
        ret_lat, _ = _retention(rest, ret_dec[l], ret_mask[l], ret_cdec[l], state_ret[:, l].astype(F32), nw_ret,
                                ctx_blocks, DEC_BATCH, DEC_SEQ)
        rs.append(fin_ret)

        x1, h2, aff_t = _out_projection(x, rest, ys, attn_ctx, attn_lat, ret_ctx, ret_lat, mod_l,
                                        ssm_d[l].reshape(1, -1), w_glu_bf[l], w_out_bf[l],
                                        norm2_w[l].reshape(1, -1), wr_split[l])

        slot_ctx, gate_ctx, _ = _route(aff_t, 0, 1, N_CTX, SEQ, CAP_CTX)
        slot_lat, gate_lat, offs_lat = _route(aff_t, N_CTX // DEC_SEQ, DEC_BATCH, DEC_SEQ, DEC_SEQ, CAP_LAT)
        xs_ctx, gs_ctx = _gather_ctx(slot_ctx, gate_ctx, h2)
        xs_lat, gs_lat = _gather_lat(offs_lat, slot_lat, gate_lat, h2)
        y_ctx, y_lat = _expert_ffn(xs_ctx, xs_lat, gs_ctx, gs_lat, w_gate, w_up, w_down, l)
        x = _scatter_ctx(slot_ctx, y_ctx, x1, mod_l)
        x = _scatter_lat(offs_lat, slot_lat, y_lat, x, mod_l)

    y = _final_norm(x, final_norm_w.reshape(1, -1))
    y_prompt = y[:N_CTX].reshape(BATCH, SEQ, D_MODEL)
    y_sample = y[N_CTX:].reshape(DEC_BATCH, DEC_SEQ, D_MODEL)
    return (y_prompt, y_sample, jnp.stack(ks, axis=1), jnp.stack(vs, axis=1),
            jnp.stack(ss, axis=1), jnp.stack(rs, axis=1))
```

```python
import functools

import jax
import jax.numpy as jnp
from jax import lax
from jax.experimental import pallas as pl
from jax.experimental.pallas import tpu as pltpu

F32 = jnp.float32
BF16 = jnp.bfloat16

D_MODEL = 1024
BATCH = 16
SEQ = 256
DEPTH = 4
DEC_BATCH = 2
DEC_SEQ = 4096
PAST_LEN = 256
GRID_W = 64
HEAD_DIM = 64
SSM_WIDTH = 256
SSM_GROUP = 16
N_SSM_GROUPS = 16
SSM_STATE = 64
ATT_WIDTH = 512
N_HEADS = 8
N_KV_HEADS = 2
KV_WIDTH = 128
RET_WIDTH = 256
RET_HEADS = 4
IN_WIDTH = 2048
RET_CHUNK = 128
N_EXPERTS = 16
EXPERT_FF = 1024
ROPE_THETA = 10000.0
EPS = 1e-6

N_CTX = BATCH * SEQ
N_LAT = DEC_BATCH * DEC_SEQ
N_TOK = N_CTX + N_LAT
TILE = 512
OUT_TILE = 256
LAT_TQ = 512
N_TILES = N_TOK // TILE
CTX_TILES = N_CTX // TILE
LAT_TILES_PER_REQ = DEC_SEQ // TILE
CAP_CTX = 2 * SEQ // N_EXPERTS
CAP_LAT = 2 * DEC_SEQ // N_EXPERTS
S5_CHUNK = 16
CTX_CHUNKS = SEQ // S5_CHUNK
LAT_CHUNKS = DEC_SEQ // S5_CHUNK
S5_ROWS_CTX = BATCH * CTX_CHUNKS
S5_ROWS = S5_ROWS_CTX + DEC_BATCH * LAT_CHUNKS
C_U, C_K, C_V, C_RQ, C_RK, C_RV, C_RG = 0, 256, 384, 512, 768, 1024, 1280
REST_WIDTH = 1536
ROUTE_ITERS = 48
OFFS_STEP = 128
GATHER_BLOCK = 256
SCATTER_WINDOW = 256
VMEM_LIMIT = 56 * 1024 * 1024

TN = (((0,), (0,)), ((), ()))
NT = (((1,), (1,)), ((), ()))


def _cparams(sem):
    return pltpu.CompilerParams(dimension_semantics=sem, vmem_limit_bytes=VMEM_LIMIT)


def _tile_mod_row(i):
    return jnp.where(i < CTX_TILES, 0, 1 + (i - CTX_TILES) // LAT_TILES_PER_REQ)


def _split_dot(v, m):
    hi = v.astype(BF16)
    lo = (v - hi.astype(F32)).astype(BF16)
    return (jnp.dot(hi, m, preferred_element_type=F32)
            + jnp.dot(lo, m, preferred_element_type=F32))


def _group_avg_matrix():
    r = lax.broadcasted_iota(jnp.int32, (128, 128), 0) // HEAD_DIM
    c = lax.broadcasted_iota(jnp.int32, (128, 128), 1) // HEAD_DIM
    return jnp.where(r == c, 1.0 / HEAD_DIM, 0.0).astype(BF16)


def _mod_kernel(ct_ref, w_ref, b_ref, o_ref):
    c = ct_ref[...]
    s = c * jax.nn.sigmoid(c)
    w = w_ref[...]
    rows = [jnp.sum(w * s[:, r:r + 1], axis=0, keepdims=True) for r in range(3)]
    rows.append(jnp.zeros((5, w.shape[1]), F32))
    o_ref[...] = jnp.concatenate(rows, axis=0) + b_ref[...]


def _modulation(cond_t, w_mod, b_mod):
    tn = 512
    n = 6 * D_MODEL
    return pl.pallas_call(
        _mod_kernel,
        grid=(DEPTH, n // tn),
        in_specs=[pl.BlockSpec((D_MODEL, 8), lambda l, j: (0, 0)),
                  pl.BlockSpec((None, D_MODEL, tn), lambda l, j: (l, 0, j)),
                  pl.BlockSpec((None, 1, tn), lambda l, j: (l, 0, j))],
        out_specs=pl.BlockSpec((None, 8, tn), lambda l, j: (l, 0, j)),
        out_shape=jax.ShapeDtypeStruct((DEPTH, 8, n), F32),
        compiler_params=_cparams(("arbitrary", "arbitrary")),
        name="modulation",
    )(cond_t, w_mod, b_mod.reshape(DEPTH, 1, n))


def _inproj_kernel(x_ref, mod_ref, nw_ref, w_ref, qn_ref, kn_ref, cos_ref, sin_ref, q_ref, kv_ref, ub_ref, rest_ref):
    x = x_ref[...]
    shift = mod_ref[:, 0:D_MODEL]
    scale = mod_ref[:, D_MODEL:2 * D_MODEL]
    y = x * lax.rsqrt(jnp.mean(x * x, axis=-1, keepdims=True) + EPS) * nw_ref[...]
    h = y * (1.0 + scale) + shift
    proj = jnp.dot(h.astype(BF16), w_ref[...], preferred_element_type=F32)

    avg = _group_avg_matrix()
    cos = cos_ref[...]
    sin = sin_ref[...]
    first_half = (lax.broadcasted_iota(jnp.int32, (TILE, 128), 1) % HEAD_DIM) < (HEAD_DIM // 2)

    def head_norm(z, wrow):
        return z * lax.rsqrt(_split_dot(z * z, avg) + EPS) * wrow

    def rope(z):
        partner = jnp.where(first_half, pltpu.roll(z, 128 - HEAD_DIM // 2, 1), pltpu.roll(z, HEAD_DIM // 2, 1))
        return z * cos + partner * sin

    def col(off, j):
        return proj[:, off + 128 * j: off + 128 * (j + 1)]

    qn = qn_ref[...]
    for j in range(ATT_WIDTH // 128):
        z = rope(head_norm(col(SSM_WIDTH, j), qn)) * (HEAD_DIM ** -0.5)
        q_ref[:, 128 * j:128 * (j + 1)] = z.astype(q_ref.dtype)
    p_k = SSM_WIDTH + ATT_WIDTH
    rest_ref[:, C_U:C_U + SSM_WIDTH] = proj[:, 0:SSM_WIDTH]
    ub_ref[...] = proj[:, 0:SSM_WIDTH].astype(ub_ref.dtype)
    k = rope(head_norm(col(p_k, 0), kn_ref[...]))
    v = col(p_k + KV_WIDTH, 0)
    rest_ref[:, C_K:C_K + KV_WIDTH] = k
    rest_ref[:, C_V:C_V + KV_WIDTH] = v
    kv_ref[:, 0:KV_WIDTH] = k.astype(kv_ref.dtype)
    kv_ref[:, KV_WIDTH:2 * KV_WIDTH] = v.astype(kv_ref.dtype)
    p_r = p_k + 2 * KV_WIDTH
    for j in range(RET_WIDTH // 128):
        rest_ref[:, C_RQ + 128 * j:C_RQ + 128 * (j + 1)] = rope(col(p_r, j))
        rest_ref[:, C_RK + 128 * j:C_RK + 128 * (j + 1)] = rope(col(p_r + RET_WIDTH, j)) * (HEAD_DIM ** -0.5)
    rest_ref[:, C_RV:C_RV + 2 * RET_WIDTH] = proj[:, p_r + 2 * RET_WIDTH:p_r + 4 * RET_WIDTH]


def _in_projection(x, mod_l, nw, w_in_bf, qn, kn, cos_t, sin_t):
    def rope_blk(i):
        return (jnp.where(i < CTX_TILES, 0, 1 + (i - CTX_TILES) % LAT_TILES_PER_REQ), 0)
    return pl.pallas_call(
        _inproj_kernel,
        grid=(N_TILES,),
        in_specs=[pl.BlockSpec((TILE, D_MODEL), lambda i: (i, 0)),
                  pl.BlockSpec((None, 1, 6 * D_MODEL), lambda i: (_tile_mod_row(i), 0, 0)),
                  pl.BlockSpec((1, D_MODEL), lambda i: (0, 0)),
                  pl.BlockSpec((D_MODEL, IN_WIDTH), lambda i: (0, 0)),
                  pl.BlockSpec((1, 128), lambda i: (0, 0)),
                  pl.BlockSpec((1, 128), lambda i: (0, 0)),
                  pl.BlockSpec((TILE, 128), rope_blk),
                  pl.BlockSpec((TILE, 128), rope_blk)],
        out_specs=[pl.BlockSpec((TILE, ATT_WIDTH), lambda i: (i, 0)),
                   pl.BlockSpec((TILE, 2 * KV_WIDTH), lambda i: (i, 0)),
                   pl.BlockSpec((TILE, SSM_WIDTH), lambda i: (i, 0)),
                   pl.BlockSpec((TILE, REST_WIDTH), lambda i: (i, 0))],
        out_shape=[jax.ShapeDtypeStruct((N_TOK, ATT_WIDTH), BF16),
                   jax.ShapeDtypeStruct((N_TOK, 2 * KV_WIDTH), BF16),
                   jax.ShapeDtypeStruct((N_TOK, SSM_WIDTH), BF16),
                   jax.ShapeDtypeStruct((N_TOK, REST_WIDTH), F32)],
        compiler_params=_cparams(("parallel",)),
        name="in_projection",
    )(x, mod_l, nw, w_in_bf, qn, kn, cos_t, sin_t)


def _attn_kernel(q_ref, k_ref, v_ref, o_ref):
    group = N_HEADS // N_KV_HEADS
    for kv in range(N_KV_HEADS):
        k = k_ref[:, kv * HEAD_DIM:(kv + 1) * HEAD_DIM]
        v = v_ref[:, kv * HEAD_DIM:(kv + 1) * HEAD_DIM]
        for g in range(group):
            cols = slice((kv * group + g) * HEAD_DIM, (kv * group + g + 1) * HEAD_DIM)
            s = lax.dot_general(q_ref[:, cols], k, NT, preferred_element_type=F32)
            m = jnp.max(s, axis=-1, keepdims=True)
            p = jnp.exp(s - m)
            l = jnp.sum(p, axis=-1, keepdims=True)
            o = jnp.dot(p.astype(BF16), v, preferred_element_type=F32) / l
            o_ref[:, cols] = o.astype(o_ref.dtype)


def _attention(q, kv, row_block0, n_req, lq, tq):
    lk = kv.shape[1]
    nq = lq // tq
    return pl.pallas_call(
        _attn_kernel,
        grid=(n_req, nq),
        in_specs=[pl.BlockSpec((tq, ATT_WIDTH), lambda r, i: (row_block0 + r * nq + i, 0)),
                  pl.BlockSpec((None, lk, KV_WIDTH), lambda r, i: (r, 0, 0)),
                  pl.BlockSpec((None, lk, KV_WIDTH), lambda r, i: (r, 0, 1))],
        out_specs=pl.BlockSpec((tq, ATT_WIDTH), lambda r, i: (r * nq + i, 0)),
        out_shape=jax.ShapeDtypeStruct((n_req * lq, ATT_WIDTH), BF16),
        compiler_params=_cparams(("parallel", "parallel")),
        name="attention",
    )(q, kv, kv)


def _s5_local_kernel(u_ref, t_ref, b_ref, y_ref, s_ref):
    u = u_ref[...]
    y_ref[...] = jnp.dot(u, t_ref[...], preferred_element_type=F32)
    s_ref[...] = jnp.dot(u, b_ref[...], preferred_element_type=F32)


def _s5_local(u2, toeplitz, inject, layer):
    tn = 256
    n = S5_CHUNK * SSM_WIDTH
    wspec = pl.BlockSpec((None, n, tn), lambda i: (layer, 0, i))
    ospec = pl.BlockSpec((S5_ROWS, tn), lambda i: (0, i))
    return pl.pallas_call(
        _s5_local_kernel,
        grid=(n // tn,),
        in_specs=[pl.BlockSpec((S5_ROWS, n), lambda i: (0, 0)), wspec, wspec],
        out_specs=[ospec, ospec],
        out_shape=[jax.ShapeDtypeStruct((S5_ROWS, n), F32)] * 2,
        compiler_params=_cparams(("parallel",)),
        name="s5_local",
    )(u2, toeplitz, inject)


def _s5_scan_kernel(sf_ref, sb_ref, a_ref, h0_ref, hf_ref, hb_ref, ff_ref, fb_ref, sfs_ref, sbs_ref):
    a1f, a2f, a1b, a2b = a_ref[0:1, :], a_ref[1:2, :], a_ref[2:3, :], a_ref[3:4, :]
    w = sf_ref.shape[1]
    for c in range(w // 128):
        cols = slice(128 * c, 128 * (c + 1))
        sfs_ref[:, cols] = pltpu.roll(sf_ref[:, cols], SSM_STATE, 1)
        sbs_ref[:, cols] = pltpu.roll(sb_ref[:, cols], SSM_STATE, 1)

    zero = jnp.zeros((1, w), F32)
    for r in range(BATCH):
        hf, hfs, hb, hbs = zero, zero, zero, zero
        for j in range(CTX_CHUNKS):
            row = r * CTX_CHUNKS + j
            hf_ref[row:row + 1, :] = hf
            hf, hfs = (a1f * hf + a2f * hfs + sf_ref[row:row + 1, :], a1f * hfs - a2f * hf + sfs_ref[row:row + 1, :])
            row = r * CTX_CHUNKS + CTX_CHUNKS - 1 - j
            hb_ref[row:row + 1, :] = hb
            hb, hbs = (a1b * hb + a2b * hbs + sb_ref[row:row + 1, :], a1b * hbs - a2b * hb + sbs_ref[row:row + 1, :])
        ff_ref[r:r + 1, :] = hf
        fb_ref[r:r + 1, :] = hb

    def body(j, carry):
        out = []
        for r in range(DEC_BATCH):
            hf, hfs, hb, hbs = carry[4 * r:4 * r + 4]
            row = pl.ds(S5_ROWS_CTX + r * LAT_CHUNKS + j, 1)
            hf_ref[row, :] = hf
            nf = a1f * hf + a2f * hfs + sf_ref[row, :]
            nfs = a1f * hfs - a2f * hf + sfs_ref[row, :]
            row = pl.ds(S5_ROWS_CTX + r * LAT_CHUNKS + LAT_CHUNKS - 1 - j, 1)
            hb_ref[row, :] = hb
            nb = a1b * hb + a2b * hbs + sb_ref[row, :]
            nbs = a1b * hbs - a2b * hb + sbs_ref[row, :]
            out += [nf, nfs, nb, nbs]
        return tuple(out)

    init = tuple(h0_ref[i:i + 1, :] for i in range(4 * DEC_BATCH))
    lax.fori_loop(0, LAT_CHUNKS, body, init)


def _s5_scan(states, a_rows, h0_rows):
    w = 512
    full = N_SSM_GROUPS * 128
    nb = full // w
    sspec = pl.BlockSpec((S5_ROWS, w), lambda i: (0, i))
    fspec = pl.BlockSpec((BATCH, w), lambda i: (0, i))
    return pl.pallas_call(
        _s5_scan_kernel,
        grid=(nb,),
        in_specs=[sspec,
                  pl.BlockSpec((S5_ROWS, w), lambda i: (0, nb + i)),
                  pl.BlockSpec((8, w), lambda i: (0, i)),
                  pl.BlockSpec((8, w), lambda i: (0, i))],
        out_specs=[sspec, sspec, fspec, fspec],
        out_shape=[jax.ShapeDtypeStruct((S5_ROWS, full), F32)] * 2 + [jax.ShapeDtypeStruct((BATCH, full), F32)] * 2,
        scratch_shapes=[pltpu.VMEM((S5_ROWS, w), F32), pltpu.VMEM((S5_ROWS, w), F32)],
        compiler_params=_cparams(("parallel",)),
        name="s5_scan",
    )(states, states, a_rows, h0_rows)


def _s5_out_kernel(y_ref, hf_ref, hb_ref, c_ref, o_ref, h_ref):
    full = N_SSM_GROUPS * 128

    @pl.when(pl.program_id(0) == 0)
    def _():
        h_ref[:, 0:full] = hf_ref[...].astype(BF16)
        h_ref[:, full:2 * full] = hb_ref[...].astype(BF16)

    o_ref[...] = y_ref[...] + jnp.dot(h_ref[...], c_ref[...], preferred_element_type=F32)


def _s5_out(y_loc, hf, hb, readout, layer):
    tn = 256
    n = S5_CHUNK * SSM_WIDTH
    full = N_SSM_GROUPS * 128
    hspec = pl.BlockSpec((S5_ROWS, full), lambda i: (0, 0))
    return pl.pallas_call(
        _s5_out_kernel,
        grid=(n // tn,),
        in_specs=[pl.BlockSpec((S5_ROWS, tn), lambda i: (0, i)), hspec, hspec,
                  pl.BlockSpec((None, 2 * full, tn), lambda i: (layer, 0, i))],
        out_specs=pl.BlockSpec((S5_ROWS, tn), lambda i: (0, i)),
        out_shape=jax.ShapeDtypeStruct((S5_ROWS, n), F32),
        scratch_shapes=[pltpu.VMEM((S5_ROWS, 2 * full), BF16)],
        compiler_params=_cparams(("arbitrary",)),
        name="s5_out",
    )(y_loc, hf, hb, readout)


def _s5_matrices(lam_re, lam_im, b_re, b_im, c_re, c_im, log_dt):
    hp = lax.Precision.HIGHEST
    n = S5_CHUNK
    tau = jnp.arange(n + 1, dtype=F32)
    toes, injs, ros, a_rows = [], [], [], []
    for di in range(2):
        lr, li = lam_re[di].astype(F32), lam_im[di].astype(F32)
        dt = jnp.exp(log_dt[di].astype(F32))[:, None]
        mag = jnp.exp(lr * dt * tau[:, None, None])
        ang = li * dt * tau[:, None, None]
        e_re, e_im = mag * jnp.cos(ang), mag * jnp.sin(ang)
        nr, ni = e_re[1] - 1.0, e_im[1]
        den = lr * lr + li * li
        f_re, f_im = (nr * lr + ni * li) / den, (ni * lr - nr * li) / den
        br, bi = b_re[di].astype(F32), b_im[di].astype(F32)
        bb_re = f_re[..., None] * br - f_im[..., None] * bi
        bb_im = f_re[..., None] * bi + f_im[..., None] * br
        cr, ci = c_re[di].astype(F32), c_im[di].astype(F32)
        ce_re = cr[None] * e_re[:n, :, None, :] - ci[None] * e_im[:n, :, None, :]
        ce_im = cr[None] * e_im[:n, :, None, :] + ci[None] * e_re[:n, :, None, :]
        kern = (jnp.einsum('tgcp,gpd->gdtc', ce_re, bb_re, precision=hp)
                - jnp.einsum('tgcp,gpd->gdtc', ce_im, bb_im, precision=hp))
        toes.append(kern.reshape(N_SSM_GROUPS * SSM_GROUP, n * SSM_GROUP))
        pe_re, pe_im = (e_re[:n][::-1], e_im[:n][::-1]) if di == 0 else (e_re[:n], e_im[:n])
        inj_re = pe_re[..., None] * bb_re[None] - pe_im[..., None] * bb_im[None]
        inj_im = pe_re[..., None] * bb_im[None] + pe_im[..., None] * bb_re[None]
        injs.append(jnp.concatenate([inj_re.transpose(0, 1, 3, 2), inj_im.transpose(0, 1, 3, 2)], -1))
        qe_re, qe_im = (e_re[1:], e_im[1:]) if di == 0 else (e_re[1:][::-1], e_im[1:][::-1])
        ro_re = cr[None] * qe_re[:, :, None, :] - ci[None] * qe_im[:, :, None, :]
        ro_im = cr[None] * qe_im[:, :, None, :] + ci[None] * qe_re[:, :, None, :]
        ros.append(jnp.concatenate([ro_re.transpose(1, 3, 0, 2), -ro_im.transpose(1, 3, 0, 2)], axis=1))
        a_re, a_im = e_re[n], e_im[n]
        a_rows.append(jnp.concatenate([a_re, a_re], -1).reshape(1, -1))
        a_rows.append(jnp.concatenate([-a_im, a_im], -1).reshape(1, -1))
    size = n * SSM_WIDTH
    g = N_SSM_GROUPS
    row = jnp.arange(size)[:, None]
    col = jnp.arange(size)[None, :]
    src = jnp.arange(n * SSM_GROUP)[:, None]

    def spread(table, copy, keep):
        wide = jnp.dot(table.astype(BF16), copy.astype(BF16), preferred_element_type=BF16)
        return jnp.where(keep, wide, jnp.zeros((), BF16))

    copy_tc = (src // SSM_GROUP == col // SSM_WIDTH) & (src % SSM_GROUP == col % SSM_GROUP)
    copy_dp = (src // 128 == col // (g * 128)) & (src % 128 == col % 128)
    lag_c = jnp.arange(2 * n * SSM_GROUP)[None, :, None]
    dst_c = jnp.arange(n * SSM_GROUP)[None, None, :]
    s_idx = jnp.arange(n)[:, None, None]
    lag_t = (lag_c % (n * SSM_GROUP)) // SSM_GROUP
    dst_t = dst_c // SSM_GROUP
    want = jnp.where(lag_c < n * SSM_GROUP, dst_t - s_idx, s_idx - dst_t)
    shift = ((lag_c % SSM_GROUP == dst_c % SSM_GROUP) & (lag_t == want)).astype(F32)
    toe = jnp.einsum('rk,skn->srn', jnp.concatenate(toes, axis=1), shift, precision=hp)
    toe = toe.reshape(size, n * SSM_GROUP)
    toeplitz = spread(toe, copy_tc, (row // SSM_GROUP) % g == (col // SSM_GROUP) % g)
    inj = jnp.stack(injs, axis=3).reshape(size, 2 * 128)
    inject = spread(inj, copy_dp, (row // SSM_GROUP) % g == (col // 128) % g)
    ro = jnp.stack(ros, axis=0).reshape(size, n * SSM_GROUP)
    readout = spread(ro, copy_tc, (row // 128) % g == (col // SSM_GROUP) % g)
    a_rows = jnp.concatenate(a_rows + [jnp.zeros((4, N_SSM_GROUPS * 128), F32)], axis=0)
    return toeplitz, inject, readout, a_rows


def _ret_kernel(q_ref, k_ref, v_ref, g_ref, dec_ref, mask_ref, cd_ref, s0_ref, nw_ref, o_ref, fin_ref,
                kvf_ref, kvb_ref, *, n_chunks):
    hd = HEAD_DIM
    nh = 2

    def local_state(i, _):
        rows = pl.ds(pl.multiple_of(i * RET_CHUNK, RET_CHUNK), RET_CHUNK)
        k = k_ref[rows, :]
        v = v_ref[rows, :].astype(BF16)
        kf = (k * dec_ref[1]).astype(BF16)
        kb = (k * dec_ref[3]).astype(BF16)
        for h in range(nh):
            ls = slice(h * hd, (h + 1) * hd)
            kvf_ref[i, h] = lax.dot_general(kf[:, ls], v[:, ls], TN, preferred_element_type=F32)
            kvb_ref[i, h] = lax.dot_general(kb[:, ls], v[:, ls], TN, preferred_element_type=F32)
        return 0

    lax.fori_loop(0, n_chunks, local_state, 0, unroll=min(8, n_chunks))

    def scan_f(i, s):
        loc = kvf_ref[i]
        kvf_ref[i] = s
        return cd_ref[0] * s + loc

    def scan_b(i, s):
        j = n_chunks - 1 - i
        loc = kvb_ref[j]
        kvb_ref[j] = s
        return cd_ref[1] * s + loc

    fin_ref[0] = lax.fori_loop(0, n_chunks, scan_f, s0_ref[0], unroll=min(4, n_chunks))
    fin_ref[1] = lax.fori_loop(0, n_chunks, scan_b, s0_ref[1], unroll=min(4, n_chunks))

    avg = _group_avg_matrix()

    def outputs(i, _):
        rows = pl.ds(pl.multiple_of(i * RET_CHUNK, RET_CHUNK), RET_CHUNK)
        q = q_ref[rows, :]
        qb = q.astype(BF16)
        kb = k_ref[rows, :].astype(BF16)
        v = v_ref[rows, :].astype(BF16)
        qf = (q * dec_ref[0]).astype(BF16)
        qr = (q * dec_ref[2]).astype(BF16)
        outs = []
        for h in range(nh):
            ls = slice(h * hd, (h + 1) * hd)
            inner = lax.dot_general(qb[:, ls], kb[:, ls], NT, preferred_element_type=F32) * mask_ref[h]
            o = jnp.dot(inner.astype(BF16), v[:, ls], preferred_element_type=F32)
            o += jnp.dot(qf[:, ls], kvf_ref[i, h].astype(BF16), preferred_element_type=F32)
            o += jnp.dot(qr[:, ls], kvb_ref[i, h].astype(BF16), preferred_element_type=F32)
            outs.append(o)
        o = jnp.concatenate(outs, axis=1)
        d = o - _split_dot(o, avg)
        o = d * lax.rsqrt(_split_dot(d * d, avg) + EPS) * nw_ref[...]
        g = g_ref[rows, :]
        o_ref[rows, :] = (g * jax.nn.sigmoid(g) * o).astype(o_ref.dtype)
        return 0

    lax.fori_loop(0, n_chunks, outputs, 0, unroll=min(8, n_chunks))


def _retention(rest, dec, mask, cdec, s0, nw, row_block0, n_req, length):
    n_chunks = length // RET_CHUNK
    hp = RET_HEADS // 2

    def tok(cb):
        return pl.BlockSpec((length, 128), lambda r, p: (row_block0 + r, cb + p))

    state = pl.BlockSpec((None, 2, 2, HEAD_DIM, HEAD_DIM), lambda r, p: (r, 0, p, 0, 0))
    return pl.pallas_call(
        functools.partial(_ret_kernel, n_chunks=n_chunks),
        grid=(n_req, hp),
        in_specs=[tok(C_RQ // 128), tok(C_RK // 128), tok(C_RV // 128), tok(C_RG // 128),
                  pl.BlockSpec((4, RET_CHUNK, 128), lambda r, p: (0, 0, p)),
                  pl.BlockSpec((2, RET_CHUNK, RET_CHUNK), lambda r, p: (p, 0, 0)),
                  pl.BlockSpec((2, 2, HEAD_DIM, HEAD_DIM), lambda r, p: (0, p, 0, 0)),
                  state,
                  pl.BlockSpec((1, 128), lambda r, p: (0, p))],
        out_specs=[pl.BlockSpec((length, 128), lambda r, p: (r, p)), state],
        out_shape=[jax.ShapeDtypeStruct((n_req * length, RET_WIDTH), BF16),
                   jax.ShapeDtypeStruct((n_req, 2, RET_HEADS, HEAD_DIM, HEAD_DIM), F32)],
        scratch_shapes=[pltpu.VMEM((n_chunks, 2, HEAD_DIM, HEAD_DIM), F32),
                        pltpu.VMEM((n_chunks, 2, HEAD_DIM, HEAD_DIM), F32)],
        compiler_params=_cparams(("parallel", "parallel")),
        name="retention",
    )(rest, rest, rest, rest, dec, mask, cdec, s0, nw)


def _retention_tables(decay_logit):
    lg = jax.nn.log_sigmoid(decay_logit.astype(F32))
    idx = jnp.arange(RET_CHUNK, dtype=F32)
    rel = idx[:, None] - idx[None, :]
    d_f = jnp.where(rel >= 0, jnp.exp(lg[0][:, None, None] * jnp.maximum(rel, 0.0)), 0.0)
    d_b = jnp.where(rel <= 0, jnp.exp(lg[1][:, None, None] * jnp.maximum(-rel, 0.0)), 0.0)
    mask = d_f + d_b

    def lanes(t):
        return jnp.repeat(t.T, HEAD_DIM, axis=1)

    dec = jnp.stack([lanes(jnp.exp(lg[0][:, None] * (idx + 1.0))),
                     lanes(jnp.exp(lg[0][:, None] * (RET_CHUNK - 1.0 - idx))),
                     lanes(jnp.exp(lg[1][:, None] * (RET_CHUNK - idx))),
                     lanes(jnp.exp(lg[1][:, None] * idx))], axis=0)
    cdec = jnp.broadcast_to(jnp.exp(lg * RET_CHUNK)[:, :, None, None], (2, RET_HEADS, HEAD_DIM, HEAD_DIM))
    return dec, mask, cdec


def _outproj_kernel(x_ref, rest_ref, ys_ref, atc_ref, atl_ref, rtc_ref, rtl_ref, mod_ref, d_ref, wglu_ref, wout_ref,
                    nw_ref, wr_ref, x1_ref, h2_ref, aff_ref):
    y = ys_ref[...] + rest_ref[...] * d_ref[...]
    y = jax.nn.gelu(y)
    y = y * jax.nn.sigmoid(jnp.dot(y.astype(BF16), wglu_ref[...], preferred_element_type=F32))
    is_ctx = pl.program_id(0) < N_CTX // OUT_TILE
    attn = jnp.where(is_ctx, atc_ref[...], atl_ref[...])
    ret = jnp.where(is_ctx, rtc_ref[...], rtl_ref[...])
    mix = (jnp.dot(y.astype(BF16), wout_ref[0:SSM_WIDTH, :], preferred_element_type=F32)
           + jnp.dot(attn, wout_ref[SSM_WIDTH:SSM_WIDTH + ATT_WIDTH, :], preferred_element_type=F32)
           + jnp.dot(ret, wout_ref[SSM_WIDTH + ATT_WIDTH:, :], preferred_element_type=F32))
    gate1 = mod_ref[:, 2 * D_MODEL:3 * D_MODEL]
    shift2 = mod_ref[:, 3 * D_MODEL:4 * D_MODEL]
    scale2 = mod_ref[:, 4 * D_MODEL:5 * D_MODEL]
    x1 = x_ref[...] + gate1 * mix
    x1_ref[...] = x1
    h2 = (x1 * lax.rsqrt(jnp.mean(x1 * x1, axis=-1, keepdims=True) + EPS) * nw_ref[...]) * (1.0 + scale2) + shift2
    h2_ref[...] = h2.astype(h2_ref.dtype)
    hi = h2.astype(BF16)
    lo = (h2 - hi.astype(F32)).astype(BF16)
    logits = (jnp.dot(hi, wr_ref[0], preferred_element_type=F32)
              + jnp.dot(lo, wr_ref[0], preferred_element_type=F32)
              + jnp.dot(hi, wr_ref[1], preferred_element_type=F32))
    valid = lax.broadcasted_iota(jnp.int32, logits.shape, 1) < N_EXPERTS
    logits = jnp.where(valid, logits, -1e30)
    e = jnp.exp(logits - jnp.max(logits, axis=-1, keepdims=True))
    aff = e / jnp.sum(e, axis=-1, keepdims=True)
    aff_ref[...] = aff.T


def _out_projection(x, rest, ys, attn_ctx, attn_lat, ret_ctx, ret_lat, mod_l, d_row, wglu_bf, wout_bf, nw2, wr_split):
    tile = OUT_TILE
    ctx_tiles = N_CTX // tile
    ctx_blk = lambda i: (jnp.minimum(i, ctx_tiles - 1), 0)
    lat_blk = lambda i: (jnp.maximum(i - ctx_tiles, 0), 0)
    mod_row = lambda i: (jnp.where(i < ctx_tiles, 0, 1 + (i - ctx_tiles) // (DEC_SEQ // tile)), 0, 0)
    return pl.pallas_call(
        _outproj_kernel,
        grid=(N_TOK // tile,),
        in_specs=[pl.BlockSpec((tile, D_MODEL), lambda i: (i, 0)),
                  pl.BlockSpec((tile, SSM_WIDTH), lambda i: (i, C_U // SSM_WIDTH)),
                  pl.BlockSpec((tile, SSM_WIDTH), lambda i: (i, 0)),
                  pl.BlockSpec((tile, ATT_WIDTH), ctx_blk),
                  pl.BlockSpec((tile, ATT_WIDTH), lat_blk),
                  pl.BlockSpec((tile, RET_WIDTH), ctx_blk),
                  pl.BlockSpec((tile, RET_WIDTH), lat_blk),
                  pl.BlockSpec((None, 1, 6 * D_MODEL), mod_row),
                  pl.BlockSpec((1, SSM_WIDTH), lambda i: (0, 0)),
                  pl.BlockSpec((SSM_WIDTH, SSM_WIDTH), lambda i: (0, 0)),
                  pl.BlockSpec((D_MODEL, D_MODEL), lambda i: (0, 0)),
                  pl.BlockSpec((1, D_MODEL), lambda i: (0, 0)),
                  pl.BlockSpec((2, D_MODEL, 128), lambda i: (0, 0, 0))],
        out_specs=[pl.BlockSpec((tile, D_MODEL), lambda i: (i, 0)),
                   pl.BlockSpec((tile, D_MODEL), lambda i: (i, 0)),
                   pl.BlockSpec((128, tile), lambda i: (0, i))],
        out_shape=[jax.ShapeDtypeStruct((N_TOK, D_MODEL), F32),
                   jax.ShapeDtypeStruct((N_TOK, D_MODEL), BF16),
                   jax.ShapeDtypeStruct((128, N_TOK), F32)],
        compiler_params=_cparams(("parallel",)),
        name="out_projection",
    )(x, rest, ys, attn_ctx, attn_lat, ret_ctx, ret_lat, mod_l, d_row, wglu_bf, wout_bf, nw2, wr_split)


def _lane_cumsum(x01):
    rows, n = x01.shape
    r = lax.broadcasted_iota(jnp.int32, (256, 256), 0)
    c = lax.broadcasted_iota(jnp.int32, (256, 256), 1)
    tri = jnp.where(r <= c, 1.0, 0.0).astype(BF16)
    off = jnp.zeros((rows, 1), F32)
    parts = []
    for j in range(n // 256):
        cs = jnp.dot(x01[:, 256 * j:256 * (j + 1)].astype(BF16), tri, preferred_element_type=F32) + off
        parts.append(cs)
        off = cs[:, 255:256]
    return jnp.concatenate(parts, axis=1)


def _count(m):
    return jnp.sum(jnp.where(m, 1.0, 0.0), axis=1, keepdims=True)


def _route_kernel(aff_ref, slot_ref, gate_ref, offs_ref, *, cap, seg):
    n_seg = aff_ref.shape[1] // seg
    segs = [slice(i * seg, (i + 1) * seg) for i in range(n_seg)]
    tiny = float(jnp.finfo(jnp.float32).tiny)

    def step(_, bounds):
        out = []
        for i in range(n_seg):
            lo, hi = bounds[2 * i], bounds[2 * i + 1]
            mid = jnp.where(lo > 0.0, jnp.sqrt(lo) * jnp.sqrt(hi), jnp.maximum(hi * (2.0 ** -16), tiny))
            mid = jnp.minimum(jnp.maximum(mid, lo), hi)
            ok = _count(aff_ref[:, segs[i]] >= mid) >= cap
            out += [jnp.where(ok, mid, lo), jnp.where(ok, hi, mid)]
        return tuple(out)

    init = (jnp.zeros((N_EXPERTS, 1), F32), jnp.full((N_EXPERTS, 1), 2.0, F32)) * n_seg
    bounds = lax.fori_loop(0, ROUTE_ITERS, step, init)

    for i in range(n_seg):
        a = aff_ref[:, segs[i]]
        lo, hi = bounds[2 * i], bounds[2 * i + 1]
        above = a >= hi
        band = (a >= lo) & (a < hi)
        sel = above | (band & (_lane_cumsum(jnp.where(band, 1.0, 0.0)) <= cap - _count(above)))
        taken = _lane_cumsum(jnp.where(sel, 1.0, 0.0))
        slot_ref[:, segs[i]] = jnp.where(sel, taken - 1.0, -1.0).astype(jnp.int32)
        gate_ref[:, segs[i]] = jnp.where(sel, a, 0.0)
    if n_seg == 1:
        n_off = seg // OFFS_STEP
        before = [jnp.zeros((N_EXPERTS, 1), F32)] + [taken[:, OFFS_STEP * k - 1:OFFS_STEP * k] for k in range(1, n_off)]
        before.append(jnp.zeros((N_EXPERTS, 128 - n_off), F32))
        offs_ref[...] = jnp.concatenate(before, axis=1).astype(jnp.int32)
    else:
        offs_ref[...] = jnp.zeros(offs_ref.shape, jnp.int32)


def _route(aff_t, col_block0, n_blocks, width, seg, cap):
    return pl.pallas_call(
        functools.partial(_route_kernel, cap=cap, seg=seg),
        grid=(n_blocks,),
        in_specs=[pl.BlockSpec((N_EXPERTS, width), lambda i: (0, col_block0 + i))],
        out_specs=[pl.BlockSpec((N_EXPERTS, width), lambda i: (0, i))] * 2
                  + [pl.BlockSpec((N_EXPERTS, 128), lambda i: (0, i))],
        out_shape=[jax.ShapeDtypeStruct((N_EXPERTS, n_blocks * width), jnp.int32),
                   jax.ShapeDtypeStruct((N_EXPERTS, n_blocks * width), F32),
                   jax.ShapeDtypeStruct((N_EXPERTS, n_blocks * 128), jnp.int32)],
        compiler_params=_cparams(("parallel",)),
        name="route",
    )(aff_t)


def _gather_lat_kernel(offs_ref, slot_ref, gate_ref, h_ref, o_ref, g_ref, acc_ref, gacc_ref):
    r = pl.program_id(0)
    e = pl.program_id(1)
    win = GATHER_BLOCK + 16
    acc_ref[...] = jnp.zeros(acc_ref.shape, F32)
    gacc_ref[...] = jnp.zeros(gacc_ref.shape, F32)
    rows = lax.broadcasted_iota(jnp.int32, (win, GATHER_BLOCK), 0)
    for b in range(DEC_SEQ // GATHER_BLOCK):
        off = offs_ref[e, r * 128 + b * (GATHER_BLOCK // OFFS_STEP)]
        base = pl.multiple_of((off // 8) * 8, 8)
        toks = slice(b * GATHER_BLOCK, (b + 1) * GATHER_BLOCK)
        hit = (rows + base) == slot_ref[pl.ds(e, 1), toks]
        onehot = jnp.where(hit, 1.0, 0.0).astype(BF16)
        acc_ref[pl.ds(base, win), :] += jnp.dot(onehot, h_ref[toks, :], preferred_element_type=F32)
        gacc_ref[pl.ds(base, win), :] += jnp.sum(jnp.where(hit, gate_ref[pl.ds(e, 1), toks], 0.0), axis=1, keepdims=True)
    o_ref[...] = acc_ref[0:CAP_LAT, :].astype(o_ref.dtype)
    g_ref[...] = gacc_ref[0:CAP_LAT, :]


def _gather_lat(offs, slot, gate, h2):
    rows = pl.BlockSpec((N_EXPERTS, DEC_SEQ), lambda r, e, offs: (0, r))
    acc_rows = CAP_LAT + GATHER_BLOCK + 16
    return pl.pallas_call(
        _gather_lat_kernel,
        grid_spec=pltpu.PrefetchScalarGridSpec(
            num_scalar_prefetch=1,
            grid=(DEC_BATCH, N_EXPERTS),
            in_specs=[rows, rows,
                      pl.BlockSpec((DEC_SEQ, D_MODEL), lambda r, e, offs: (N_CTX // DEC_SEQ + r, 0))],
            out_specs=[pl.BlockSpec((None, CAP_LAT, D_MODEL), lambda r, e, offs: (e, r, 0)),
                       pl.BlockSpec((None, CAP_LAT, 1), lambda r, e, offs: (e, r, 0))],
            scratch_shapes=[pltpu.VMEM((acc_rows, D_MODEL), F32), pltpu.VMEM((acc_rows, 1), F32)]),
        out_shape=[jax.ShapeDtypeStruct((N_EXPERTS, DEC_BATCH * CAP_LAT, D_MODEL), BF16),
                   jax.ShapeDtypeStruct((N_EXPERTS, DEC_BATCH * CAP_LAT, 1), F32)],
        compiler_params=_cparams(("parallel", "arbitrary")),
        name="gather_lat",
    )(offs, slot, gate, h2)


def _ctx_onehot(slot):
    rows = lax.broadcasted_iota(jnp.int32, (CAP_CTX, SEQ), 0)
    hits = [rows == slot[e:e + 1, :] for e in range(N_EXPERTS)]
    onehot = jnp.concatenate([jnp.where(h, 1.0, 0.0) for h in hits], axis=0).astype(BF16)
    return onehot, hits


def _gather_ctx_kernel(slot_ref, gate_ref, h_ref, o_ref, g_ref):
    onehot, hits = _ctx_onehot(slot_ref[...])
    xs = jnp.dot(onehot, h_ref[...], preferred_element_type=F32).astype(o_ref.dtype)
    gate = gate_ref[...]
    for e in range(N_EXPERTS):
        o_ref[e] = xs[e * CAP_CTX:(e + 1) * CAP_CTX]
        g_ref[e] = jnp.sum(jnp.where(hits[e], gate[e:e + 1, :], 0.0), axis=1, keepdims=True)


def _gather_ctx(slot, gate, h2):
    rows = pl.BlockSpec((N_EXPERTS, SEQ), lambda r: (0, r))
    return pl.pallas_call(
        _gather_ctx_kernel,
        grid=(BATCH,),
        in_specs=[rows, rows, pl.BlockSpec((SEQ, D_MODEL), lambda r: (r, 0))],
        out_specs=[pl.BlockSpec((N_EXPERTS, CAP_CTX, D_MODEL), lambda r: (0, r, 0)),
                   pl.BlockSpec((N_EXPERTS, CAP_CTX, 1), lambda r: (0, r, 0))],
        out_shape=[jax.ShapeDtypeStruct((N_EXPERTS, BATCH * CAP_CTX, D_MODEL), BF16),
                   jax.ShapeDtypeStruct((N_EXPERTS, BATCH * CAP_CTX, 1), F32)],
        compiler_params=_cparams(("parallel",)),
        name="gather_ctx",
    )(slot, gate, h2)


def _ffn_kernel(xc_ref, xl_ref, gc_ref, gl_ref, wg_ref, wu_ref, wd_ref, yc_ref, yl_ref, accc_ref, accl_ref):
    f = pl.program_id(1)
    wg = wg_ref[...].astype(BF16)
    wu = wu_ref[...].astype(BF16)
    wd = wd_ref[...].astype(BF16)

    def part(x_ref, gate_ref, acc_ref, y_ref):
        x = x_ref[...]
        a = jnp.dot(x, wg, preferred_element_type=F32)
        up = jnp.dot(x, wu, preferred_element_type=F32)
        mid = (a * jax.nn.sigmoid(a) * up).astype(BF16)
        y = jnp.dot(mid, wd, preferred_element_type=F32)

        @pl.when(f == 0)
        def _():
            acc_ref[...] = y

        @pl.when(f > 0)
        def _():
            acc_ref[...] += y

        @pl.when(f == pl.num_programs(1) - 1)
        def _():
            y_ref[...] = (acc_ref[...] * gate_ref[...]).astype(y_ref.dtype)

    part(xc_ref, gc_ref, accc_ref, yc_ref)
    part(xl_ref, gl_ref, accl_ref, yl_ref)


def _expert_ffn(xs_ctx, xs_lat, gs_ctx, gs_lat, w_gate, w_up, w_down, layer):
    tf = 512
    nc, nl = xs_ctx.shape[1], xs_lat.shape[1]
    return pl.pallas_call(
        _ffn_kernel,
        grid=(N_EXPERTS, EXPERT_FF // tf),
        in_specs=[pl.BlockSpec((None, nc, D_MODEL), lambda e, f: (e, 0, 0)),
                  pl.BlockSpec((None, nl, D_MODEL), lambda e, f: (e, 0, 0)),
                  pl.BlockSpec((None, nc, 1), lambda e, f: (e, 0, 0)),
                  pl.BlockSpec((None, nl, 1), lambda e, f: (e, 0, 0)),
                  pl.BlockSpec((None, None, D_MODEL, tf), lambda e, f: (layer, e, 0, f)),
                  pl.BlockSpec((None, None, D_MODEL, tf), lambda e, f: (layer, e, 0, f)),
                  pl.BlockSpec((None, None, tf, D_MODEL), lambda e, f: (layer, e, f, 0))],
        out_specs=[pl.BlockSpec((None, nc, D_MODEL), lambda e, f: (e, 0, 0)),
                   pl.BlockSpec((None, nl, D_MODEL), lambda e, f: (e, 0, 0))],
        out_shape=[jax.ShapeDtypeStruct(xs_ctx.shape, BF16), jax.ShapeDtypeStruct(xs_lat.shape, BF16)],
        scratch_shapes=[pltpu.VMEM((nc, D_MODEL), F32), pltpu.VMEM((nl, D_MODEL), F32)],
        compiler_params=_cparams(("parallel", "arbitrary")),
        name="expert_ffn",
    )(xs_ctx, xs_lat, gs_ctx, gs_lat, w_gate, w_up, w_down)


def _scatter_lat_kernel(offs_ref, slot_ref, y_ref, x_ref, mod_ref, o_ref):
    r = pl.program_id(0)
    t = pl.program_id(1)
    slot_t = slot_ref[...].astype(F32).T
    lane = lax.broadcasted_iota(jnp.int32, (OFFS_STEP, SCATTER_WINDOW), 1)
    ffn = jnp.zeros((OFFS_STEP, D_MODEL), F32)
    for e in range(N_EXPERTS):
        off = offs_ref[e, r * 128 + t]
        base = pl.multiple_of(jnp.minimum((off // 16) * 16, CAP_LAT - SCATTER_WINDOW), 16)
        onehot = jnp.where((lane + base).astype(F32) == slot_t[:, e:e + 1], 1.0, 0.0).astype(BF16)
        ffn += jnp.dot(onehot, y_ref[e, pl.ds(base, SCATTER_WINDOW), :], preferred_element_type=F32)
    o_ref[...] = x_ref[...] + mod_ref[:, 5 * D_MODEL:6 * D_MODEL] * ffn


def _scatter_lat(offs, slot, ys, x, mod_l):
    tt = OFFS_STEP
    nt = DEC_SEQ // tt
    blk0 = N_CTX // tt
    tok = pl.BlockSpec((tt, D_MODEL), lambda r, t, offs: (blk0 + r * nt + t, 0))
    return pl.pallas_call(
        _scatter_lat_kernel,
        grid_spec=pltpu.PrefetchScalarGridSpec(
            num_scalar_prefetch=1,
            grid=(DEC_BATCH, nt),
            in_specs=[pl.BlockSpec((N_EXPERTS, tt), lambda r, t, offs: (0, r * nt + t)),
                      pl.BlockSpec((N_EXPERTS, CAP_LAT, D_MODEL), lambda r, t, offs: (0, r, 0)),
                      tok,
                      pl.BlockSpec((None, 1, 6 * D_MODEL), lambda r, t, offs: (1 + r, 0, 0))],
            out_specs=tok),
        out_shape=jax.ShapeDtypeStruct(x.shape, F32),
        input_output_aliases={3: 0},
        compiler_params=_cparams(("parallel", "parallel")),
        name="scatter_lat",
    )(offs, slot, ys, x, mod_l)


def _scatter_ctx_kernel(slot_ref, y_ref, x_ref, mod_ref, o_ref):
    onehot, _ = _ctx_onehot(slot_ref[...])
    y = jnp.concatenate([y_ref[e] for e in range(N_EXPERTS)], axis=0)
    ffn = lax.dot_general(onehot, y, TN, preferred_element_type=F32)
    o_ref[...] = x_ref[...] + mod_ref[:, 5 * D_MODEL:6 * D_MODEL] * ffn


def _scatter_ctx(slot, ys, x, mod_l):
    tok = pl.BlockSpec((SEQ, D_MODEL), lambda r: (r, 0))
    return pl.pallas_call(
        _scatter_ctx_kernel,
        grid=(BATCH,),
        in_specs=[pl.BlockSpec((N_EXPERTS, SEQ), lambda r: (0, r)),
                  pl.BlockSpec((N_EXPERTS, CAP_CTX, D_MODEL), lambda r: (0, r, 0)),
                  tok,
                  pl.BlockSpec((None, 1, 6 * D_MODEL), lambda r: (0, 0, 0))],
        out_specs=tok,
        out_shape=jax.ShapeDtypeStruct(x.shape, F32),
        input_output_aliases={2: 0},
        compiler_params=_cparams(("parallel",)),
        name="scatter_ctx",
    )(slot, ys, x, mod_l)


def _final_norm_kernel(x_ref, w_ref, o_ref):
    x = x_ref[...]
    o_ref[...] = x * lax.rsqrt(jnp.mean(x * x, axis=-1, keepdims=True) + EPS) * w_ref[...]


def _final_norm(x, w):
    return pl.pallas_call(
        _final_norm_kernel,
        grid=(N_TILES,),
        in_specs=[pl.BlockSpec((TILE, D_MODEL), lambda i: (i, 0)), pl.BlockSpec((1, D_MODEL), lambda i: (0, 0))],
        out_specs=pl.BlockSpec((TILE, D_MODEL), lambda i: (i, 0)),
        out_shape=jax.ShapeDtypeStruct(x.shape, F32),
        compiler_params=_cparams(("parallel",)),
        name="final_norm",
    )(x, w)


def _rope_tables():
    rows = DEC_SEQ // GRID_W
    row = jnp.repeat(jnp.arange(rows, dtype=F32), GRID_W)
    col = jnp.tile(jnp.arange(GRID_W, dtype=F32), rows)
    n_freq = HEAD_DIM // 4
    inv_freq = ROPE_THETA ** (-jnp.arange(n_freq, dtype=F32) / n_freq)
    ang = jnp.concatenate([row[:, None] * inv_freq, col[:, None] * inv_freq], axis=-1)
    cos, sin = jnp.cos(ang), jnp.sin(ang)
    cos_t = jnp.tile(jnp.concatenate([cos, cos], -1), (1, 128 // HEAD_DIM))
    sin_t = jnp.tile(jnp.concatenate([-sin, sin], -1), (1, 128 // HEAD_DIM))
    cos_t = jnp.concatenate([jnp.ones((TILE, 128), F32), cos_t], axis=0)
    sin_t = jnp.concatenate([jnp.zeros((TILE, 128), F32), sin_t], axis=0)
    return cos_t, sin_t


def _s5_initial_rows(state_ssm):
    st = state_ssm.astype(F32)
    re, im = st[..., 0], st[..., 1]
    both = jnp.stack([jnp.concatenate([re, im], -1), jnp.concatenate([im, re], -1)], axis=3)
    return both.transpose(1, 0, 2, 3, 4, 5).reshape(DEPTH, 4 * DEC_BATCH, N_SSM_GROUPS * 128)


def kernel(x_prompt, x_sample, cache_k, cache_v, state_ssm, state_ret, c, c_ctx, w_mod, b_mod, norm1_w, norm2_w, w_in, w_out, qn_w, kn_w, ssm_lambda_re, ssm_lambda_im, ssm_b_re, ssm_b_im, ssm_c_re, ssm_c_im, ssm_log_dt, ssm_d, ssm_w_glu, ret_decay_logit, ret_norm_w, w_router, w_gate, w_up, w_down, final_norm_w):
    x = jnp.concatenate([x_prompt.reshape(N_CTX, D_MODEL), x_sample.reshape(N_LAT, D_MODEL)], axis=0)
    cond_t = jnp.zeros((D_MODEL, 8), F32).at[:, 0].set(c_ctx).at[:, 1:1 + DEC_BATCH].set(c.T)
    mod = _modulation(cond_t, w_mod, b_mod).reshape(DEPTH, 8, 1, 6 * D_MODEL)
    cos_t, sin_t = _rope_tables()
    zero_ret = jnp.zeros((BATCH, 2, RET_HEADS, HEAD_DIM, HEAD_DIM), F32)
    ctx_blocks = N_CTX // DEC_SEQ

    w_in_bf, w_out_bf, w_glu_bf = w_in.astype(BF16), w_out.astype(BF16), ssm_w_glu.astype(BF16)
    s5_toe, s5_inj, s5_ro, s5_a = jax.vmap(_s5_matrices)(ssm_lambda_re, ssm_lambda_im, ssm_b_re, ssm_b_im,
                                                         ssm_c_re, ssm_c_im, ssm_log_dt)
    s5_h0 = _s5_initial_rows(state_ssm)
    ret_dec, ret_mask, ret_cdec = jax.vmap(_retention_tables)(ret_decay_logit)
    wr = jnp.pad(w_router.astype(F32), ((0, 0), (0, 0), (0, 128 - N_EXPERTS)))
    wr_hi = wr.astype(BF16)
    wr_split = jnp.stack([wr_hi, (wr - wr_hi.astype(F32)).astype(BF16)], axis=1)
    qn_t, kn_t = jnp.tile(qn_w, (1, 2)), jnp.tile(kn_w, (1, 2))
    cache_kv = jnp.concatenate([cache_k.reshape(DEC_BATCH, DEPTH, PAST_LEN, KV_WIDTH),
                                cache_v.reshape(DEC_BATCH, DEPTH, PAST_LEN, KV_WIDTH)], axis=-1).astype(BF16)

    ks, vs, ss, rs = [], [], [], []
    for l in range(DEPTH):
        mod_l = mod[l]
        q, kv, ub, rest = _in_projection(x, mod_l, norm1_w[l].reshape(1, -1), w_in_bf[l],
                                     qn_t[l].reshape(1, -1), kn_t[l].reshape(1, -1), cos_t, sin_t)
        ks.append(rest[:N_CTX, C_K:C_K + KV_WIDTH].reshape(BATCH, SEQ, N_KV_HEADS, HEAD_DIM))
        vs.append(rest[:N_CTX, C_V:C_V + KV_WIDTH].reshape(BATCH, SEQ, N_KV_HEADS, HEAD_DIM))

        kv_ctx = kv[:N_CTX].reshape(BATCH, SEQ, 2 * KV_WIDTH)
        kv_lat = jnp.concatenate([kv[N_CTX:].reshape(DEC_BATCH, DEC_SEQ, 2 * KV_WIDTH), cache_kv[:, l]], axis=1)
        attn_ctx = _attention(q, kv_ctx, 0, BATCH, SEQ, SEQ)
        attn_lat = _attention(q, kv_lat, N_CTX // LAT_TQ, DEC_BATCH, DEC_SEQ, LAT_TQ)

        y_loc, states = _s5_local(ub.reshape(S5_ROWS, S5_CHUNK * SSM_WIDTH), s5_toe, s5_inj, l)
        hf, hb, fin_f, fin_b = _s5_scan(states, s5_a[l], s5_h0[l])
        ys = _s5_out(y_loc, hf, hb, s5_ro, l).reshape(N_TOK, SSM_WIDTH)
        fin = jnp.stack([fin_f, fin_b], axis=1).reshape(BATCH, 2, N_SSM_GROUPS, 2, SSM_STATE)
        ss.append(fin.transpose(0, 1, 2, 4, 3))

        nw_ret = ret_norm_w[l].reshape(1, -1)
        ret_ctx, fin_ret = _retention(rest, ret_dec[l], ret_mask[l], ret_cdec[l], zero_ret, nw_ret, 0, BATCH, SEQ)
        ret_lat, _ = _retention(rest, ret_dec[l], ret_mask[l], ret_cdec[l], state_ret[:, l].astype(F32), nw_ret,
                                ctx_blocks, DEC_BATCH, DEC_SEQ)
        rs.append(fin_ret)

        x1, h2, aff_t = _out_projection(x, rest, ys, attn_ctx, attn_lat, ret_ctx, ret_lat, mod_l,
                                        ssm_d[l].reshape(1, -1), w_glu_bf[l], w_out_bf[l],
                                        norm2_w[l].reshape(1, -1), wr_split[l])

        slot_ctx, gate_ctx, _ = _route(aff_t, 0, 1, N_CTX, SEQ, CAP_CTX)
        slot_lat, gate_lat, offs_lat = _route(aff_t, N_CTX // DEC_SEQ, DEC_BATCH, DEC_SEQ, DEC_SEQ, CAP_LAT)
        xs_ctx, gs_ctx = _gather_ctx(slot_ctx, gate_ctx, h2)
        xs_lat, gs_lat = _gather_lat(offs_lat, slot_lat, gate_lat, h2)
        y_ctx, y_lat = _expert_ffn(xs_ctx, xs_lat, gs_ctx, gs_lat, w_gate, w_up, w_down, l)
        x = _scatter_ctx(slot_ctx, y_ctx, x1, mod_l)
        x = _scatter_lat(offs_lat, slot_lat, y_lat, x, mod_l)

    y = _final_norm(x, final_norm_w.reshape(1, -1))
    y_prompt = y[:N_CTX].reshape(BATCH, SEQ, D_MODEL)
    y_sample = y[N_CTX:].reshape(DEC_BATCH, DEC_SEQ, D_MODEL)
    return (y_prompt, y_sample, jnp.stack(ks, axis=1), jnp.stack(vs, axis=1),
            jnp.stack(ss, axis=1), jnp.stack(rs, axis=1))
```
